```python
import math
import jax, jax.numpy as jnp
from jax import lax
import numpy as np

D_MODEL = 1024
BATCH = 4
SEQ = 4096
DEPTH = 1

ROPE_THETA = 500000.0
NORM_EPS = 1e-6
Q_BLOCK = 128
DA_HEADS = 8
DA_HEAD_DIM = 64
DA_ROT = DA_HEAD_DIM // 4
DA_QK_W = DA_HEADS * 2 * DA_HEAD_DIM
DA_V_W = DA_HEADS * 2 * DA_HEAD_DIM
MLA_HEADS = 8
MLA_Q_LORA = 768
MLA_KV_LORA = 512
MLA_NOPE = 128
MLA_ROPE = 64
MLA_V = 128
N_EXPERTS = 32
TOP_K = 4
D_FF_EXPERT = 1024
SWIGLU_ALPHA = 1.702
SWIGLU_LIMIT = 7.0
EXPERT_BLOCK = 128
IN_SPLITS = (DA_QK_W, DA_QK_W, DA_V_W, MLA_Q_LORA, MLA_KV_LORA, MLA_ROPE, 2 * D_MODEL)
IN_WIDTH = sum(IN_SPLITS)
IN_OFFSETS = [int(o) for o in np.cumsum(IN_SPLITS)[:-1]]

kernel_name = "hybrid_diffattn_mla_gated_moe_encoder"


def rms_norm(x, g):
    xf = x.astype(jnp.float32)
    y = xf * lax.rsqrt(jnp.mean(xf * xf, axis=-1, keepdims=True) + NORM_EPS)
    return (y * g.astype(jnp.float32)).astype(x.dtype)


def apply_rope(x, positions, rot_dim):
    half = rot_dim // 2
    inv = ROPE_THETA ** (-jnp.arange(0, rot_dim, 2, dtype=jnp.float32) / rot_dim)
    ang = positions.astype(jnp.float32)[..., None] * inv
    shape = ang.shape[:2] + (1,) * (x.ndim - 3) + (half,)
    c = jnp.cos(ang).reshape(shape).astype(x.dtype)
    s = jnp.sin(ang).reshape(shape).astype(x.dtype)
    x1 = x[..., :half]
    x2 = x[..., half:rot_dim]
    return jnp.concatenate([x1 * c - x2 * s, x2 * c + x1 * s, x[..., rot_dim:]], axis=-1)


def _to_blocks(t):
    b, s = t.shape[:2]
    t = t.reshape((b, s // Q_BLOCK, Q_BLOCK) + t.shape[2:])
    return jnp.moveaxis(t, 1, 0)


def _from_blocks(t):
    t = jnp.moveaxis(t, 0, 1)
    return t.reshape((t.shape[0], t.shape[1] * t.shape[2]) + t.shape[3:])


def diff_attention(q, k, v, lam, lam_init, g_subln):
    scale = DA_HEAD_DIM ** -0.5

    def one_block(qb):
        s = jnp.einsum('bqhmd,bkhmd->bhmqk', qb, k).astype(jnp.float32) * scale
        p = jax.nn.softmax(s, axis=-1)
        a = (p[:, :, 0] - lam * p[:, :, 1]).astype(v.dtype)
        return jnp.einsum('bhqk,bkhe->bqhe', a, v)

    o = _from_blocks(lax.map(one_block, _to_blocks(q)))
    o = rms_norm(o, g_subln) * (1.0 - lam_init)
    return o.reshape(o.shape[:2] + (-1,))


def mla_attention(c_q, c_kv, k_rope, positions, g_q, g_kv, w_uq, w_ukv):
    b, s = c_q.shape[:2]
    q = (rms_norm(c_q, g_q) @ w_uq).reshape(b, s, MLA_HEADS, MLA_NOPE + MLA_ROPE)
    q_nope = q[..., :MLA_NOPE]
    q_rope = apply_rope(q[..., MLA_NOPE:], positions, MLA_ROPE)
    kv = (rms_norm(c_kv, g_kv) @ w_ukv).reshape(b, s, MLA_HEADS, MLA_NOPE + MLA_V)
    k_nope = kv[..., :MLA_NOPE]
    v = kv[..., MLA_NOPE:]
    k_r = apply_rope(k_rope, positions, MLA_ROPE)
    scale = (MLA_NOPE + MLA_ROPE) ** -0.5

    def one_block(qs):
        qn, qr = qs
        sc = (jnp.einsum('bqhd,bkhd->bhqk', qn, k_nope)
              + jnp.einsum('bqhr,bkr->bhqk', qr, k_r))
        p = jax.nn.softmax(sc.astype(jnp.float32) * scale, axis=-1).astype(v.dtype)
        return jnp.einsum('bhqk,bkhe->bqhe', p, v)

    o = _from_blocks(lax.map(one_block, (_to_blocks(q_nope), _to_blocks(q_rope))))
    return o.reshape(b, s, MLA_HEADS * MLA_V)


def moe_ffn(h, w_router, b_router, w_gate, b_gate, w_up, b_up, w_down, b_down):
    b, s, d = h.shape
    t = b * s
    hf = h.reshape(t, d)
    logits = (hf @ w_router + b_router).astype(jnp.float32)
    top_val, top_idx = lax.top_k(logits, TOP_K)
    top_w = jax.nn.softmax(top_val, axis=-1).astype(h.dtype)
    n_assign = t * TOP_K
    e_flat = top_idx.reshape(-1).astype(jnp.int32)
    w_flat = top_w.reshape(-1)
    tok_flat = jnp.arange(n_assign, dtype=jnp.int32) // TOP_K
    order = jnp.argsort(e_flat)
    e_sorted = e_flat[order]
    counts = jnp.bincount(e_flat, length=N_EXPERTS).astype(jnp.int32)
    padded = (counts + EXPERT_BLOCK - 1) // EXPERT_BLOCK * EXPERT_BLOCK
    padded_end = jnp.cumsum(padded)
    padded_start = padded_end - padded
    start = jnp.cumsum(counts) - counts
    dest = padded_start[e_sorted] + jnp.arange(n_assign, dtype=jnp.int32) - start[e_sorted]
    n_blocks = -(-n_assign // EXPERT_BLOCK) + N_EXPERTS
    n_rows = n_blocks * EXPERT_BLOCK
    row_tok = jnp.zeros((n_rows,), jnp.int32).at[dest].set(tok_flat[order])
    row_w = jnp.zeros((n_rows,), h.dtype).at[dest].set(w_flat[order])
    block_e = jnp.minimum(
        jnp.searchsorted(padded_end, jnp.arange(n_blocks, dtype=jnp.int32) * EXPERT_BLOCK,
                         side='right'), N_EXPERTS - 1)
    xs = hf[row_tok].reshape(n_blocks, EXPERT_BLOCK, d)

    def expert_block(args):
        xb, e = args
        gate = jnp.minimum(xb @ w_gate[e] + b_gate[e], SWIGLU_LIMIT)
        up = jnp.clip(xb @ w_up[e] + b_up[e], -SWIGLU_LIMIT, SWIGLU_LIMIT)
        act = (up + 1.0) * (gate * jax.nn.sigmoid(SWIGLU_ALPHA * gate))
        return act @ w_down[e] + b_down[e]

    ys = lax.map(expert_block, (xs, block_e)).reshape(n_rows, d)
    out = jax.ops.segment_sum(ys * row_w[:, None], row_tok, num_segments=t)
    return out.reshape(b, s, d)


def setup_inputs(seed: int = 0) -> dict:
    key = jax.random.key(seed)
    ks = iter(jax.random.split(key, 40))
    L = DEPTH

    def nrm(shape, scale):
        return jax.random.normal(next(ks), shape, jnp.float32) * scale

    def gain(shape):
        return 1.0 + nrm(shape, 0.02)

    x = jax.random.normal(next(ks), (BATCH, SEQ, D_MODEL), jnp.float32)
    offs = jax.random.randint(next(ks), (BATCH, 1), 0, SEQ, dtype=jnp.int32)
    positions = jnp.arange(SEQ, dtype=jnp.int32)[None, :] + offs
    E, F, D = N_EXPERTS, D_FF_EXPERT, D_MODEL
    return {
        "x": x,
        "positions": positions,
        "g_mix": gain((L, D)),
        "w_in": nrm((L, D, IN_WIDTH), D ** -0.5),
        "lam_q1": nrm((L, DA_HEAD_DIM), 0.1),
        "lam_k1": nrm((L, DA_HEAD_DIM), 0.1),
        "lam_q2": nrm((L, DA_HEAD_DIM), 0.1),
        "lam_k2": nrm((L, DA_HEAD_DIM), 0.1),
        "g_subln": gain((L, 2 * DA_HEAD_DIM)),
        "g_q": gain((L, MLA_Q_LORA)),
        "g_kv": gain((L, MLA_KV_LORA)),
        "w_uq": nrm((L, MLA_Q_LORA, MLA_HEADS * (MLA_NOPE + MLA_ROPE)), MLA_Q_LORA ** -0.5),
        "w_ukv": nrm((L, MLA_KV_LORA, MLA_HEADS * (MLA_NOPE + MLA_V)), MLA_KV_LORA ** -0.5),
        "w_o_diff": nrm((L, DA_V_W, D), DA_V_W ** -0.5),
        "w_o_mla": nrm((L, MLA_HEADS * MLA_V, D), (MLA_HEADS * MLA_V) ** -0.5),
        "b_gates": nrm((L, 2 * D), 0.1),
        "w_out": nrm((L, D, D), D ** -0.5),
        "g_ffn": gain((L, D)),
        "w_router": nrm((L, D, E), D ** -0.5),
        "b_router": nrm((L, E), 0.01),
        "w_gate": nrm((L, E, D, F), D ** -0.5),
        "b_gate": nrm((L, E, F), 0.02),
        "w_up": nrm((L, E, D, F), D ** -0.5),
        "b_up": nrm((L, E, F), 0.02),
        "w_down": nrm((L, E, F, D), F ** -0.5),
        "b_down": nrm((L, E, D), 0.02),
        "g_final": gain((D,)),
    }


def reference(x, positions, g_mix, w_in, lam_q1, lam_k1, lam_q2, lam_k2, g_subln, g_q, g_kv,
              w_uq, w_ukv, w_o_diff, w_o_mla, b_gates, w_out, g_ffn, w_router, b_router,
              w_gate, b_gate, w_up, b_up, w_down, b_down, g_final):
    b, s, d = x.shape
    for l in range(DEPTH):
        h = rms_norm(x, g_mix[l])
        z = h @ w_in[l]
        z_qa, z_ka, z_va, z_cq, z_ckv, z_kr, z_g = jnp.split(z, IN_OFFSETS, axis=-1)

        q_a = apply_rope(z_qa.reshape(b, s, DA_HEADS, 2, DA_HEAD_DIM), positions, DA_ROT)
        k_a = apply_rope(z_ka.reshape(b, s, DA_HEADS, 2, DA_HEAD_DIM), positions, DA_ROT)
        v_a = z_va.reshape(b, s, DA_HEADS, 2 * DA_HEAD_DIM)
        lam_init = 0.8 - 0.6 * math.exp(-0.3 * l)
        lam = (jnp.exp(jnp.sum(lam_q1[l].astype(jnp.float32) * lam_k1[l].astype(jnp.float32)))
               - jnp.exp(jnp.sum(lam_q2[l].astype(jnp.float32) * lam_k2[l].astype(jnp.float32)))
               + lam_init)
        o_a = diff_attention(q_a, k_a, v_a, lam, lam_init, g_subln[l])

        o_b = mla_attention(z_cq, z_ckv, z_kr, positions, g_q[l], g_kv[l], w_uq[l], w_ukv[l])

        gates = jax.nn.sigmoid(z_g + b_gates[l]).reshape(b, s, 2, d)
        merged = gates[:, :, 0] * (o_a @ w_o_diff[l]) + gates[:, :, 1] * (o_b @ w_o_mla[l])
        x = x + merged @ w_out[l]

        x = x + moe_ffn(rms_norm(x, g_ffn[l]), w_router[l], b_router[l], w_gate[l], b_gate[l],
                        w_up[l], b_up[l], w_down[l], b_down[l])
    return rms_norm(x, g_final)
```

```python
import functools
import math

import jax
import jax.numpy as jnp
from jax import lax
from jax.experimental import pallas as pl
from jax.experimental.pallas import tpu as pltpu

D_MODEL = 1024
ROPE_THETA = 500000.0
NORM_EPS = 1e-6
DA_HEADS = 8
DA_HEAD_DIM = 64
DA_ROT = DA_HEAD_DIM // 4
MLA_HEADS = 8
MLA_Q_LORA = 768
MLA_KV_LORA = 512
MLA_NOPE = 128
MLA_ROPE = 64
MLA_V = 128
N_EXPERTS = 32
TOP_K = 4
D_FF = 1024
SWIGLU_ALPHA = 1.702
SWIGLU_LIMIT = 7.0
LAM_INIT = 0.8 - 0.6 * math.exp(-0.3 * 0)

LANES = 128
MLA_QK_PAD = 256
TOKEN_TILE = 256
Q_TILE = 256
ROW_BLOCK = 256
VMEM_LIMIT = 56 * 1024 * 1024

_NT = (((1,), (1,)), ((), ()))


def _rms(x, g):
    return x * lax.rsqrt(jnp.mean(x * x, axis=-1, keepdims=True) + NORM_EPS) * g


def _dot(a, b):
    return jnp.dot(a, b, preferred_element_type=jnp.float32)


def _rope_tables(pos, freq, m_lo, m_hi):
    ang = pos * freq
    c = jnp.cos(ang)
    s = jnp.sin(ang)
    return c, -s * m_lo, s * m_hi


def _rope_block(xb, tables, half):
    c, s_lo, s_hi = tables
    return (xb * c + pltpu.roll(xb, LANES - half, 1) * s_lo
            + pltpu.roll(xb, half, 1) * s_hi)


def _proj_kernel(x_ref, pos_ref, gmix_ref, wqa_ref, wka_ref, wva_ref, wcq_ref, wckr_ref,
                 wg_ref, bg_ref, gq_ref, gkv_ref, wuq_ref, wkn_ref, wv_ref,
                 fd_ref, fm_ref,
                 qa_ref, ka_ref, va_ref, gates_ref, qm_ref, km_ref, vm_ref):
    hb = _rms(x_ref[...], gmix_ref[...]).astype(jnp.bfloat16)
    pos = pos_ref[...]
    td = _rope_tables(pos, fd_ref[0:1, :], fd_ref[1:2, :], fd_ref[2:3, :])
    tmla = _rope_tables(pos, fm_ref[0:1, :], fm_ref[1:2, :], fm_ref[2:3, :])

    da_scale = DA_HEAD_DIM ** -0.5
    zq = _dot(hb, wqa_ref[...])
    for j in range(DA_HEADS):
        sl = slice(j * LANES, (j + 1) * LANES)
        qa_ref[:, sl] = (_rope_block(zq[:, sl], td, DA_ROT // 2) * da_scale).astype(jnp.bfloat16)
    zk = _dot(hb, wka_ref[...])
    for j in range(DA_HEADS):
        sl = slice(j * LANES, (j + 1) * LANES)
        ka_ref[:, sl] = _rope_block(zk[:, sl], td, DA_ROT // 2).astype(jnp.bfloat16)
    va_ref[...] = _dot(hb, wva_ref[...]).astype(jnp.bfloat16)
    gates_ref[...] = jax.nn.sigmoid(_dot(hb, wg_ref[...]) + bg_ref[...]).astype(jnp.bfloat16)

    mla_scale = (MLA_NOPE + MLA_ROPE) ** -0.5
    cq = _rms(_dot(hb, wcq_ref[...]), gq_ref[...]).astype(jnp.bfloat16)
    qm = _dot(cq, wuq_ref[...])
    for h in range(MLA_HEADS):
        lo = slice(h * MLA_QK_PAD, h * MLA_QK_PAD + LANES)
        hi = slice(h * MLA_QK_PAD + LANES, (h + 1) * MLA_QK_PAD)
        qm_ref[:, lo] = (qm[:, lo] * mla_scale).astype(jnp.bfloat16)
        qm_ref[:, hi] = (_rope_block(qm[:, hi], tmla, MLA_ROPE // 2) * mla_scale).astype(jnp.bfloat16)

    ck = _dot(hb, wckr_ref[...])
    kr = _rope_block(ck[:, MLA_KV_LORA:], tmla, MLA_ROPE // 2).astype(jnp.bfloat16)
    ckv = _rms(ck[:, :MLA_KV_LORA], gkv_ref[...]).astype(jnp.bfloat16)
    kn = _dot(ckv, wkn_ref[...])
    for h in range(MLA_HEADS):
        km_ref[:, h * MLA_QK_PAD:h * MLA_QK_PAD + LANES] = kn[:, h * LANES:(h + 1) * LANES].astype(jnp.bfloat16)
        km_ref[:, h * MLA_QK_PAD + LANES:(h + 1) * MLA_QK_PAD] = kr
    vm_ref[...] = _dot(ckv, wv_ref[...]).astype(jnp.bfloat16)


def _const_spec(shape):
    return pl.BlockSpec(shape, lambda i: (0,) * len(shape))


def _proj(x2, posf, gmix, wqa, wka, wva, wcq, wckr, wg, bg, gq, gkv, wuq, wkn, wv, fd, fm):
    t = x2.shape[0]
    tm = TOKEN_TILE
    bf = jnp.bfloat16
    row = lambda w: pl.BlockSpec((tm, w), lambda i: (i, 0))
    consts = (gmix, wqa, wka, wva, wcq, wckr, wg, bg, gq, gkv, wuq, wkn, wv, fd, fm)
    out_w = (D_MODEL, D_MODEL, D_MODEL, 2 * D_MODEL, MLA_HEADS * MLA_QK_PAD,
             MLA_HEADS * MLA_QK_PAD, MLA_HEADS * MLA_V)
    return pl.pallas_call(
        _proj_kernel,
        grid=(t // tm,),
        in_specs=[row(D_MODEL), row(1)] + [_const_spec(c.shape) for c in consts],
        out_specs=[row(w) for w in out_w],
        out_shape=[jax.ShapeDtypeStruct((t, w), bf) for w in out_w],
        compiler_params=pltpu.CompilerParams(
            dimension_semantics=("parallel",), vmem_limit_bytes=VMEM_LIMIT),
        name="proj",
    )(x2, posf, *consts)


def _softmax_parts(s):
    m = jnp.max(s, axis=-1, keepdims=True)
    p = jnp.exp(s - m)
    return p, jnp.sum(p, axis=-1, keepdims=True)


def _dattn_kernel(q_ref, k_ref, v_ref, lam_ref, gsub_ref, o_ref):
    q = q_ref[...]
    k = k_ref[...]
    lane = lax.broadcasted_iota(jnp.int32, q.shape, 1)
    zero = jnp.zeros_like(q)
    q0 = jnp.where(lane < DA_HEAD_DIM, q, zero)
    q1 = jnp.where(lane >= DA_HEAD_DIM, q, zero)
    p0, l0 = _softmax_parts(lax.dot_general(q0, k, _NT, preferred_element_type=jnp.float32))
    p1, l1 = _softmax_parts(lax.dot_general(q1, k, _NT, preferred_element_type=jnp.float32))
    lamv = lam_ref[...]
    lam = (jnp.exp(jnp.sum(lamv[0:1] * lamv[1:2], axis=-1, keepdims=True))
           - jnp.exp(jnp.sum(lamv[2:3] * lamv[3:4], axis=-1, keepdims=True)) + LAM_INIT)
    a = (p0 * (1.0 / l0) - p1 * (lam / l1)).astype(jnp.bfloat16)
    o = _dot(a, v_ref[...])
    o_ref[...] = (_rms(o, gsub_ref[...]) * (1.0 - LAM_INIT)).astype(o_ref.dtype)


def _dattn(qa, ka, va, lamv, gsub, batch, seq):
    tq = Q_TILE
    nq = seq // tq
    return pl.pallas_call(
        _dattn_kernel,
        grid=(batch, DA_HEADS, nq),
        in_specs=[
            pl.BlockSpec((tq, LANES), lambda b, h, i: (b * nq + i, h)),
            pl.BlockSpec((seq, LANES), lambda b, h, i: (b, h)),
            pl.BlockSpec((seq, LANES), lambda b, h, i: (b, h)),
            pl.BlockSpec(lamv.shape, lambda b, h, i: (0, 0)),
            pl.BlockSpec(gsub.shape, lambda b, h, i: (0, 0)),
        ],
        out_specs=pl.BlockSpec((tq, LANES), lambda b, h, i: (b * nq + i, h)),
        out_shape=jax.ShapeDtypeStruct(qa.shape, jnp.bfloat16),
        compiler_params=pltpu.CompilerParams(
            dimension_semantics=("parallel", "parallel", "parallel"),
            vmem_limit_bytes=VMEM_LIMIT),
        name="dattn",
    )(qa, ka, va, lamv, gsub)


def _mattn_kernel(q_ref, k_ref, v_ref, o_ref):
    s = lax.dot_general(q_ref[...], k_ref[...], _NT, preferred_element_type=jnp.float32)
    p, l = _softmax_parts(s)
    a = (p * (1.0 / l)).astype(jnp.bfloat16)
    o_ref[...] = _dot(a, v_ref[...]).astype(o_ref.dtype)


def _mattn(qm, km, vm, batch, seq):
    tq = Q_TILE
    nq = seq // tq
    return pl.pallas_call(
        _mattn_kernel,
        grid=(batch, MLA_HEADS, nq),
        in_specs=[
            pl.BlockSpec((tq, MLA_QK_PAD), lambda b, h, i: (b * nq + i, h)),
            pl.BlockSpec((seq, MLA_QK_PAD), lambda b, h, i: (b, h)),
            pl.BlockSpec((seq, MLA_V), lambda b, h, i: (b, h)),
        ],
        out_specs=pl.BlockSpec((tq, MLA_V), lambda b, h, i: (b * nq + i, h)),
        out_shape=jax.ShapeDtypeStruct(vm.shape, jnp.bfloat16),
        compiler_params=pltpu.CompilerParams(
            dimension_semantics=("parallel", "parallel", "parallel"),
            vmem_limit_bytes=VMEM_LIMIT),
        name="mattn",
    )(qm, km, vm)


def _cols(parts, width, dtype):
    lane = lax.broadcasted_iota(jnp.int32, (parts[0].shape[0], width), 1)
    out = jnp.zeros((parts[0].shape[0], width), dtype)
    for k, p in enumerate(parts):
        out = jnp.where(lane == k, p.astype(dtype), out)
    return out


def _merge_kernel(x_ref, oa_ref, ob_ref, gates_ref, woa_ref, wob_ref, wout_ref, gffn_ref,
                  wr_ref, br_ref,
                  x1_ref, h2_ref, idx_ref, w_ref, rank_ref, counts_ref, carry_ref):
    i = pl.program_id(0)

    @pl.when(i == 0)
    def _():
        carry_ref[...] = jnp.zeros_like(carry_ref)

    gates = gates_ref[...].astype(jnp.float32)
    merged = (gates[:, :D_MODEL] * _dot(oa_ref[...], woa_ref[...])
              + gates[:, D_MODEL:] * _dot(ob_ref[...], wob_ref[...]))
    x1 = x_ref[...] + _dot(merged.astype(jnp.bfloat16), wout_ref[...])
    x1_ref[...] = x1
    h2 = _rms(x1, gffn_ref[...])
    h2_ref[...] = h2

    logits = jnp.dot(h2, wr_ref[...], precision=lax.Precision.HIGHEST,
                     preferred_element_type=jnp.float32) + br_ref[...]
    tm = logits.shape[0]
    lane = lax.broadcasted_iota(jnp.int32, logits.shape, 1)
    vals, idxs, hots = [], [], []
    l = logits
    for _ in range(TOP_K):
        m = jnp.max(l, axis=-1, keepdims=True)
        idx = jnp.min(jnp.where(l == m, lane, N_EXPERTS), axis=-1, keepdims=True)
        hot = lane == idx
        vals.append(m)
        idxs.append(idx)
        hots.append(hot)
        l = jnp.where(hot, -jnp.inf, l)
    es = [jnp.exp(v - vals[0]) for v in vals]
    den = es[0] + es[1] + es[2] + es[3]
    w_ref[...] = _cols([e / den for e in es], TOP_K, jnp.float32)
    idx_ref[...] = _cols(idxs, TOP_K, jnp.int32)

    chosen = (hots[0] | hots[1] | hots[2] | hots[3]).astype(jnp.float32)
    r_i = lax.broadcasted_iota(jnp.int32, (tm, tm), 0)
    c_i = lax.broadcasted_iota(jnp.int32, (tm, tm), 1)
    lower = (c_i < r_i).astype(jnp.bfloat16)
    prefix = _dot(lower, chosen.astype(jnp.bfloat16)) + carry_ref[...]
    ranks = [jnp.sum(jnp.where(h, prefix, 0.0), axis=-1, keepdims=True) for h in hots]
    rank_ref[...] = _cols(ranks, TOP_K, jnp.int32)
    carry = carry_ref[...] + jnp.sum(chosen, axis=0, keepdims=True)
    carry_ref[...] = carry
    counts_ref[...] = carry.astype(jnp.int32)


def _merge(x2, oa, ob, gates, woa, wob, wout, gffn, wr, br):
    t = x2.shape[0]
    tm = TOKEN_TILE
    row = lambda w: pl.BlockSpec((tm, w), lambda i: (i, 0))
    consts = (woa, wob, wout, gffn, wr, br)
    return pl.pallas_call(
        _merge_kernel,
        grid=(t // tm,),
        in_specs=[row(D_MODEL), row(D_MODEL), row(D_MODEL), row(2 * D_MODEL)]
        + [_const_spec(c.shape) for c in consts],
        out_specs=[row(D_MODEL), row(D_MODEL), row(TOP_K), row(TOP_K), row(TOP_K),
                   _const_spec((1, N_EXPERTS))],
        out_shape=[
            jax.ShapeDtypeStruct((t, D_MODEL), jnp.float32),
            jax.ShapeDtypeStruct((t, D_MODEL), jnp.float32),
            jax.ShapeDtypeStruct((t, TOP_K), jnp.int32),
            jax.ShapeDtypeStruct((t, TOP_K), jnp.float32),
            jax.ShapeDtypeStruct((t, TOP_K), jnp.int32),
            jax.ShapeDtypeStruct((1, N_EXPERTS), jnp.int32),
        ],
        scratch_shapes=[pltpu.VMEM((1, N_EXPERTS), jnp.float32)],
        compiler_params=pltpu.CompilerParams(
            dimension_semantics=("arbitrary",), vmem_limit_bytes=VMEM_LIMIT),
        name="merge",
    )(x2, oa, ob, gates, *consts)


def _row_copy_wait(src_rows_ref, dst_rows_ref, sem, n):
    for _ in range(n):
        pltpu.make_async_copy(src_rows_ref, dst_rows_ref, sem).wait()


def _dispatch_kernel(dest_ref, h2_ref, xs_in_ref, xs_ref, sem):
    del xs_in_ref
    i = pl.program_id(0)
    tm = h2_ref.shape[0]

    def body(r, c):
        base = (i * tm + r) * TOP_K
        for k in range(TOP_K):
            d = dest_ref[base + k]
            pltpu.make_async_copy(h2_ref.at[pl.ds(r, 1)], xs_ref.at[pl.ds(d, 1)], sem).start()
        return c

    lax.fori_loop(0, tm, body, 0)
    _row_copy_wait(h2_ref, xs_ref.at[pl.ds(0, tm)], sem, TOP_K)


def _dispatch(dest, h2, xs0):
    t = h2.shape[0]
    tm = TOKEN_TILE
    return pl.pallas_call(
        _dispatch_kernel,
        grid_spec=pltpu.PrefetchScalarGridSpec(
            num_scalar_prefetch=1,
            grid=(t // tm,),
            in_specs=[pl.BlockSpec((tm, D_MODEL), lambda i, d: (i, 0)),
                      pl.BlockSpec(memory_space=pl.ANY)],
            out_specs=pl.BlockSpec(memory_space=pl.ANY),
            scratch_shapes=[pltpu.SemaphoreType.DMA(())],
        ),
        out_shape=jax.ShapeDtypeStruct(xs0.shape, xs0.dtype),
        input_output_aliases={2: 0},
        compiler_params=pltpu.CompilerParams(
            dimension_semantics=("arbitrary",), vmem_limit_bytes=VMEM_LIMIT),
        name="dispatch",
    )(dest, h2, xs0)


def _experts_kernel(be_ref, nu_ref, xs_ref, wg_ref, bg_ref, wu_ref, bu_ref, wd_ref, bd_ref, ys_ref):
    @pl.when(pl.program_id(0) >= nu_ref[0])
    def _():
        ys_ref[...] = jnp.zeros_like(ys_ref)

    @pl.when(pl.program_id(0) < nu_ref[0])
    def _():
        xb = xs_ref[...].astype(jnp.bfloat16)
        gate = jnp.minimum(_dot(xb, wg_ref[0]) + bg_ref[0], SWIGLU_LIMIT)
        up = jnp.clip(_dot(xb, wu_ref[0]) + bu_ref[0], -SWIGLU_LIMIT, SWIGLU_LIMIT)
        act = (up + 1.0) * (gate * jax.nn.sigmoid(SWIGLU_ALPHA * gate))
        ys_ref[...] = _dot(act.astype(jnp.bfloat16), wd_ref[0]) + bd_ref[0]


def _experts(block_e, n_used, xs, wg, bg, wu, bu, wd, bd, n_blocks):
    bm = ROW_BLOCK
    rows = lambda b, be, nu: (jnp.minimum(b, nu[0] - 1), 0)
    wspec = lambda: pl.BlockSpec((1, D_MODEL, D_FF), lambda b, be, nu: (be[b], 0, 0))
    bspec = lambda: pl.BlockSpec((1, 1, D_FF), lambda b, be, nu: (be[b], 0, 0))
    return pl.pallas_call(
        _experts_kernel,
        grid_spec=pltpu.PrefetchScalarGridSpec(
            num_scalar_prefetch=2,
            grid=(n_blocks,),
            in_specs=[pl.BlockSpec((bm, D_MODEL), rows), wspec(), bspec(), wspec(), bspec(),
                      wspec(), bspec()],
            out_specs=pl.BlockSpec((bm, D_MODEL), lambda b, be, nu: (b, 0)),
        ),
        out_shape=jax.ShapeDtypeStruct((n_blocks * bm, D_MODEL), jnp.float32),
        compiler_params=pltpu.CompilerParams(
            dimension_semantics=("arbitrary",), vmem_limit_bytes=VMEM_LIMIT),
        name="experts",
    )(block_e, n_used, xs, wg, bg, wu, bu, wd, bd)


def _combine_kernel(dest_ref, ys_ref, x1_ref, w_ref, gfin_ref, o_ref, buf_ref, sem):
    i = pl.program_id(0)
    tm = x1_ref.shape[0]

    def body(r, c):
        base = (i * tm + r) * TOP_K
        for k in range(TOP_K):
            d = dest_ref[base + k]
            pltpu.make_async_copy(ys_ref.at[pl.ds(d, 1)], buf_ref.at[k, pl.ds(r, 1)], sem).start()
        return c

    lax.fori_loop(0, tm, body, 0)
    _row_copy_wait(ys_ref.at[pl.ds(0, tm)], buf_ref.at[0], sem, TOP_K)
    w = w_ref[...]
    y = x1_ref[...]
    for k in range(TOP_K):
        y = y + buf_ref[k] * w[:, k:k + 1]
    o_ref[...] = _rms(y, gfin_ref[...])


def _combine(dest, ys, x1, top_w, gfin):
    t = x1.shape[0]
    tm = TOKEN_TILE
    return pl.pallas_call(
        _combine_kernel,
        grid_spec=pltpu.PrefetchScalarGridSpec(
            num_scalar_prefetch=1,
            grid=(t // tm,),
            in_specs=[pl.BlockSpec(memory_space=pl.ANY),
                      pl.BlockSpec((tm, D_MODEL), lambda i, d: (i, 0)),
                      pl.BlockSpec((tm, TOP_K), lambda i, d: (i, 0)),
                      pl.BlockSpec((1, D_MODEL), lambda i, d: (0, 0))],
            out_specs=pl.BlockSpec((tm, D_MODEL), lambda i, d: (i, 0)),
            scratch_shapes=[pltpu.VMEM((TOP_K, tm, D_MODEL), jnp.float32),
                            pltpu.SemaphoreType.DMA(())],
        ),
        out_shape=jax.ShapeDtypeStruct((t, D_MODEL), jnp.float32),
        compiler_params=pltpu.CompilerParams(
            dimension_semantics=("arbitrary",), vmem_limit_bytes=VMEM_LIMIT),
        name="combine",
    )(dest, ys, x1, top_w, gfin)


def _rope_lane_table(rot, group):
    half = rot // 2
    inv = ROPE_THETA ** (-jnp.arange(0, rot, 2, dtype=jnp.float32) / rot)
    d = jnp.arange(LANES) % group
    first = jnp.arange(LANES) < (LANES if group < LANES else rot)
    in_lo = (d < half) & first
    in_hi = (d >= half) & (d < rot) & first
    freq = jnp.where(in_lo | in_hi, inv[d % half], 0.0)
    return jnp.stack([freq, in_lo.astype(jnp.float32), in_hi.astype(jnp.float32)]).astype(jnp.float32)


def kernel(x, positions, g_mix, w_in, lam_q1, lam_k1, lam_q2, lam_k2, g_subln, g_q, g_kv, w_uq, w_ukv, w_o_diff, w_o_mla, b_gates, w_out, g_ffn, w_router, b_router, w_gate, b_gate, w_up, b_up, w_down, b_down, g_final):
    batch, seq, d = x.shape
    t = batch * seq
    bf = jnp.bfloat16
    l = 0
    x2 = x.reshape(t, d)
    posf = positions.astype(jnp.float32).reshape(t, 1)

    w = w_in[l]
    o0, o1, o2, o3, o4, o5 = 1024, 2048, 3072, 3072 + 768, 3072 + 768 + 512, 3072 + 768 + 512 + 64
    wqa, wka, wva = w[:, :o0].astype(bf), w[:, o0:o1].astype(bf), w[:, o1:o2].astype(bf)
    wcq = w[:, o2:o3].astype(bf)
    wckr = jnp.pad(w[:, o3:o5], ((0, 0), (0, LANES - MLA_ROPE))).astype(bf)
    wg = w[:, o5:].astype(bf)
    wuq = jnp.pad(w_uq[l].reshape(MLA_Q_LORA, MLA_HEADS, MLA_NOPE + MLA_ROPE),
                  ((0, 0), (0, 0), (0, MLA_QK_PAD - MLA_NOPE - MLA_ROPE))
                  ).reshape(MLA_Q_LORA, MLA_HEADS * MLA_QK_PAD).astype(bf)
    wukv = w_ukv[l].reshape(MLA_KV_LORA, MLA_HEADS, MLA_NOPE + MLA_V)
    wkn = wukv[:, :, :MLA_NOPE].reshape(MLA_KV_LORA, MLA_HEADS * MLA_NOPE).astype(bf)
    wv = wukv[:, :, MLA_NOPE:].reshape(MLA_KV_LORA, MLA_HEADS * MLA_V).astype(bf)
    fd = _rope_lane_table(DA_ROT, DA_HEAD_DIM)
    fm = _rope_lane_table(MLA_ROPE, LANES)

    qa, ka, va, gates, qm, km, vm = _proj(
        x2, posf, g_mix[l][None], wqa, wka, wva, wcq, wckr, wg, b_gates[l][None],
        g_q[l][None], g_kv[l][None], wuq, wkn, wv, fd, fm)

    lamv = jnp.stack([lam_q1[l], lam_k1[l], lam_q2[l], lam_k2[l]]).astype(jnp.float32)
    oa = _dattn(qa, ka, va, lamv, g_subln[l][None], batch, seq)
    ob = _mattn(qm, km, vm, batch, seq)

    x1, h2, e_idx, top_w, rank, counts = _merge(
        x2, oa, ob, gates, w_o_diff[l].astype(bf), w_o_mla[l].astype(bf), w_out[l].astype(bf),
        g_ffn[l][None], w_router[l], b_router[l][None])

    bm = ROW_BLOCK
    n_blocks = (t * TOP_K) // bm + N_EXPERTS
    counts = counts[0]
    padded = (counts + bm - 1) // bm * bm
    padded_end = jnp.cumsum(padded)
    padded_start = padded_end - padded
    n_used = (padded_end[-1] // bm).astype(jnp.int32)
    blk = jnp.minimum(jnp.arange(n_blocks, dtype=jnp.int32), n_used - 1)
    block_e = jnp.minimum(jnp.searchsorted(padded_end, blk * bm, side='right'),
                          N_EXPERTS - 1).astype(jnp.int32)
    dest = (padded_start[e_idx] + rank).reshape(-1).astype(jnp.int32)

    xs0 = jnp.zeros((n_blocks * bm, d), jnp.float32)
    xs = _dispatch(dest, h2, xs0)
    ys = _experts(block_e, n_used.reshape(1), xs,
                  w_gate[l].astype(bf), b_gate[l][:, None, :],
                  w_up[l].astype(bf), b_up[l][:, None, :],
                  w_down[l].astype(bf), b_down[l][:, None, :], n_blocks)
    out = _combine(dest, ys, x1, top_w, g_final[None])
    return out.reshape(batch, seq, d)
```

```python
import functools
import math

import jax
import jax.numpy as jnp
from jax import lax
from jax.experimental import pallas as pl
from jax.experimental.pallas import tpu as pltpu

D_MODEL = 1024
ROPE_THETA = 500000.0
NORM_EPS = 1e-6
DA_HEADS = 8
DA_HEAD_DIM = 64
DA_ROT = DA_HEAD_DIM // 4
MLA_HEADS = 8
MLA_Q_LORA = 768
MLA_KV_LORA = 512
MLA_NOPE = 128
MLA_ROPE = 64
MLA_V = 128
N_EXPERTS = 32
TOP_K = 4
D_FF = 1024
SWIGLU_ALPHA = 1.702
SWIGLU_LIMIT = 7.0
LAM_INIT = 0.8 - 0.6 * math.exp(-0.3 * 0)

LANES = 128
MLA_QK_PAD = 256
TOKEN_TILE = 256
Q_TILE = 256
ROW_BLOCK = 256
VMEM_LIMIT = 56 * 1024 * 1024

LOG2_E = math.log2(math.e)

IN_W = {"qa": DA_HEADS * 2 * DA_HEAD_DIM, "ka": DA_HEADS * 2 * DA_HEAD_DIM,
        "va": DA_HEADS * 2 * DA_HEAD_DIM, "cq": MLA_Q_LORA,
        "ckr": MLA_KV_LORA + LANES,
        "g": 2 * D_MODEL}
IN_OFF = dict(zip(IN_W, [sum(list(IN_W.values())[:n]) for n in range(len(IN_W))]))
IN_KR_END = IN_OFF["ckr"] + MLA_KV_LORA + MLA_ROPE

_NT = (((1,), (1,)), ((), ()))


def _rms(x, g):
    return x * lax.rsqrt(jnp.mean(x * x, axis=-1, keepdims=True) + NORM_EPS) * g


def _dot(a, b):
    return jnp.dot(a, b, preferred_element_type=jnp.float32)


def _rope_tables(pos, freq, m_lo, m_hi):
    ang = pos * freq
    c = jnp.cos(ang)
    s = jnp.sin(ang)
    return c, -s * m_lo, s * m_hi


def _rope_block(xb, tables, half):
    c, s_lo, s_hi = tables
    return (xb * c + pltpu.roll(xb, LANES - half, 1) * s_lo
            + pltpu.roll(xb, half, 1) * s_hi)


def _proj_kernel(x_ref, pos_ref, gmix_ref, win_ref, bg_ref, gq_ref, gkv_ref, wuq_ref, wkn_ref, wv_ref,
                 fd_ref, fm_ref,
                 qa_ref, ka_ref, va_ref, gates_ref, qm_ref, km_ref, vm_ref):
    hb = _rms(x_ref[...], gmix_ref[...]).astype(jnp.bfloat16)
    pos = pos_ref[...]
    td = _rope_tables(pos, fd_ref[0:1, :], fd_ref[1:2, :], fd_ref[2:3, :])
    tmla = _rope_tables(pos, fm_ref[0:1, :], fm_ref[1:2, :], fm_ref[2:3, :])

    da_scale = DA_HEAD_DIM ** -0.5 * LOG2_E
    w_cols = lambda name: win_ref[:, IN_OFF[name]:IN_OFF[name] + IN_W[name]]
    zq = _dot(hb, w_cols("qa"))
    for j in range(DA_HEADS):
        sl = slice(j * LANES, (j + 1) * LANES)
        qa_ref[:, sl] = (_rope_block(zq[:, sl], td, DA_ROT // 2) * da_scale).astype(jnp.bfloat16)
    zk = _dot(hb, w_cols("ka"))
    for j in range(DA_HEADS):
        sl = slice(j * LANES, (j + 1) * LANES)
        ka_ref[:, sl] = _rope_block(zk[:, sl], td, DA_ROT // 2).astype(jnp.bfloat16)
    va_ref[...] = _dot(hb, w_cols("va")).astype(jnp.bfloat16)
    gates_ref[...] = jax.nn.sigmoid(_dot(hb, w_cols("g")) + bg_ref[...]).astype(jnp.bfloat16)

    mla_scale = (MLA_NOPE + MLA_ROPE) ** -0.5 * LOG2_E
    cq = _rms(_dot(hb, w_cols("cq")), gq_ref[...]).astype(jnp.bfloat16)
    qm = _dot(cq, wuq_ref[...])
    for h in range(MLA_HEADS):
        lo = slice(h * MLA_QK_PAD, h * MLA_QK_PAD + LANES)
        hi = slice(h * MLA_QK_PAD + LANES, (h + 1) * MLA_QK_PAD)
        qm_ref[:, lo] = (qm[:, lo] * mla_scale).astype(jnp.bfloat16)
        qm_ref[:, hi] = (_rope_block(qm[:, hi], tmla, MLA_ROPE // 2) * mla_scale).astype(jnp.bfloat16)

    ck = _dot(hb, w_cols("ckr"))
    kr = _rope_block(ck[:, MLA_KV_LORA:], tmla, MLA_ROPE // 2).astype(jnp.bfloat16)
    ckv = _rms(ck[:, :MLA_KV_LORA], gkv_ref[...]).astype(jnp.bfloat16)
    kn = _dot(ckv, wkn_ref[...])
    for h in range(MLA_HEADS):
        km_ref[:, h * MLA_QK_PAD:h * MLA_QK_PAD + LANES] = kn[:, h * LANES:(h + 1) * LANES].astype(jnp.bfloat16)
        km_ref[:, h * MLA_QK_PAD + LANES:(h + 1) * MLA_QK_PAD] = kr
    vm_ref[...] = _dot(ckv, wv_ref[...]).astype(jnp.bfloat16)


def _const_spec(shape):
    return pl.BlockSpec(shape, lambda i: (0,) * len(shape))


def _proj(x2, posf, gmix, win, bg, gq, gkv, wuq, wkn, wv, fd, fm):
    t = x2.shape[0]
    tm = TOKEN_TILE
    bf = jnp.bfloat16
    row = lambda w: pl.BlockSpec((tm, w), lambda i: (i, 0))
    consts = (gmix, win, bg, gq, gkv, wuq, wkn, wv, fd, fm)
    out_w = (D_MODEL, D_MODEL, D_MODEL, 2 * D_MODEL, MLA_HEADS * MLA_QK_PAD,
             MLA_HEADS * MLA_QK_PAD, MLA_HEADS * MLA_V)
    return pl.pallas_call(
        _proj_kernel,
        grid=(t // tm,),
        in_specs=[row(D_MODEL), row(1)] + [_const_spec(c.shape) for c in consts],
        out_specs=[row(w) for w in out_w],
        out_shape=[jax.ShapeDtypeStruct((t, w), bf) for w in out_w],
        compiler_params=pltpu.CompilerParams(
            dimension_semantics=("parallel",), vmem_limit_bytes=VMEM_LIMIT),
        name="proj",
    )(x2, posf, *consts)


def _softmax_parts(s):
    m = jnp.max(s, axis=-1, keepdims=True)
    p = jnp.exp2(s - m)
    return p, jnp.sum(p, axis=-1, keepdims=True)


def _dattn_kernel(q_ref, k_ref, v_ref, lam_ref, gsub_ref, o_ref):
    q = q_ref[...]
    k = k_ref[...]
    lane = lax.broadcasted_iota(jnp.int32, q.shape, 1)
    zero = jnp.zeros_like(q)
    q0 = jnp.where(lane < DA_HEAD_DIM, q, zero)
    q1 = jnp.where(lane >= DA_HEAD_DIM, q, zero)
    p0, l0 = _softmax_parts(lax.dot_general(q0, k, _NT, preferred_element_type=jnp.float32))
    p1, l1 = _softmax_parts(lax.dot_general(q1, k, _NT, preferred_element_type=jnp.float32))
    lamv = lam_ref[...]
    lam = (jnp.exp(jnp.sum(lamv[0:1] * lamv[1:2], axis=-1, keepdims=True))
           - jnp.exp(jnp.sum(lamv[2:3] * lamv[3:4], axis=-1, keepdims=True)) + LAM_INIT)
    a = (p0 - p1 * (lam * l0 / l1)).astype(jnp.bfloat16)
    o = _dot(a, v_ref[...]) * (1.0 / l0)
    o_ref[...] = (_rms(o, gsub_ref[...]) * (1.0 - LAM_INIT)).astype(o_ref.dtype)


def _dattn(qa, ka, va, lamv, gsub, batch, seq):
    tq = Q_TILE
    nq = seq // tq
    return pl.pallas_call(
        _dattn_kernel,
        grid=(batch, DA_HEADS, nq),
        in_specs=[
            pl.BlockSpec((tq, LANES), lambda b, h, i: (b * nq + i, h)),
            pl.BlockSpec((seq, LANES), lambda b, h, i: (b, h)),
            pl.BlockSpec((seq, LANES), lambda b, h, i: (b, h)),
            pl.BlockSpec(lamv.shape, lambda b, h, i: (0, 0)),
            pl.BlockSpec(gsub.shape, lambda b, h, i: (0, 0)),
        ],
        out_specs=pl.BlockSpec((tq, LANES), lambda b, h, i: (b * nq + i, h)),
        out_shape=jax.ShapeDtypeStruct(qa.shape, jnp.bfloat16),
        compiler_params=pltpu.CompilerParams(
            dimension_semantics=("parallel", "parallel", "parallel"),
            vmem_limit_bytes=VMEM_LIMIT),
        name="dattn",
    )(qa, ka, va, lamv, gsub)


def _mattn_kernel(q_ref, k_ref, v_ref, o_ref):
    s = lax.dot_general(q_ref[...], k_ref[...], _NT, preferred_element_type=jnp.float32)
    p, l = _softmax_parts(s)
    o_ref[...] = (_dot(p.astype(jnp.bfloat16), v_ref[...]) * (1.0 / l)).astype(o_ref.dtype)


def _mattn(qm, km, vm, batch, seq):
    tq = Q_TILE
    nq = seq // tq
    return pl.pallas_call(
        _mattn_kernel,
        grid=(batch, MLA_HEADS, nq),
        in_specs=[
            pl.BlockSpec((tq, MLA_QK_PAD), lambda b, h, i: (b * nq + i, h)),
            pl.BlockSpec((seq, MLA_QK_PAD), lambda b, h, i: (b, h)),
            pl.BlockSpec((seq, MLA_V), lambda b, h, i: (b, h)),
        ],
        out_specs=pl.BlockSpec((tq, MLA_V), lambda b, h, i: (b * nq + i, h)),
        out_shape=jax.ShapeDtypeStruct(vm.shape, jnp.bfloat16),
        compiler_params=pltpu.CompilerParams(
            dimension_semantics=("parallel", "parallel", "parallel"),
            vmem_limit_bytes=VMEM_LIMIT),
        name="mattn",
    )(qm, km, vm)


def _cols(parts, width, dtype):
    lane = lax.broadcasted_iota(jnp.int32, (parts[0].shape[0], width), 1)
    out = jnp.zeros((parts[0].shape[0], width), dtype)
    for k, p in enumerate(parts):
        out = jnp.where(lane == k, p.astype(dtype), out)
    return out


def _merge_kernel(x_ref, oa_ref, ob_ref, gates_ref, woa_ref, wob_ref, wout_ref, gffn_ref,
                  wr_ref, br_ref,
                  x1_ref, h2_ref, idx_ref, w_ref, rank_ref, counts_ref, carry_ref):
    i = pl.program_id(0)

    @pl.when(i == 0)
    def _():
        carry_ref[...] = jnp.zeros_like(carry_ref)

    gates = gates_ref[...].astype(jnp.float32)
    merged = (gates[:, :D_MODEL] * _dot(oa_ref[...], woa_ref[...])
              + gates[:, D_MODEL:] * _dot(ob_ref[...], wob_ref[...]))
    x1 = x_ref[...] + _dot(merged.astype(jnp.bfloat16), wout_ref[...])
    x1_ref[...] = x1
    h2 = _rms(x1, gffn_ref[...])
    h2_ref[...] = h2

    logits = jnp.dot(h2, wr_ref[...], precision=lax.Precision.HIGHEST,
                     preferred_element_type=jnp.float32) + br_ref[...]
    tm = logits.shape[0]
    lane = lax.broadcasted_iota(jnp.int32, logits.shape, 1)
    vals, idxs, hots = [], [], []
    l = logits
    for _ in range(TOP_K):
        m = jnp.max(l, axis=-1, keepdims=True)
        idx = jnp.min(jnp.where(l == m, lane, N_EXPERTS), axis=-1, keepdims=True)
        hot = lane == idx
        vals.append(m)
        idxs.append(idx)
        hots.append(hot)
        l = jnp.where(hot, -jnp.inf, l)
    es = [jnp.exp(v - vals[0]) for v in vals]
    den = es[0] + es[1] + es[2] + es[3]
    w_ref[...] = _cols([e / den for e in es], TOP_K, jnp.float32)
    idx_ref[...] = _cols(idxs, TOP_K, jnp.int32)

    chosen = (hots[0] | hots[1] | hots[2] | hots[3]).astype(jnp.float32)
    r_i = lax.broadcasted_iota(jnp.int32, (tm, tm), 0)
    c_i = lax.broadcasted_iota(jnp.int32, (tm, tm), 1)
    lower = (c_i < r_i).astype(jnp.bfloat16)
    prefix = _dot(lower, chosen.astype(jnp.bfloat16)) + carry_ref[...]
    ranks = [jnp.sum(jnp.where(h, prefix, 0.0), axis=-1, keepdims=True) for h in hots]
    rank_ref[...] = _cols(ranks, TOP_K, jnp.int32)
    carry = carry_ref[...] + jnp.sum(chosen, axis=0, keepdims=True)
    carry_ref[...] = carry
    counts_ref[...] = carry.astype(jnp.int32)


def _merge(x2, oa, ob, gates, woa, wob, wout, gffn, wr, br):
    t = x2.shape[0]
    tm = TOKEN_TILE
    row = lambda w: pl.BlockSpec((tm, w), lambda i: (i, 0))
    consts = (woa, wob, wout, gffn, wr, br)
    return pl.pallas_call(
        _merge_kernel,
        grid=(t // tm,),
        in_specs=[row(D_MODEL), row(D_MODEL), row(D_MODEL), row(2 * D_MODEL)]
        + [_const_spec(c.shape) for c in consts],
        out_specs=[row(D_MODEL), row(D_MODEL), row(TOP_K), row(TOP_K), row(TOP_K),
                   _const_spec((1, N_EXPERTS))],
        out_shape=[
            jax.ShapeDtypeStruct((t, D_MODEL), jnp.float32),
            jax.ShapeDtypeStruct((t, D_MODEL), jnp.float32),
            jax.ShapeDtypeStruct((t, TOP_K), jnp.int32),
            jax.ShapeDtypeStruct((t, TOP_K), jnp.float32),
            jax.ShapeDtypeStruct((t, TOP_K), jnp.int32),
            jax.ShapeDtypeStruct((1, N_EXPERTS), jnp.int32),
        ],
        scratch_shapes=[pltpu.VMEM((1, N_EXPERTS), jnp.float32)],
        compiler_params=pltpu.CompilerParams(
            dimension_semantics=("arbitrary",), vmem_limit_bytes=VMEM_LIMIT),
        name="merge",
    )(x2, oa, ob, gates, *consts)


def _row_copy_wait(src_rows_ref, dst_rows_ref, sem, n):
    for _ in range(n):
        pltpu.make_async_copy(src_rows_ref, dst_rows_ref, sem).wait()


def _dispatch_kernel(dest_ref, pad_lo_ref, pad_hi_ref, nu_ref, h2_ref, xs_ref, zero_ref, sem, zsem):
    i = pl.program_id(0)
    tm = h2_ref.shape[0]
    bm = zero_ref.shape[0]
    n_blocks = xs_ref.shape[0] // bm

    @pl.when(i == 0)
    def _():
        zero_ref[...] = jnp.zeros_like(zero_ref)

        def pad_rows(fn):
            def per_expert(e, c):
                return lax.fori_loop(pad_lo_ref[e], pad_hi_ref[e], fn, c)
            lax.fori_loop(0, N_EXPERTS, per_expert, 0)

        def row_copy(j):
            return pltpu.make_async_copy(zero_ref.at[pl.ds(0, 1)], xs_ref.at[pl.ds(j, 1)], zsem)

        def blk_copy(b):
            return pltpu.make_async_copy(zero_ref, xs_ref.at[pl.ds(pl.multiple_of(b * bm, bm), bm)], zsem)

        def start_row(j, c):
            row_copy(j).start()
            return c

        def wait_row(j, c):
            row_copy(j).wait()
            return c

        def start_blk(b, c):
            blk_copy(b).start()
            return c

        def wait_blk(b, c):
            blk_copy(b).wait()
            return c

        pad_rows(start_row)
        lax.fori_loop(nu_ref[0], n_blocks, start_blk, 0)
        pad_rows(wait_row)
        lax.fori_loop(nu_ref[0], n_blocks, wait_blk, 0)

    def body(r, c):
        base = (i * tm + r) * TOP_K
        for k in range(TOP_K):
            d = dest_ref[base + k]
            pltpu.make_async_copy(h2_ref.at[pl.ds(r, 1)], xs_ref.at[pl.ds(d, 1)], sem).start()
        return c

    lax.fori_loop(0, tm, body, 0)
    _row_copy_wait(h2_ref, xs_ref.at[pl.ds(0, tm)], sem, TOP_K)


def _dispatch(dest, pad_lo, pad_hi, n_used, h2, n_blocks):
    t = h2.shape[0]
    tm = TOKEN_TILE
    bm = ROW_BLOCK
    return pl.pallas_call(
        _dispatch_kernel,
        grid_spec=pltpu.PrefetchScalarGridSpec(
            num_scalar_prefetch=4,
            grid=(t // tm,),
            in_specs=[pl.BlockSpec((tm, D_MODEL), lambda i, *_: (i, 0))],
            out_specs=pl.BlockSpec(memory_space=pl.ANY),
            scratch_shapes=[pltpu.VMEM((bm, D_MODEL), jnp.float32),
                            pltpu.SemaphoreType.DMA(()), pltpu.SemaphoreType.DMA(())],
        ),
        out_shape=jax.ShapeDtypeStruct((n_blocks * bm, D_MODEL), jnp.float32),
        compiler_params=pltpu.CompilerParams(
            dimension_semantics=("arbitrary",), vmem_limit_bytes=VMEM_LIMIT),
        name="dispatch",
    )(dest, pad_lo, pad_hi, n_used, h2)


def _experts_kernel(be_ref, nu_ref, xs_ref, wg_ref, bg_ref, wu_ref, bu_ref, wd_ref, bd_ref, ys_ref,
                    wbf_ref):
    b = pl.program_id(0)

    @pl.when(jnp.logical_or(b == 0, be_ref[b] != be_ref[jnp.maximum(b - 1, 0)]))
    def _():
        wbf_ref[0] = wg_ref[0].astype(jnp.bfloat16)
        wbf_ref[1] = wu_ref[0].astype(jnp.bfloat16)
        wbf_ref[2] = wd_ref[0].astype(jnp.bfloat16)

    @pl.when(b >= nu_ref[0])
    def _():
        ys_ref[...] = jnp.zeros_like(ys_ref)

    @pl.when(b < nu_ref[0])
    def _():
        xb = xs_ref[...].astype(jnp.bfloat16)
        gate = jnp.minimum(_dot(xb, wbf_ref[0]) + bg_ref[0], SWIGLU_LIMIT)
        up = jnp.clip(_dot(xb, wbf_ref[1]) + bu_ref[0], -SWIGLU_LIMIT, SWIGLU_LIMIT)
        act = (up + 1.0) * (gate * jax.nn.sigmoid(SWIGLU_ALPHA * gate))
        ys_ref[...] = _dot(act.astype(jnp.bfloat16), wbf_ref[2]) + bd_ref[0]


def _experts(block_e, n_used, xs, wg, bg, wu, bu, wd, bd, n_blocks):
    bm = ROW_BLOCK
    rows = lambda b, be, nu: (jnp.minimum(b, nu[0] - 1), 0)
    wspec = lambda: pl.BlockSpec((1, D_MODEL, D_FF), lambda b, be, nu: (be[b], 0, 0))
    bspec = lambda: pl.BlockSpec((1, 1, D_FF), lambda b, be, nu: (be[b], 0, 0))
    return pl.pallas_call(
        _experts_kernel,
        grid_spec=pltpu.PrefetchScalarGridSpec(
            num_scalar_prefetch=2,
            grid=(n_blocks,),
            in_specs=[pl.BlockSpec((bm, D_MODEL), rows), wspec(), bspec(), wspec(), bspec(),
                      wspec(), bspec()],
            out_specs=pl.BlockSpec((bm, D_MODEL), lambda b, be, nu: (b, 0)),
            scratch_shapes=[pltpu.VMEM((3, D_MODEL, D_FF), jnp.bfloat16)],
        ),
        out_shape=jax.ShapeDtypeStruct((n_blocks * bm, D_MODEL), jnp.float32),
        compiler_params=pltpu.CompilerParams(
            dimension_semantics=("arbitrary",), vmem_limit_bytes=VMEM_LIMIT),
        name="experts",
    )(block_e, n_used, xs, wg, bg, wu, bu, wd, bd)


def _combine_kernel(dest_ref, ys_ref, x1_ref, w_ref, gfin_ref, o_ref, buf_ref, sem):
    i = pl.program_id(0)
    tm = x1_ref.shape[0]

    def body(r, c):
        base = (i * tm + r) * TOP_K
        for k in range(TOP_K):
            d = dest_ref[base + k]
            pltpu.make_async_copy(ys_ref.at[pl.ds(d, 1)], buf_ref.at[k, pl.ds(r, 1)], sem).start()
        return c

    lax.fori_loop(0, tm, body, 0)
    _row_copy_wait(ys_ref.at[pl.ds(0, tm)], buf_ref.at[0], sem, TOP_K)
    w = w_ref[...]
    y = x1_ref[...]
    for k in range(TOP_K):
        y = y + buf_ref[k] * w[:, k:k + 1]
    o_ref[...] = _rms(y, gfin_ref[...])


def _combine(dest, ys, x1, top_w, gfin):
    t = x1.shape[0]
    tm = TOKEN_TILE
    return pl.pallas_call(
        _combine_kernel,
        grid_spec=pltpu.PrefetchScalarGridSpec(
            num_scalar_prefetch=1,
            grid=(t // tm,),
            in_specs=[pl.BlockSpec(memory_space=pl.ANY),
                      pl.BlockSpec((tm, D_MODEL), lambda i, d: (i, 0)),
                      pl.BlockSpec((tm, TOP_K), lambda i, d: (i, 0)),
                      pl.BlockSpec((1, D_MODEL), lambda i, d: (0, 0))],
            out_specs=pl.BlockSpec((tm, D_MODEL), lambda i, d: (i, 0)),
            scratch_shapes=[pltpu.VMEM((TOP_K, tm, D_MODEL), jnp.float32),
                            pltpu.SemaphoreType.DMA(())],
        ),
        out_shape=jax.ShapeDtypeStruct((t, D_MODEL), jnp.float32),
        compiler_params=pltpu.CompilerParams(
            dimension_semantics=("arbitrary",), vmem_limit_bytes=VMEM_LIMIT),
        name="combine",
    )(dest, ys, x1, top_w, gfin)


def _rope_lane_table(rot, group):
    half = rot // 2
    inv = ROPE_THETA ** (-jnp.arange(0, rot, 2, dtype=jnp.float32) / rot)
    d = jnp.arange(LANES) % group
    first = jnp.arange(LANES) < (LANES if group < LANES else rot)
    in_lo = (d < half) & first
    in_hi = (d >= half) & (d < rot) & first
    freq = jnp.where(in_lo | in_hi, inv[d % half], 0.0)
    return jnp.stack([freq, in_lo.astype(jnp.float32), in_hi.astype(jnp.float32)]).astype(jnp.float32)


def kernel(x, positions, g_mix, w_in, lam_q1, lam_k1, lam_q2, lam_k2, g_subln, g_q, g_kv, w_uq, w_ukv, w_o_diff, w_o_mla, b_gates, w_out, g_ffn, w_router, b_router, w_gate, b_gate, w_up, b_up, w_down, b_down, g_final):
    batch, seq, d = x.shape
    t = batch * seq
    bf = jnp.bfloat16
    l = 0
    x2 = x.reshape(t, d)
    posf = positions.astype(jnp.float32).reshape(t, 1)

    w = w_in[l]
    win = jnp.concatenate([w[:, :IN_KR_END], jnp.zeros((d, LANES - MLA_ROPE), w.dtype),
                           w[:, IN_KR_END:]], axis=1).astype(bf)
    wuq = jnp.pad(w_uq[l].reshape(MLA_Q_LORA, MLA_HEADS, MLA_NOPE + MLA_ROPE),
                  ((0, 0), (0, 0), (0, MLA_QK_PAD - MLA_NOPE - MLA_ROPE))
                  ).reshape(MLA_Q_LORA, MLA_HEADS * MLA_QK_PAD).astype(bf)
    wukv = w_ukv[l].reshape(MLA_KV_LORA, MLA_HEADS, MLA_NOPE + MLA_V)
    wkn = wukv[:, :, :MLA_NOPE].reshape(MLA_KV_LORA, MLA_HEADS * MLA_NOPE).astype(bf)
    wv = wukv[:, :, MLA_NOPE:].reshape(MLA_KV_LORA, MLA_HEADS * MLA_V).astype(bf)
    fd = _rope_lane_table(DA_ROT, DA_HEAD_DIM)
    fm = _rope_lane_table(MLA_ROPE, LANES)

    qa, ka, va, gates, qm, km, vm = _proj(
        x2, posf, g_mix[l][None], win, b_gates[l][None],
        g_q[l][None], g_kv[l][None], wuq, wkn, wv, fd, fm)

    lamv = jnp.stack([lam_q1[l], lam_k1[l], lam_q2[l], lam_k2[l]]).astype(jnp.float32)
    oa = _dattn(qa, ka, va, lamv, g_subln[l][None], batch, seq)
    ob = _mattn(qm, km, vm, batch, seq)

    x1, h2, e_idx, top_w, rank, counts = _merge(
        x2, oa, ob, gates, w_o_diff[l].astype(bf), w_o_mla[l].astype(bf), w_out[l].astype(bf),
        g_ffn[l][None], w_router[l], b_router[l][None])

    bm = ROW_BLOCK
    n_blocks = (t * TOP_K) // bm + N_EXPERTS
    counts = counts[0]
    padded = (counts + bm - 1) // bm * bm
    padded_end = jnp.cumsum(padded)
    padded_start = padded_end - padded
    n_used = (padded_end[-1] // bm).astype(jnp.int32)
    blk = jnp.minimum(jnp.arange(n_blocks, dtype=jnp.int32), n_used - 1)
    block_e = jnp.minimum(jnp.sum(padded_end[None, :] <= (blk * bm)[:, None], axis=1),
                          N_EXPERTS - 1).astype(jnp.int32)
    hot = e_idx[:, :, None] == jnp.arange(N_EXPERTS, dtype=jnp.int32)
    dest = (jnp.sum(jnp.where(hot, padded_start, 0), axis=-1) + rank).reshape(-1).astype(jnp.int32)

    n_used = n_used.reshape(1)
    xs = _dispatch(dest, (padded_start + counts).astype(jnp.int32), padded_end.astype(jnp.int32),
                   n_used, h2, n_blocks)
    ys = _experts(block_e, n_used, xs, w_gate[l], b_gate[l][:, None, :], w_up[l],
                  b_up[l][:, None, :], w_down[l], b_down[l][:, None, :], n_blocks)
    out = _combine(dest, ys, x1, top_w, g_final[None])
    return out.reshape(batch, seq, d)
```

```python
import math

import jax
import jax.numpy as jnp
from jax import lax
from jax.experimental import pallas as pl
from jax.experimental.pallas import tpu as pltpu

D_MODEL = 1024
ROPE_THETA = 500000.0
NORM_EPS = 1e-6
DA_HEADS = 8
DA_HEAD_DIM = 64
DA_ROT = DA_HEAD_DIM // 4
MLA_HEADS = 8
MLA_Q_LORA = 768
MLA_KV_LORA = 512
MLA_NOPE = 128
MLA_ROPE = 64
MLA_V = 128
N_EXPERTS = 32
TOP_K = 4
D_FF = 1024
SWIGLU_ALPHA = 1.702
SWIGLU_LIMIT = 7.0
LAM_INIT = 0.8 - 0.6 * math.exp(-0.3 * 0)

LANES = 128
MLA_QK_PAD = 256
TOKEN_TILE = 256
Q_TILE = 256
KEY_CHUNK = 256
ROW_BLOCK = 256
VMEM_LIMIT = 56 * 1024 * 1024

LOG2_E = math.log2(math.e)

IN_W = {"qa": DA_HEADS * 2 * DA_HEAD_DIM, "ka": DA_HEADS * 2 * DA_HEAD_DIM,
        "va": DA_HEADS * 2 * DA_HEAD_DIM, "cq": MLA_Q_LORA,
        "ckr": MLA_KV_LORA + LANES,
        "g": 2 * D_MODEL}
IN_OFF = dict(zip(IN_W, [sum(list(IN_W.values())[:n]) for n in range(len(IN_W))]))
IN_KR_END = IN_OFF["ckr"] + MLA_KV_LORA + MLA_ROPE

_NT = (((1,), (1,)), ((), ()))


def _rms(x, g):
    return x * lax.rsqrt(jnp.mean(x * x, axis=-1, keepdims=True) + NORM_EPS) * g


def _dot(a, b):
    return jnp.dot(a, b, preferred_element_type=jnp.float32)


def _rope_tables(pos, freq, m_lo, m_hi):
    ang = pos * freq
    c = jnp.cos(ang)
    s = jnp.sin(ang)
    return c, -s * m_lo, s * m_hi


def _rope_block(xb, tables, half):
    c, s_lo, s_hi = tables
    return (xb * c + pltpu.roll(xb, LANES - half, 1) * s_lo
            + pltpu.roll(xb, half, 1) * s_hi)


def _proj_kernel(x_ref, pos_ref, gmix_ref, win_ref, bg_ref, gq_ref, gkv_ref, wuq_ref, wkn_ref, wv_ref,
                 fd_ref, fm_ref,
                 qa_ref, ka_ref, va_ref, gates_ref, qm_ref, km_ref, vm_ref):
    hb = _rms(x_ref[...], gmix_ref[...]).astype(jnp.bfloat16)
    pos = pos_ref[...]
    td = _rope_tables(pos, fd_ref[0:1, :], fd_ref[1:2, :], fd_ref[2:3, :])
    tmla = _rope_tables(pos, fm_ref[0:1, :], fm_ref[1:2, :], fm_ref[2:3, :])

    da_scale = DA_HEAD_DIM ** -0.5 * LOG2_E
    w_cols = lambda name: win_ref[:, IN_OFF[name]:IN_OFF[name] + IN_W[name]]
    zq = _dot(hb, w_cols("qa"))
    for j in range(DA_HEADS):
        sl = slice(j * LANES, (j + 1) * LANES)
        qa_ref[:, sl] = (_rope_block(zq[:, sl], td, DA_ROT // 2) * da_scale).astype(jnp.bfloat16)
    zk = _dot(hb, w_cols("ka"))
    for j in range(DA_HEADS):
        sl = slice(j * LANES, (j + 1) * LANES)
        ka_ref[:, sl] = _rope_block(zk[:, sl], td, DA_ROT // 2).astype(jnp.bfloat16)
    va_ref[...] = _dot(hb, w_cols("va")).astype(jnp.bfloat16)
    gates_ref[...] = jax.nn.sigmoid(_dot(hb, w_cols("g")) + bg_ref[...]).astype(jnp.bfloat16)

    mla_scale = (MLA_NOPE + MLA_ROPE) ** -0.5 * LOG2_E
    cq = _rms(_dot(hb, w_cols("cq")), gq_ref[...]).astype(jnp.bfloat16)
    qm = _dot(cq, wuq_ref[...])
    for h in range(MLA_HEADS):
        lo = slice(h * MLA_QK_PAD, h * MLA_QK_PAD + LANES)
        hi = slice(h * MLA_QK_PAD + LANES, (h + 1) * MLA_QK_PAD)
        qm_ref[:, lo] = (qm[:, lo] * mla_scale).astype(jnp.bfloat16)
        qm_ref[:, hi] = (_rope_block(qm[:, hi], tmla, MLA_ROPE // 2) * mla_scale).astype(jnp.bfloat16)

    ck = _dot(hb, w_cols("ckr"))
    kr = _rope_block(ck[:, MLA_KV_LORA:], tmla, MLA_ROPE // 2).astype(jnp.bfloat16)
    ckv = _rms(ck[:, :MLA_KV_LORA], gkv_ref[...]).astype(jnp.bfloat16)
    kn = _dot(ckv, wkn_ref[...])
    for h in range(MLA_HEADS):
        km_ref[:, h * MLA_QK_PAD:h * MLA_QK_PAD + LANES] = kn[:, h * LANES:(h + 1) * LANES].astype(jnp.bfloat16)
        km_ref[:, h * MLA_QK_PAD + LANES:(h + 1) * MLA_QK_PAD] = kr
    vm_ref[...] = _dot(ckv, wv_ref[...]).astype(jnp.bfloat16)


def _const_spec(shape):
    return pl.BlockSpec(shape, lambda i: (0,) * len(shape))


def _proj(x2, posf, gmix, win, bg, gq, gkv, wuq, wkn, wv, fd, fm):
    t = x2.shape[0]
    tm = TOKEN_TILE
    bf = jnp.bfloat16
    row = lambda w: pl.BlockSpec((tm, w), lambda i: (i, 0))
    consts = (gmix, win, bg, gq, gkv, wuq, wkn, wv, fd, fm)
    out_w = (D_MODEL, D_MODEL, D_MODEL, 2 * D_MODEL, MLA_HEADS * MLA_QK_PAD,
             MLA_HEADS * MLA_QK_PAD, MLA_HEADS * MLA_V)
    return pl.pallas_call(
        _proj_kernel,
        grid=(t // tm,),
        in_specs=[row(D_MODEL), row(1)] + [_const_spec(c.shape) for c in consts],
        out_specs=[row(w) for w in out_w],
        out_shape=[jax.ShapeDtypeStruct((t, w), bf) for w in out_w],
        compiler_params=pltpu.CompilerParams(
            dimension_semantics=("parallel",), vmem_limit_bytes=VMEM_LIMIT),
        name="proj",
    )(x2, posf, *consts)


def _pipelined_tiles(n_tiles, scores, finish):
    scores(0, 0)

    def pair(j, c):
        t = 2 * j
        scores(t + 1, 1)
        finish(t, 0)
        scores(t + 2, 0)
        finish(t + 1, 1)
        return c

    lax.fori_loop(0, n_tiles // 2 - 1, pair, 0)
    scores(n_tiles - 1, 1)
    finish(n_tiles - 2, 0)
    finish(n_tiles - 1, 1)


def _q_rows(t):
    return pl.ds(pl.multiple_of(t * Q_TILE, Q_TILE), Q_TILE)


def _fill_values_ext(vx_ref, v_ref):
    width = v_ref.shape[1]
    vx_ref[:, :width] = v_ref[...]
    lane = lax.broadcasted_iota(jnp.int32, (v_ref.shape[0], vx_ref.shape[1] - width), 1)
    vx_ref[:, width:] = jnp.where(lane == 0, 1.0, 0.0).astype(vx_ref.dtype)


def _softmax_values(s_ref, vx_ref):
    m = jnp.max(s_ref[...], axis=-1, keepdims=True)
    acc = None
    for c in range(s_ref.shape[1] // KEY_CHUNK):
        cols = slice(c * KEY_CHUNK, (c + 1) * KEY_CHUNK)
        p = jnp.exp2(s_ref[:, cols] - m).astype(jnp.bfloat16)
        part = _dot(p, vx_ref[cols, :])
        acc = part if acc is None else acc + part
    return acc


def _dattn_kernel(q_ref, k_ref, v_ref, lam_ref, gsub_ref, o_ref, s0_ref, s1_ref, vx_ref):
    s_refs = (s0_ref, s1_ref)
    _fill_values_ext(vx_ref, v_ref)
    lamv = lam_ref[...]
    lam = (jnp.exp(jnp.sum(lamv[0:1] * lamv[1:2], axis=-1, keepdims=True))
           - jnp.exp(jnp.sum(lamv[2:3] * lamv[3:4], axis=-1, keepdims=True)) + LAM_INIT)
    width = v_ref.shape[1]

    def scores(t, slot):
        q = q_ref[_q_rows(t), :]
        lane = lax.broadcasted_iota(jnp.int32, q.shape, 1)
        zero = jnp.zeros_like(q)
        k = k_ref[...]
        s_refs[slot][0] = lax.dot_general(jnp.where(lane < DA_HEAD_DIM, q, zero), k, _NT,
                                          preferred_element_type=jnp.float32)
        s_refs[slot][1] = lax.dot_general(jnp.where(lane >= DA_HEAD_DIM, q, zero), k, _NT,
                                          preferred_element_type=jnp.float32)

    def finish(t, slot):
        ox0 = _softmax_values(s_refs[slot].at[0], vx_ref)
        ox1 = _softmax_values(s_refs[slot].at[1], vx_ref)
        o = (ox0[:, :width] * (1.0 / ox0[:, width:width + 1])
             - ox1[:, :width] * (lam / ox1[:, width:width + 1]))
        o_ref[_q_rows(t), :] = (_rms(o, gsub_ref[...]) * (1.0 - LAM_INIT)).astype(o_ref.dtype)

    _pipelined_tiles(q_ref.shape[0] // Q_TILE, scores, finish)


def _dattn(qa, ka, va, lamv, gsub, batch, seq):
    head = lambda b, h: (b, h)
    return pl.pallas_call(
        _dattn_kernel,
        grid=(batch, DA_HEADS),
        in_specs=[
            pl.BlockSpec((seq, LANES), head),
            pl.BlockSpec((seq, LANES), head),
            pl.BlockSpec((seq, LANES), head),
            pl.BlockSpec(lamv.shape, lambda b, h: (0, 0)),
            pl.BlockSpec(gsub.shape, lambda b, h: (0, 0)),
        ],
        out_specs=pl.BlockSpec((seq, LANES), head),
        out_shape=jax.ShapeDtypeStruct(qa.shape, jnp.bfloat16),
        scratch_shapes=[pltpu.VMEM((2, Q_TILE, seq), jnp.float32)] * 2
        + [pltpu.VMEM((seq, 2 * LANES), jnp.bfloat16)],
        compiler_params=pltpu.CompilerParams(
            dimension_semantics=("parallel", "parallel"), vmem_limit_bytes=VMEM_LIMIT),
        name="dattn",
    )(qa, ka, va, lamv, gsub)


def _mattn_kernel(q_ref, k_ref, v_ref, o_ref, s0_ref, s1_ref, vx_ref):
    s_refs = (s0_ref, s1_ref)
    _fill_values_ext(vx_ref, v_ref)
    width = v_ref.shape[1]

    def scores(t, slot):
        s_refs[slot][...] = lax.dot_general(q_ref[_q_rows(t), :], k_ref[...], _NT,
                                            preferred_element_type=jnp.float32)

    def finish(t, slot):
        ox = _softmax_values(s_refs[slot], vx_ref)
        o_ref[_q_rows(t), :] = (ox[:, :width] * (1.0 / ox[:, width:width + 1])).astype(o_ref.dtype)

    _pipelined_tiles(q_ref.shape[0] // Q_TILE, scores, finish)


def _mattn(qm, km, vm, batch, seq):
    head = lambda b, h: (b, h)
    return pl.pallas_call(
        _mattn_kernel,
        grid=(batch, MLA_HEADS),
        in_specs=[
            pl.BlockSpec((seq, MLA_QK_PAD), head),
            pl.BlockSpec((seq, MLA_QK_PAD), head),
            pl.BlockSpec((seq, MLA_V), head),
        ],
        out_specs=pl.BlockSpec((seq, MLA_V), head),
        out_shape=jax.ShapeDtypeStruct(vm.shape, jnp.bfloat16),
        scratch_shapes=[pltpu.VMEM((Q_TILE, seq), jnp.float32)] * 2
        + [pltpu.VMEM((seq, 2 * MLA_V), jnp.bfloat16)],
        compiler_params=pltpu.CompilerParams(
            dimension_semantics=("parallel", "parallel"), vmem_limit_bytes=VMEM_LIMIT),
        name="mattn",
    )(qm, km, vm)


def _cols(parts, width, dtype):
    lane = lax.broadcasted_iota(jnp.int32, (parts[0].shape[0], width), 1)
    out = jnp.zeros((parts[0].shape[0], width), dtype)
    for k, p in enumerate(parts):
        out = jnp.where(lane == k, p.astype(dtype), out)
    return out


def _merge_kernel(x_ref, oa_ref, ob_ref, gates_ref, woa_ref, wob_ref, wout_ref, gffn_ref,
                  wr_ref, br_ref,
                  x1_ref, h2_ref, idx_ref, w_ref, rank_ref, counts_ref, carry_ref):
    i = pl.program_id(0)

    @pl.when(i == 0)
    def _():
        carry_ref[...] = jnp.zeros_like(carry_ref)

    gates = gates_ref[...].astype(jnp.float32)
    merged = (gates[:, :D_MODEL] * _dot(oa_ref[...], woa_ref[...])
              + gates[:, D_MODEL:] * _dot(ob_ref[...], wob_ref[...]))
    x1 = x_ref[...] + _dot(merged.astype(jnp.bfloat16), wout_ref[...])
    x1_ref[...] = x1
    h2 = _rms(x1, gffn_ref[...])
    h2_ref[...] = h2

    logits = jnp.dot(h2, wr_ref[...], precision=lax.Precision.HIGHEST,
                     preferred_element_type=jnp.float32) + br_ref[...]
    tm = logits.shape[0]
    lane = lax.broadcasted_iota(jnp.int32, logits.shape, 1)
    vals, idxs, hots = [], [], []
    l = logits
    for _ in range(TOP_K):
        m = jnp.max(l, axis=-1, keepdims=True)
        idx = jnp.min(jnp.where(l == m, lane, N_EXPERTS), axis=-1, keepdims=True)
        hot = lane == idx
        vals.append(m)
        idxs.append(idx)
        hots.append(hot)
        l = jnp.where(hot, -jnp.inf, l)
    es = [jnp.exp(v - vals[0]) for v in vals]
    den = es[0] + es[1] + es[2] + es[3]
    w_ref[...] = _cols([e / den for e in es], TOP_K, jnp.float32)
    idx_ref[...] = _cols(idxs, TOP_K, jnp.int32)

    chosen = (hots[0] | hots[1] | hots[2] | hots[3]).astype(jnp.float32)
    r_i = lax.broadcasted_iota(jnp.int32, (tm, tm), 0)
    c_i = lax.broadcasted_iota(jnp.int32, (tm, tm), 1)
    lower = (c_i < r_i).astype(jnp.bfloat16)
    prefix = _dot(lower, chosen.astype(jnp.bfloat16)) + carry_ref[...]
    ranks = [jnp.sum(jnp.where(h, prefix, 0.0), axis=-1, keepdims=True) for h in hots]
    rank_ref[...] = _cols(ranks, TOP_K, jnp.int32)
    carry = carry_ref[...] + jnp.sum(chosen, axis=0, keepdims=True)
    carry_ref[...] = carry
    counts_ref[...] = carry.astype(jnp.int32)


def _merge(x2, oa, ob, gates, woa, wob, wout, gffn, wr, br):
    t = x2.shape[0]
    tm = TOKEN_TILE
    row = lambda w: pl.BlockSpec((tm, w), lambda i: (i, 0))
    consts = (woa, wob, wout, gffn, wr, br)
    return pl.pallas_call(
        _merge_kernel,
        grid=(t // tm,),
        in_specs=[row(D_MODEL), row(D_MODEL), row(D_MODEL), row(2 * D_MODEL)]
        + [_const_spec(c.shape) for c in consts],
        out_specs=[row(D_MODEL), row(D_MODEL), row(TOP_K), row(TOP_K), row(TOP_K),
                   _const_spec((1, N_EXPERTS))],
        out_shape=[
            jax.ShapeDtypeStruct((t, D_MODEL), jnp.float32),
            jax.ShapeDtypeStruct((t, D_MODEL), jnp.float32),
            jax.ShapeDtypeStruct((t, TOP_K), jnp.int32),
            jax.ShapeDtypeStruct((t, TOP_K), jnp.float32),
            jax.ShapeDtypeStruct((t, TOP_K), jnp.int32),
            jax.ShapeDtypeStruct((1, N_EXPERTS), jnp.int32),
        ],
        scratch_shapes=[pltpu.VMEM((1, N_EXPERTS), jnp.float32)],
        compiler_params=pltpu.CompilerParams(
            dimension_semantics=("arbitrary",), vmem_limit_bytes=VMEM_LIMIT),
        name="merge",
    )(x2, oa, ob, gates, *consts)


def _row_copy_wait(src_rows_ref, dst_rows_ref, sem, n):
    for _ in range(n):
        pltpu.make_async_copy(src_rows_ref, dst_rows_ref, sem).wait()


def _dispatch_kernel(dest_ref, pad_lo_ref, pad_hi_ref, nu_ref, h2_ref, xs_ref, zero_ref, sem, zsem):
    i = pl.program_id(0)
    tm = h2_ref.shape[0]
    bm = zero_ref.shape[0]
    n_blocks = xs_ref.shape[0] // bm

    @pl.when(i == 0)
    def _():
        zero_ref[...] = jnp.zeros_like(zero_ref)

        def pad_rows(fn):
            def per_expert(e, c):
                return lax.fori_loop(pad_lo_ref[e], pad_hi_ref[e], fn, c)
            lax.fori_loop(0, N_EXPERTS, per_expert, 0)

        def row_copy(j):
            return pltpu.make_async_copy(zero_ref.at[pl.ds(0, 1)], xs_ref.at[pl.ds(j, 1)], zsem)

        def blk_copy(b):
            return pltpu.make_async_copy(zero_ref, xs_ref.at[pl.ds(pl.multiple_of(b * bm, bm), bm)], zsem)

        def start_row(j, c):
            row_copy(j).start()
            return c

        def wait_row(j, c):
            row_copy(j).wait()
            return c

        def start_blk(b, c):
            blk_copy(b).start()
            return c

        def wait_blk(b, c):
            blk_copy(b).wait()
            return c

        pad_rows(start_row)
        lax.fori_loop(nu_ref[0], n_blocks, start_blk, 0)
        pad_rows(wait_row)
        lax.fori_loop(nu_ref[0], n_blocks, wait_blk, 0)

    def body(r, c):
        base = (i * tm + r) * TOP_K
        for k in range(TOP_K):
            d = dest_ref[base + k]
            pltpu.make_async_copy(h2_ref.at[pl.ds(r, 1)], xs_ref.at[pl.ds(d, 1)], sem).start()
        return c

    lax.fori_loop(0, tm, body, 0)
    _row_copy_wait(h2_ref, xs_ref.at[pl.ds(0, tm)], sem, TOP_K)


def _dispatch(dest, pad_lo, pad_hi, n_used, h2, n_blocks):
    t = h2.shape[0]
    tm = TOKEN_TILE
    bm = ROW_BLOCK
    return pl.pallas_call(
        _dispatch_kernel,
        grid_spec=pltpu.PrefetchScalarGridSpec(
            num_scalar_prefetch=4,
            grid=(t // tm,),
            in_specs=[pl.BlockSpec((tm, D_MODEL), lambda i, *_: (i, 0))],
            out_specs=pl.BlockSpec(memory_space=pl.ANY),
            scratch_shapes=[pltpu.VMEM((bm, D_MODEL), jnp.float32),
                            pltpu.SemaphoreType.DMA(()), pltpu.SemaphoreType.DMA(())],
        ),
        out_shape=jax.ShapeDtypeStruct((n_blocks * bm, D_MODEL), jnp.float32),
        compiler_params=pltpu.CompilerParams(
            dimension_semantics=("arbitrary",), vmem_limit_bytes=VMEM_LIMIT),
        name="dispatch",
    )(dest, pad_lo, pad_hi, n_used, h2)


def _experts_kernel(be_ref, nu_ref, xs_ref, wg_ref, bg_ref, wu_ref, bu_ref, wd_ref, bd_ref, ys_ref,
                    wbf_ref):
    b = pl.program_id(0)

    @pl.when(jnp.logical_or(b == 0, be_ref[b] != be_ref[jnp.maximum(b - 1, 0)]))
    def _():
        wbf_ref[0] = wg_ref[0].astype(jnp.bfloat16)
        wbf_ref[1] = wu_ref[0].astype(jnp.bfloat16)
        wbf_ref[2] = wd_ref[0].astype(jnp.bfloat16)

    @pl.when(b >= nu_ref[0])
    def _():
        ys_ref[...] = jnp.zeros_like(ys_ref)

    @pl.when(b < nu_ref[0])
    def _():
        xb = xs_ref[...].astype(jnp.bfloat16)
        gate = jnp.minimum(_dot(xb, wbf_ref[0]) + bg_ref[0], SWIGLU_LIMIT)
        up = jnp.clip(_dot(xb, wbf_ref[1]) + bu_ref[0], -SWIGLU_LIMIT, SWIGLU_LIMIT)
        act = (up + 1.0) * (gate * jax.nn.sigmoid(SWIGLU_ALPHA * gate))
        ys_ref[...] = _dot(act.astype(jnp.bfloat16), wbf_ref[2]) + bd_ref[0]


def _experts(block_e, n_used, xs, wg, bg, wu, bu, wd, bd, n_blocks):
    bm = ROW_BLOCK
    rows = lambda b, be, nu: (jnp.maximum(jnp.minimum(b, nu[0] - 1), 0), 0)
    wspec = lambda: pl.BlockSpec((1, D_MODEL, D_FF), lambda b, be, nu: (be[b], 0, 0))
    bspec = lambda: pl.BlockSpec((1, 1, D_FF), lambda b, be, nu: (be[b], 0, 0))
    return pl.pallas_call(
        _experts_kernel,
        grid_spec=pltpu.PrefetchScalarGridSpec(
            num_scalar_prefetch=2,
            grid=(n_blocks,),
            in_specs=[pl.BlockSpec((bm, D_MODEL), rows), wspec(), bspec(), wspec(), bspec(),
                      wspec(), bspec()],
            out_specs=pl.BlockSpec((bm, D_MODEL), lambda b, be, nu: (b, 0)),
            scratch_shapes=[pltpu.VMEM((3, D_MODEL, D_FF), jnp.bfloat16)],
        ),
        out_shape=jax.ShapeDtypeStruct((n_blocks * bm, D_MODEL), jnp.float32),
        compiler_params=pltpu.CompilerParams(
            dimension_semantics=("arbitrary",), vmem_limit_bytes=VMEM_LIMIT),
        name="experts",
    )(block_e, n_used, xs, wg, bg, wu, bu, wd, bd)


def _combine_kernel(dest_ref, ys_ref, x1_ref, w_ref, gfin_ref, o_ref, buf_ref, sem):
    i = pl.program_id(0)
    tm = x1_ref.shape[0]

    def body(r, c):
        base = (i * tm + r) * TOP_K
        for k in range(TOP_K):
            d = dest_ref[base + k]
            pltpu.make_async_copy(ys_ref.at[pl.ds(d, 1)], buf_ref.at[k, pl.ds(r, 1)], sem).start()
        return c

    lax.fori_loop(0, tm, body, 0)
    _row_copy_wait(ys_ref.at[pl.ds(0, tm)], buf_ref.at[0], sem, TOP_K)
    w = w_ref[...]
    y = x1_ref[...]
    for k in range(TOP_K):
        y = y + buf_ref[k] * w[:, k:k + 1]
    o_ref[...] = _rms(y, gfin_ref[...])


def _combine(dest, ys, x1, top_w, gfin):
    t = x1.shape[0]
    tm = TOKEN_TILE
    return pl.pallas_call(
        _combine_kernel,
        grid_spec=pltpu.PrefetchScalarGridSpec(
            num_scalar_prefetch=1,
            grid=(t // tm,),
            in_specs=[pl.BlockSpec(memory_space=pl.ANY),
                      pl.BlockSpec((tm, D_MODEL), lambda i, d: (i, 0)),
                      pl.BlockSpec((tm, TOP_K), lambda i, d: (i, 0)),
                      pl.BlockSpec((1, D_MODEL), lambda i, d: (0, 0))],
            out_specs=pl.BlockSpec((tm, D_MODEL), lambda i, d: (i, 0)),
            scratch_shapes=[pltpu.VMEM((TOP_K, tm, D_MODEL), jnp.float32),
                            pltpu.SemaphoreType.DMA(())],
        ),
        out_shape=jax.ShapeDtypeStruct((t, D_MODEL), jnp.float32),
        compiler_params=pltpu.CompilerParams(
            dimension_semantics=("arbitrary",), vmem_limit_bytes=VMEM_LIMIT),
        name="combine",
    )(dest, ys, x1, top_w, gfin)


def _rope_lane_table(rot, group):
    half = rot // 2
    inv = ROPE_THETA ** (-jnp.arange(0, rot, 2, dtype=jnp.float32) / rot)
    d = jnp.arange(LANES) % group
    first = jnp.arange(LANES) < (LANES if group < LANES else rot)
    in_lo = (d < half) & first
    in_hi = (d >= half) & (d < rot) & first
    freq = jnp.where(in_lo | in_hi, inv[d % half], 0.0)
    return jnp.stack([freq, in_lo.astype(jnp.float32), in_hi.astype(jnp.float32)]).astype(jnp.float32)


def kernel(x, positions, g_mix, w_in, lam_q1, lam_k1, lam_q2, lam_k2, g_subln, g_q, g_kv, w_uq, w_ukv, w_o_diff, w_o_mla, b_gates, w_out, g_ffn, w_router, b_router, w_gate, b_gate, w_up, b_up, w_down, b_down, g_final):
    batch, seq, d = x.shape
    t = batch * seq
    bf = jnp.bfloat16
    l = 0
    x2 = x.reshape(t, d)
    posf = positions.astype(jnp.float32).reshape(t, 1)

    w = w_in[l]
    win = jnp.concatenate([w[:, :IN_KR_END], jnp.zeros((d, LANES - MLA_ROPE), w.dtype),
                           w[:, IN_KR_END:]], axis=1).astype(bf)
    wuq = jnp.pad(w_uq[l].reshape(MLA_Q_LORA, MLA_HEADS, MLA_NOPE + MLA_ROPE),
                  ((0, 0), (0, 0), (0, MLA_QK_PAD - MLA_NOPE - MLA_ROPE))
                  ).reshape(MLA_Q_LORA, MLA_HEADS * MLA_QK_PAD).astype(bf)
    wukv = w_ukv[l].reshape(MLA_KV_LORA, MLA_HEADS, MLA_NOPE + MLA_V)
    wkn = wukv[:, :, :MLA_NOPE].reshape(MLA_KV_LORA, MLA_HEADS * MLA_NOPE).astype(bf)
    wv = wukv[:, :, MLA_NOPE:].reshape(MLA_KV_LORA, MLA_HEADS * MLA_V).astype(bf)
    fd = _rope_lane_table(DA_ROT, DA_HEAD_DIM)
    fm = _rope_lane_table(MLA_ROPE, LANES)

    qa, ka, va, gates, qm, km, vm = _proj(
        x2, posf, g_mix[l][None], win, b_gates[l][None],
        g_q[l][None], g_kv[l][None], wuq, wkn, wv, fd, fm)

    lamv = jnp.stack([lam_q1[l], lam_k1[l], lam_q2[l], lam_k2[l]]).astype(jnp.float32)
    oa = _dattn(qa, ka, va, lamv, g_subln[l][None], batch, seq)
    ob = _mattn(qm, km, vm, batch, seq)

    x1, h2, e_idx, top_w, rank, counts = _merge(
        x2, oa, ob, gates, w_o_diff[l].astype(bf), w_o_mla[l].astype(bf), w_out[l].astype(bf),
        g_ffn[l][None], w_router[l], b_router[l][None])

    bm = ROW_BLOCK
    n_blocks = (t * TOP_K) // bm + N_EXPERTS
    counts = counts[0]
    padded = (counts + bm - 1) // bm * bm
    padded_end = jnp.cumsum(padded)
    padded_start = padded_end - padded
    n_used = (padded_end[-1] // bm).astype(jnp.int32)
    blk = jnp.minimum(jnp.arange(n_blocks, dtype=jnp.int32), n_used - 1)
    block_e = jnp.minimum(jnp.sum(padded_end[None, :] <= (blk * bm)[:, None], axis=1),
                          N_EXPERTS - 1).astype(jnp.int32)
    hot = e_idx[:, :, None] == jnp.arange(N_EXPERTS, dtype=jnp.int32)
    dest = (jnp.sum(jnp.where(hot, padded_start, 0), axis=-1) + rank).reshape(-1).astype(jnp.int32)

    n_used = n_used.reshape(1)
    xs = _dispatch(dest, (padded_start + counts).astype(jnp.int32), padded_end.astype(jnp.int32),
                   n_used, h2, n_blocks)
    ys = _experts(block_e, n_used, xs, w_gate[l], b_gate[l][:, None, :], w_up[l],
                  b_up[l][:, None, :], w_down[l], b_down[l][:, None, :], n_blocks)
    out = _combine(dest, ys, x1, top_w, g_final[None])
    return out.reshape(batch, seq, d)
```

```python
import math

import jax
import jax.numpy as jnp
from jax import lax
from jax.experimental import pallas as pl
from jax.experimental.pallas import tpu as pltpu

D_MODEL = 1024
ROPE_THETA = 500000.0
NORM_EPS = 1e-6
DA_HEADS = 8
DA_HEAD_DIM = 64
DA_ROT = DA_HEAD_DIM // 4
MLA_HEADS = 8
MLA_Q_LORA = 768
MLA_KV_LORA = 512
MLA_NOPE = 128
MLA_ROPE = 64
MLA_V = 128
N_EXPERTS = 32
TOP_K = 4
D_FF = 1024
SWIGLU_ALPHA = 1.702
SWIGLU_LIMIT = 7.0
LAM_INIT = 0.8 - 0.6 * math.exp(-0.3 * 0)

LANES = 128
MLA_QK_PAD = 256
TOKEN_TILE = 256
DISPATCH_TILE = 512
Q_TILE = 256
KEY_CHUNK = 256
ROW_BLOCK = 256
ROW_DMA_UNROLL = 8
VMEM_LIMIT = 56 * 1024 * 1024

LOG2_E = math.log2(math.e)

IN_W = {"qa": DA_HEADS * 2 * DA_HEAD_DIM, "ka": DA_HEADS * 2 * DA_HEAD_DIM,
        "va": DA_HEADS * 2 * DA_HEAD_DIM, "cq": MLA_Q_LORA,
        "ckr": MLA_KV_LORA + LANES,
        "g": 2 * D_MODEL}
IN_OFF = dict(zip(IN_W, [sum(list(IN_W.values())[:n]) for n in range(len(IN_W))]))
IN_KR_END = IN_OFF["ckr"] + MLA_KV_LORA + MLA_ROPE

_NT = (((1,), (1,)), ((), ()))


def _rms(x, g):
    return x * lax.rsqrt(jnp.mean(x * x, axis=-1, keepdims=True) + NORM_EPS) * g


def _dot(a, b):
    return jnp.dot(a, b, preferred_element_type=jnp.float32)


def _rope_tables(pos, freq, m_lo, m_hi):
    ang = pos * freq
    c = jnp.cos(ang)
    s = jnp.sin(ang)
    return c, -s * m_lo, s * m_hi


def _rope_block(xb, tables, half):
    c, s_lo, s_hi = tables
    return (xb * c + pltpu.roll(xb, LANES - half, 1) * s_lo
            + pltpu.roll(xb, half, 1) * s_hi)


def _proj_kernel(x_ref, pos_ref, gmix_ref, win_ref, bg_ref, gq_ref, gkv_ref, wuq_ref, wkn_ref, wv_ref,
                 fd_ref, fm_ref,
                 qa_ref, ka_ref, va_ref, gates_ref, qm_ref, km_ref, vm_ref):
    hb = _rms(x_ref[...], gmix_ref[...]).astype(jnp.bfloat16)
    pos = pos_ref[...]
    td = _rope_tables(pos, fd_ref[0:1, :], fd_ref[1:2, :], fd_ref[2:3, :])
    tmla = _rope_tables(pos, fm_ref[0:1, :], fm_ref[1:2, :], fm_ref[2:3, :])

    da_scale = DA_HEAD_DIM ** -0.5 * LOG2_E
    w_cols = lambda name: win_ref[:, IN_OFF[name]:IN_OFF[name] + IN_W[name]]
    zq = _dot(hb, w_cols("qa"))
    for j in range(DA_HEADS):
        sl = slice(j * LANES, (j + 1) * LANES)
        qa_ref[:, sl] = (_rope_block(zq[:, sl], td, DA_ROT // 2) * da_scale).astype(jnp.bfloat16)
    zk = _dot(hb, w_cols("ka"))
    for j in range(DA_HEADS):
        sl = slice(j * LANES, (j + 1) * LANES)
        ka_ref[:, sl] = _rope_block(zk[:, sl], td, DA_ROT // 2).astype(jnp.bfloat16)
    va_ref[...] = _dot(hb, w_cols("va")).astype(jnp.bfloat16)
    gates_ref[...] = jax.nn.sigmoid(_dot(hb, w_cols("g")) + bg_ref[...]).astype(jnp.bfloat16)

    mla_scale = (MLA_NOPE + MLA_ROPE) ** -0.5 * LOG2_E
    cq = _rms(_dot(hb, w_cols("cq")), gq_ref[...]).astype(jnp.bfloat16)
    qm = _dot(cq, wuq_ref[...])
    for h in range(MLA_HEADS):
        lo = slice(h * MLA_QK_PAD, h * MLA_QK_PAD + LANES)
        hi = slice(h * MLA_QK_PAD + LANES, (h + 1) * MLA_QK_PAD)
        qm_ref[:, lo] = (qm[:, lo] * mla_scale).astype(jnp.bfloat16)
        qm_ref[:, hi] = (_rope_block(qm[:, hi], tmla, MLA_ROPE // 2) * mla_scale).astype(jnp.bfloat16)

    ck = _dot(hb, w_cols("ckr"))
    kr = _rope_block(ck[:, MLA_KV_LORA:], tmla, MLA_ROPE // 2).astype(jnp.bfloat16)
    ckv = _rms(ck[:, :MLA_KV_LORA], gkv_ref[...]).astype(jnp.bfloat16)
    kn = _dot(ckv, wkn_ref[...])
    for h in range(MLA_HEADS):
        km_ref[:, h * MLA_QK_PAD:h * MLA_QK_PAD + LANES] = kn[:, h * LANES:(h + 1) * LANES].astype(jnp.bfloat16)
        km_ref[:, h * MLA_QK_PAD + LANES:(h + 1) * MLA_QK_PAD] = kr
    vm_ref[...] = _dot(ckv, wv_ref[...]).astype(jnp.bfloat16)


def _const_spec(shape):
    return pl.BlockSpec(shape, lambda i: (0,) * len(shape))


def _proj(x2, posf, gmix, win, bg, gq, gkv, wuq, wkn, wv, fd, fm):
    t = x2.shape[0]
    tm = TOKEN_TILE
    bf = jnp.bfloat16
    row = lambda w: pl.BlockSpec((tm, w), lambda i: (i, 0))
    consts = (gmix, win, bg, gq, gkv, wuq, wkn, wv, fd, fm)
    out_w = (D_MODEL, D_MODEL, D_MODEL, 2 * D_MODEL, MLA_HEADS * MLA_QK_PAD,
             MLA_HEADS * MLA_QK_PAD, MLA_HEADS * MLA_V)
    return pl.pallas_call(
        _proj_kernel,
        grid=(t // tm,),
        in_specs=[row(D_MODEL), row(1)] + [_const_spec(c.shape) for c in consts],
        out_specs=[row(w) for w in out_w],
        out_shape=[jax.ShapeDtypeStruct((t, w), bf) for w in out_w],
        compiler_params=pltpu.CompilerParams(
            dimension_semantics=("parallel",), vmem_limit_bytes=VMEM_LIMIT),
        name="proj",
    )(x2, posf, *consts)


def _pipelined_tiles(n_tiles, scores, finish):
    scores(0, 0)

    def pair(j, c):
        t = 2 * j
        scores(t + 1, 1)
        finish(t, 0)
        scores(t + 2, 0)
        finish(t + 1, 1)
        return c

    lax.fori_loop(0, n_tiles // 2 - 1, pair, 0)
    scores(n_tiles - 1, 1)
    finish(n_tiles - 2, 0)
    finish(n_tiles - 1, 1)


def _q_rows(t):
    return pl.ds(pl.multiple_of(t * Q_TILE, Q_TILE), Q_TILE)


def _fill_values_ext(vx_ref, v_ref):
    width = v_ref.shape[1]
    vx_ref[:, :width] = v_ref[...]
    lane = lax.broadcasted_iota(jnp.int32, (v_ref.shape[0], vx_ref.shape[1] - width), 1)
    vx_ref[:, width:] = jnp.where(lane == 0, 1.0, 0.0).astype(vx_ref.dtype)


def _softmax_values(s_ref, vx_ref):
    m = jnp.max(s_ref[...], axis=-1, keepdims=True)
    acc = None
    for c in range(s_ref.shape[1] // KEY_CHUNK):
        cols = slice(c * KEY_CHUNK, (c + 1) * KEY_CHUNK)
        p = jnp.exp2(s_ref[:, cols] - m).astype(jnp.bfloat16)
        part = _dot(p, vx_ref[cols, :])
        acc = part if acc is None else acc + part
    return acc


def _dattn_kernel(q_ref, k_ref, v_ref, lam_ref, gsub_ref, o_ref, s0_ref, s1_ref, vx_ref):
    s_refs = (s0_ref, s1_ref)
    _fill_values_ext(vx_ref, v_ref)
    lamv = lam_ref[...]
    lam = (jnp.exp(jnp.sum(lamv[0:1] * lamv[1:2], axis=-1, keepdims=True))
           - jnp.exp(jnp.sum(lamv[2:3] * lamv[3:4], axis=-1, keepdims=True)) + LAM_INIT)
    width = v_ref.shape[1]

    def scores(t, slot):
        q = q_ref[_q_rows(t), :]
        lane = lax.broadcasted_iota(jnp.int32, q.shape, 1)
        zero = jnp.zeros_like(q)
        k = k_ref[...]
        s_refs[slot][0] = lax.dot_general(jnp.where(lane < DA_HEAD_DIM, q, zero), k, _NT,
                                          preferred_element_type=jnp.float32)
        s_refs[slot][1] = lax.dot_general(jnp.where(lane >= DA_HEAD_DIM, q, zero), k, _NT,
                                          preferred_element_type=jnp.float32)

    def finish(t, slot):
        ox0 = _softmax_values(s_refs[slot].at[0], vx_ref)
        ox1 = _softmax_values(s_refs[slot].at[1], vx_ref)
        o = (ox0[:, :width] * (1.0 / ox0[:, width:width + 1])
             - ox1[:, :width] * (lam / ox1[:, width:width + 1]))
        o_ref[_q_rows(t), :] = (_rms(o, gsub_ref[...]) * (1.0 - LAM_INIT)).astype(o_ref.dtype)

    _pipelined_tiles(q_ref.shape[0] // Q_TILE, scores, finish)


def _dattn(qa, ka, va, lamv, gsub, batch, seq):
    head = lambda b, h: (b, h)
    return pl.pallas_call(
        _dattn_kernel,
        grid=(batch, DA_HEADS),
        in_specs=[
            pl.BlockSpec((seq, LANES), head),
            pl.BlockSpec((seq, LANES), head),
            pl.BlockSpec((seq, LANES), head),
            pl.BlockSpec(lamv.shape, lambda b, h: (0, 0)),
            pl.BlockSpec(gsub.shape, lambda b, h: (0, 0)),
        ],
        out_specs=pl.BlockSpec((seq, LANES), head),
        out_shape=jax.ShapeDtypeStruct(qa.shape, jnp.bfloat16),
        scratch_shapes=[pltpu.VMEM((2, Q_TILE, seq), jnp.float32)] * 2
        + [pltpu.VMEM((seq, 2 * LANES), jnp.bfloat16)],
        compiler_params=pltpu.CompilerParams(
            dimension_semantics=("parallel", "parallel"), vmem_limit_bytes=VMEM_LIMIT),
        name="dattn",
    )(qa, ka, va, lamv, gsub)


def _mattn_kernel(q_ref, k_ref, v_ref, o_ref, s0_ref, s1_ref, vx_ref):
    s_refs = (s0_ref, s1_ref)
    _fill_values_ext(vx_ref, v_ref)
    width = v_ref.shape[1]

    def scores(t, slot):
        s_refs[slot][...] = lax.dot_general(q_ref[_q_rows(t), :], k_ref[...], _NT,
                                            preferred_element_type=jnp.float32)

    def finish(t, slot):
        ox = _softmax_values(s_refs[slot], vx_ref)
        o_ref[_q_rows(t), :] = (ox[:, :width] * (1.0 / ox[:, width:width + 1])).astype(o_ref.dtype)

    _pipelined_tiles(q_ref.shape[0] // Q_TILE, scores, finish)


def _mattn(qm, km, vm, batch, seq):
    head = lambda b, h: (b, h)
    return pl.pallas_call(
        _mattn_kernel,
        grid=(batch, MLA_HEADS),
        in_specs=[
            pl.BlockSpec((seq, MLA_QK_PAD), head),
            pl.BlockSpec((seq, MLA_QK_PAD), head),
            pl.BlockSpec((seq, MLA_V), head),
        ],
        out_specs=pl.BlockSpec((seq, MLA_V), head),
        out_shape=jax.ShapeDtypeStruct(vm.shape, jnp.bfloat16),
        scratch_shapes=[pltpu.VMEM((Q_TILE, seq), jnp.float32)] * 2
        + [pltpu.VMEM((seq, 2 * MLA_V), jnp.bfloat16)],
        compiler_params=pltpu.CompilerParams(
            dimension_semantics=("parallel", "parallel"), vmem_limit_bytes=VMEM_LIMIT),
        name="mattn",
    )(qm, km, vm)


def _cols(parts, width, dtype):
    lane = lax.broadcasted_iota(jnp.int32, (parts[0].shape[0], width), 1)
    out = jnp.zeros((parts[0].shape[0], width), dtype)
    for k, p in enumerate(parts):
        out = jnp.where(lane == k, p.astype(dtype), out)
    return out


def _merge_kernel(x_ref, oa_ref, ob_ref, gates_ref, woa_ref, wob_ref, wout_ref, gffn_ref,
                  wr_ref, br_ref,
                  x1_ref, h2_ref, idx_ref, w_ref, rank_ref, counts_ref, carry_ref):
    i = pl.program_id(0)

    @pl.when(i == 0)
    def _():
        carry_ref[...] = jnp.zeros_like(carry_ref)

    gates = gates_ref[...].astype(jnp.float32)
    merged = (gates[:, :D_MODEL] * _dot(oa_ref[...], woa_ref[...])
              + gates[:, D_MODEL:] * _dot(ob_ref[...], wob_ref[...]))
    x1 = x_ref[...] + _dot(merged.astype(jnp.bfloat16), wout_ref[...])
    x1_ref[...] = x1
    h2 = _rms(x1, gffn_ref[...])
    h2_ref[...] = h2

    logits = jnp.dot(h2, wr_ref[...], precision=lax.Precision.HIGHEST,
                     preferred_element_type=jnp.float32) + br_ref[...]
    tm = logits.shape[0]
    lane = lax.broadcasted_iota(jnp.int32, logits.shape, 1)
    vals, idxs, hots = [], [], []
    l = logits
    for _ in range(TOP_K):
        m = jnp.max(l, axis=-1, keepdims=True)
        idx = jnp.min(jnp.where(l == m, lane, N_EXPERTS), axis=-1, keepdims=True)
        hot = lane == idx
        vals.append(m)
        idxs.append(idx)
        hots.append(hot)
        l = jnp.where(hot, -jnp.inf, l)
    es = [jnp.exp(v - vals[0]) for v in vals]
    den = es[0] + es[1] + es[2] + es[3]
    w_ref[...] = _cols([e / den for e in es], TOP_K, jnp.float32)
    idx_ref[...] = _cols(idxs, TOP_K, jnp.int32)

    chosen = (hots[0] | hots[1] | hots[2] | hots[3]).astype(jnp.float32)
    r_i = lax.broadcasted_iota(jnp.int32, (tm, tm), 0)
    c_i = lax.broadcasted_iota(jnp.int32, (tm, tm), 1)
    lower = (c_i < r_i).astype(jnp.bfloat16)
    prefix = _dot(lower, chosen.astype(jnp.bfloat16)) + carry_ref[...]
    ranks = [jnp.sum(jnp.where(h, prefix, 0.0), axis=-1, keepdims=True) for h in hots]
    rank_ref[...] = _cols(ranks, TOP_K, jnp.int32)
    carry = carry_ref[...] + jnp.sum(chosen, axis=0, keepdims=True)
    carry_ref[...] = carry
    counts_ref[...] = carry.astype(jnp.int32)


def _merge(x2, oa, ob, gates, woa, wob, wout, gffn, wr, br):
    t = x2.shape[0]
    tm = TOKEN_TILE
    row = lambda w: pl.BlockSpec((tm, w), lambda i: (i, 0))
    consts = (woa, wob, wout, gffn, wr, br)
    return pl.pallas_call(
        _merge_kernel,
        grid=(t // tm,),
        in_specs=[row(D_MODEL), row(D_MODEL), row(D_MODEL), row(2 * D_MODEL)]
        + [_const_spec(c.shape) for c in consts],
        out_specs=[row(D_MODEL), row(D_MODEL), row(TOP_K), row(TOP_K), row(TOP_K),
                   _const_spec((1, N_EXPERTS))],
        out_shape=[
            jax.ShapeDtypeStruct((t, D_MODEL), jnp.float32),
            jax.ShapeDtypeStruct((t, D_MODEL), jnp.float32),
            jax.ShapeDtypeStruct((t, TOP_K), jnp.int32),
            jax.ShapeDtypeStruct((t, TOP_K), jnp.float32),
            jax.ShapeDtypeStruct((t, TOP_K), jnp.int32),
            jax.ShapeDtypeStruct((1, N_EXPERTS), jnp.int32),
        ],
        scratch_shapes=[pltpu.VMEM((1, N_EXPERTS), jnp.float32)],
        compiler_params=pltpu.CompilerParams(
            dimension_semantics=("arbitrary",), vmem_limit_bytes=VMEM_LIMIT),
        name="merge",
    )(x2, oa, ob, gates, *consts)


def _row_copy_wait(src_rows_ref, dst_rows_ref, sem, n):
    for _ in range(n):
        pltpu.make_async_copy(src_rows_ref, dst_rows_ref, sem).wait()


def _dispatch_kernel(dest_ref, pad_lo_ref, pad_hi_ref, nu_ref, h2_ref, xs_ref, zero_ref, sem, zsem):
    i = pl.program_id(0)
    tm = h2_ref.shape[0]
    bm = zero_ref.shape[0]
    n_blocks = xs_ref.shape[0] // bm

    @pl.when(i == 0)
    def _():
        zero_ref[...] = jnp.zeros_like(zero_ref)

        def pad_rows(fn):
            def per_expert(e, c):
                return lax.fori_loop(pad_lo_ref[e], pad_hi_ref[e], fn, c)
            lax.fori_loop(0, N_EXPERTS, per_expert, 0)

        def row_copy(j):
            return pltpu.make_async_copy(zero_ref.at[pl.ds(0, 1)], xs_ref.at[pl.ds(j, 1)], zsem)

        def blk_copy(b):
            return pltpu.make_async_copy(zero_ref, xs_ref.at[pl.ds(pl.multiple_of(b * bm, bm), bm)], zsem)

        def start_row(j, c):
            row_copy(j).start()
            return c

        def wait_row(j, c):
            row_copy(j).wait()
            return c

        def start_blk(b, c):
            blk_copy(b).start()
            return c

        def wait_blk(b, c):
            blk_copy(b).wait()
            return c

        pad_rows(start_row)
        lax.fori_loop(nu_ref[0], n_blocks, start_blk, 0)
        pad_rows(wait_row)
        lax.fori_loop(nu_ref[0], n_blocks, wait_blk, 0)

    def body(r, c):
        base = (i * tm + r) * TOP_K
        for k in range(TOP_K):
            d = dest_ref[base + k]
            pltpu.make_async_copy(h2_ref.at[pl.ds(r, 1)], xs_ref.at[pl.ds(d, 1)], sem).start()
        return c

    lax.fori_loop(0, tm, body, 0, unroll=ROW_DMA_UNROLL)
    _row_copy_wait(h2_ref, xs_ref.at[pl.ds(0, tm)], sem, TOP_K)


def _dispatch(dest, pad_lo, pad_hi, n_used, h2, n_blocks):
    t = h2.shape[0]
    tm = DISPATCH_TILE
    bm = ROW_BLOCK
    return pl.pallas_call(
        _dispatch_kernel,
        grid_spec=pltpu.PrefetchScalarGridSpec(
            num_scalar_prefetch=4,
            grid=(t // tm,),
            in_specs=[pl.BlockSpec((tm, D_MODEL), lambda i, *_: (i, 0))],
            out_specs=pl.BlockSpec(memory_space=pl.ANY),
            scratch_shapes=[pltpu.VMEM((bm, D_MODEL), jnp.float32),
                            pltpu.SemaphoreType.DMA(()), pltpu.SemaphoreType.DMA(())],
        ),
        out_shape=jax.ShapeDtypeStruct((n_blocks * bm, D_MODEL), jnp.float32),
        compiler_params=pltpu.CompilerParams(
            dimension_semantics=("arbitrary",), vmem_limit_bytes=VMEM_LIMIT),
        name="dispatch",
    )(dest, pad_lo, pad_hi, n_used, h2)


def _experts_kernel(be_ref, nu_ref, xs_ref, wg_ref, bg_ref, wu_ref, bu_ref, wd_ref, bd_ref, ys_ref,
                    wbf_ref):
    b = pl.program_id(0)

    @pl.when(jnp.logical_or(b == 0, be_ref[b] != be_ref[jnp.maximum(b - 1, 0)]))
    def _():
        wbf_ref[0] = wg_ref[0].astype(jnp.bfloat16)
        wbf_ref[1] = wu_ref[0].astype(jnp.bfloat16)
        wbf_ref[2] = wd_ref[0].astype(jnp.bfloat16)

    @pl.when(b >= nu_ref[0])
    def _():
        ys_ref[...] = jnp.zeros_like(ys_ref)

    @pl.when(b < nu_ref[0])
    def _():
        xb = xs_ref[...].astype(jnp.bfloat16)
        gate = jnp.minimum(_dot(xb, wbf_ref[0]) + bg_ref[0], SWIGLU_LIMIT)
        up = jnp.clip(_dot(xb, wbf_ref[1]) + bu_ref[0], -SWIGLU_LIMIT, SWIGLU_LIMIT)
        act = (up + 1.0) * (gate * jax.nn.sigmoid(SWIGLU_ALPHA * gate))
        ys_ref[...] = _dot(act.astype(jnp.bfloat16), wbf_ref[2]) + bd_ref[0]


def _experts(block_e, n_used, xs, wg, bg, wu, bu, wd, bd, n_blocks):
    bm = ROW_BLOCK
    rows = lambda b, be, nu: (jnp.maximum(jnp.minimum(b, nu[0] - 1), 0), 0)
    wspec = lambda: pl.BlockSpec((1, D_MODEL, D_FF), lambda b, be, nu: (be[b], 0, 0))
    bspec = lambda: pl.BlockSpec((1, 1, D_FF), lambda b, be, nu: (be[b], 0, 0))
    return pl.pallas_call(
        _experts_kernel,
        grid_spec=pltpu.PrefetchScalarGridSpec(
            num_scalar_prefetch=2,
            grid=(n_blocks,),
            in_specs=[pl.BlockSpec((bm, D_MODEL), rows), wspec(), bspec(), wspec(), bspec(),
                      wspec(), bspec()],
            out_specs=pl.BlockSpec((bm, D_MODEL), lambda b, be, nu: (b, 0)),
            scratch_shapes=[pltpu.VMEM((3, D_MODEL, D_FF), jnp.bfloat16)],
        ),
        out_shape=jax.ShapeDtypeStruct((n_blocks * bm, D_MODEL), jnp.float32),
        compiler_params=pltpu.CompilerParams(
            dimension_semantics=("arbitrary",), vmem_limit_bytes=VMEM_LIMIT),
        name="experts",
    )(block_e, n_used, xs, wg, bg, wu, bu, wd, bd)


def _combine_kernel(dest_ref, ys_ref, x1_ref, w_ref, gfin_ref, o_ref, buf_ref, sem):
    i = pl.program_id(0)
    tm = x1_ref.shape[0]

    def gather(tile):
        slot = lax.rem(tile, 2)

        def body(r, c):
            base = (tile * tm + r) * TOP_K
            for k in range(TOP_K):
                d = dest_ref[base + k]
                pltpu.make_async_copy(ys_ref.at[pl.ds(d, 1)], buf_ref.at[slot, k, pl.ds(r, 1)],
                                      sem.at[slot]).start()
            return c

        lax.fori_loop(0, tm, body, 0, unroll=ROW_DMA_UNROLL)

    @pl.when(i == 0)
    def _():
        gather(i)

    @pl.when(i + 1 < pl.num_programs(0))
    def _():
        gather(i + 1)

    slot = lax.rem(i, 2)
    _row_copy_wait(ys_ref.at[pl.ds(0, tm)], buf_ref.at[slot, 0], sem.at[slot], TOP_K)
    w = w_ref[...]
    y = x1_ref[...]
    for k in range(TOP_K):
        y = y + buf_ref[slot, k] * w[:, k:k + 1]
    o_ref[...] = _rms(y, gfin_ref[...])


def _combine(dest, ys, x1, top_w, gfin):
    t = x1.shape[0]
    tm = TOKEN_TILE
    return pl.pallas_call(
        _combine_kernel,
        grid_spec=pltpu.PrefetchScalarGridSpec(
            num_scalar_prefetch=1,
            grid=(t // tm,),
            in_specs=[pl.BlockSpec(memory_space=pl.ANY),
                      pl.BlockSpec((tm, D_MODEL), lambda i, d: (i, 0)),
                      pl.BlockSpec((tm, TOP_K), lambda i, d: (i, 0)),
                      pl.BlockSpec((1, D_MODEL), lambda i, d: (0, 0))],
            out_specs=pl.BlockSpec((tm, D_MODEL), lambda i, d: (i, 0)),
            scratch_shapes=[pltpu.VMEM((2, TOP_K, tm, D_MODEL), jnp.float32),
                            pltpu.SemaphoreType.DMA((2,))],
        ),
        out_shape=jax.ShapeDtypeStruct((t, D_MODEL), jnp.float32),
        compiler_params=pltpu.CompilerParams(
            dimension_semantics=("arbitrary",), vmem_limit_bytes=VMEM_LIMIT),
        name="combine",
    )(dest, ys, x1, top_w, gfin)


def _rope_lane_table(rot, group):
    half = rot // 2
    inv = ROPE_THETA ** (-jnp.arange(0, rot, 2, dtype=jnp.float32) / rot)
    d = jnp.arange(LANES) % group
    first = jnp.arange(LANES) < (LANES if group < LANES else rot)
    in_lo = (d < half) & first
    in_hi = (d >= half) & (d < rot) & first
    freq = jnp.where(in_lo | in_hi, inv[d % half], 0.0)
    return jnp.stack([freq, in_lo.astype(jnp.float32), in_hi.astype(jnp.float32)]).astype(jnp.float32)


def kernel(x, positions, g_mix, w_in, lam_q1, lam_k1, lam_q2, lam_k2, g_subln, g_q, g_kv, w_uq, w_ukv, w_o_diff, w_o_mla, b_gates, w_out, g_ffn, w_router, b_router, w_gate, b_gate, w_up, b_up, w_down, b_down, g_final):
    batch, seq, d = x.shape
    t = batch * seq
    bf = jnp.bfloat16
    l = 0
    x2 = x.reshape(t, d)
    posf = positions.astype(jnp.float32).reshape(t, 1)

    w = w_in[l]
    win = jnp.concatenate([w[:, :IN_KR_END], jnp.zeros((d, LANES - MLA_ROPE), w.dtype),
                           w[:, IN_KR_END:]], axis=1).astype(bf)
    wuq = jnp.pad(w_uq[l].reshape(MLA_Q_LORA, MLA_HEADS, MLA_NOPE + MLA_ROPE),
                  ((0, 0), (0, 0), (0, MLA_QK_PAD - MLA_NOPE - MLA_ROPE))
                  ).reshape(MLA_Q_LORA, MLA_HEADS * MLA_QK_PAD).astype(bf)
    wukv = w_ukv[l].reshape(MLA_KV_LORA, MLA_HEADS, MLA_NOPE + MLA_V)
    wkn = wukv[:, :, :MLA_NOPE].reshape(MLA_KV_LORA, MLA_HEADS * MLA_NOPE).astype(bf)
    wv = wukv[:, :, MLA_NOPE:].reshape(MLA_KV_LORA, MLA_HEADS * MLA_V).astype(bf)
    fd = _rope_lane_table(DA_ROT, DA_HEAD_DIM)
    fm = _rope_lane_table(MLA_ROPE, LANES)

    qa, ka, va, gates, qm, km, vm = _proj(
        x2, posf, g_mix[l][None], win, b_gates[l][None],
        g_q[l][None], g_kv[l][None], wuq, wkn, wv, fd, fm)

    lamv = jnp.stack([lam_q1[l], lam_k1[l], lam_q2[l], lam_k2[l]]).astype(jnp.float32)
    oa = _dattn(qa, ka, va, lamv, g_subln[l][None], batch, seq)
    ob = _mattn(qm, km, vm, batch, seq)

    x1, h2, e_idx, top_w, rank, counts = _merge(
        x2, oa, ob, gates, w_o_diff[l].astype(bf), w_o_mla[l].astype(bf), w_out[l].astype(bf),
        g_ffn[l][None], w_router[l], b_router[l][None])

    bm = ROW_BLOCK
    n_blocks = (t * TOP_K) // bm + N_EXPERTS
    counts = counts[0]
    padded = (counts + bm - 1) // bm * bm
    padded_end = jnp.cumsum(padded)
    padded_start = padded_end - padded
    n_used = (padded_end[-1] // bm).astype(jnp.int32)
    blk = jnp.minimum(jnp.arange(n_blocks, dtype=jnp.int32), n_used - 1)
    block_e = jnp.minimum(jnp.sum(padded_end[None, :] <= (blk * bm)[:, None], axis=1),
                          N_EXPERTS - 1).astype(jnp.int32)
    hot = e_idx[:, :, None] == jnp.arange(N_EXPERTS, dtype=jnp.int32)
    dest = (jnp.sum(jnp.where(hot, padded_start, 0), axis=-1) + rank).reshape(-1).astype(jnp.int32)

    n_used = n_used.reshape(1)
    xs = _dispatch(dest, (padded_start + counts).astype(jnp.int32), padded_end.astype(jnp.int32),
                   n_used, h2, n_blocks)
    ys = _experts(block_e, n_used, xs, w_gate[l], b_gate[l][:, None, :], w_up[l],
                  b_up[l][:, None, :], w_down[l], b_down[l][:, None, :], n_blocks)
    out = _combine(dest, ys, x1, top_w, g_final[None])
    return out.reshape(batch, seq, d)
```

```python
import math

import jax
import jax.numpy as jnp
from jax import lax
from jax.experimental import pallas as pl
from jax.experimental.pallas import tpu as pltpu

D_MODEL = 1024
ROPE_THETA = 500000.0
NORM_EPS = 1e-6
DA_HEADS = 8
DA_HEAD_DIM = 64
DA_ROT = DA_HEAD_DIM // 4
MLA_HEADS = 8
MLA_Q_LORA = 768
MLA_KV_LORA = 512
MLA_NOPE = 128
MLA_ROPE = 64
MLA_V = 128
N_EXPERTS = 32
TOP_K = 4
D_FF = 1024
SWIGLU_ALPHA = 1.702
SWIGLU_LIMIT = 7.0
LAM_INIT = 0.8 - 0.6 * math.exp(-0.3 * 0)

LANES = 128
MLA_QK_PAD = 256
TOKEN_TILE = 256
DISPATCH_TILE = 512
Q_TILE = 256
KEY_CHUNK = 256
ROW_BLOCK = 256
ROW_DMA_UNROLL = 8
VMEM_LIMIT = 56 * 1024 * 1024

LOG2_E = math.log2(math.e)

IN_W = {"qa": DA_HEADS * 2 * DA_HEAD_DIM, "ka": DA_HEADS * 2 * DA_HEAD_DIM,
        "va": DA_HEADS * 2 * DA_HEAD_DIM, "cq": MLA_Q_LORA,
        "ckr": MLA_KV_LORA + LANES,
        "g": 2 * D_MODEL}
IN_OFF = dict(zip(IN_W, [sum(list(IN_W.values())[:n]) for n in range(len(IN_W))]))
IN_KR_END = IN_OFF["ckr"] + MLA_KV_LORA + MLA_ROPE

_NT = (((1,), (1,)), ((), ()))


def _rms(x, g):
    return x * lax.rsqrt(jnp.mean(x * x, axis=-1, keepdims=True) + NORM_EPS) * g


def _dot(a, b):
    return jnp.dot(a, b, preferred_element_type=jnp.float32)


def _rope_tables(pos, freq, m_lo, m_hi):
    ang = pos * freq
    c = jnp.cos(ang)
    s = jnp.sin(ang)
    return c, -s * m_lo, s * m_hi


def _rope_block(xb, tables, half):
    c, s_lo, s_hi = tables
    return (xb * c + pltpu.roll(xb, LANES - half, 1) * s_lo
            + pltpu.roll(xb, half, 1) * s_hi)


def _proj_kernel(x_ref, pos_ref, gmix_ref, win_ref, bg_ref, gq_ref, gkv_ref, wuq_ref, wkn_ref, wv_ref,
                 fd_ref, fm_ref,
                 qa_ref, ka_ref, va_ref, gates_ref, qm_ref, km_ref, vm_ref):
    hb = _rms(x_ref[...], gmix_ref[...]).astype(jnp.bfloat16)
    pos = pos_ref[...]
    td = _rope_tables(pos, fd_ref[0:1, :], fd_ref[1:2, :], fd_ref[2:3, :])
    tmla = _rope_tables(pos, fm_ref[0:1, :], fm_ref[1:2, :], fm_ref[2:3, :])

    da_scale = DA_HEAD_DIM ** -0.5 * LOG2_E
    w_cols = lambda name: win_ref[:, IN_OFF[name]:IN_OFF[name] + IN_W[name]]
    mla_scale = (MLA_NOPE + MLA_ROPE) ** -0.5 * LOG2_E

    cq = _rms(_dot(hb, w_cols("cq")), gq_ref[...]).astype(jnp.bfloat16)
    ck = _dot(hb, w_cols("ckr"))
    kr = _rope_block(ck[:, MLA_KV_LORA:], tmla, MLA_ROPE // 2).astype(jnp.bfloat16)
    ckv = _rms(ck[:, :MLA_KV_LORA], gkv_ref[...]).astype(jnp.bfloat16)

    zq = _dot(hb, w_cols("qa"))
    for j in range(DA_HEADS):
        sl = slice(j * LANES, (j + 1) * LANES)
        qa_ref[:, sl] = (_rope_block(zq[:, sl], td, DA_ROT // 2) * da_scale).astype(jnp.bfloat16)

    qm = _dot(cq, wuq_ref[...])
    for h in range(MLA_HEADS):
        lo = slice(h * MLA_QK_PAD, h * MLA_QK_PAD + LANES)
        hi = slice(h * MLA_QK_PAD + LANES, (h + 1) * MLA_QK_PAD)
        qm_ref[:, lo] = (qm[:, lo] * mla_scale).astype(jnp.bfloat16)
        qm_ref[:, hi] = (_rope_block(qm[:, hi], tmla, MLA_ROPE // 2) * mla_scale).astype(jnp.bfloat16)

    zk = _dot(hb, w_cols("ka"))
    for j in range(DA_HEADS):
        sl = slice(j * LANES, (j + 1) * LANES)
        ka_ref[:, sl] = _rope_block(zk[:, sl], td, DA_ROT // 2).astype(jnp.bfloat16)

    kn = _dot(ckv, wkn_ref[...])
    for h in range(MLA_HEADS):
        km_ref[:, h * MLA_QK_PAD:h * MLA_QK_PAD + LANES] = kn[:, h * LANES:(h + 1) * LANES].astype(jnp.bfloat16)
        km_ref[:, h * MLA_QK_PAD + LANES:(h + 1) * MLA_QK_PAD] = kr
    vm_ref[...] = _dot(ckv, wv_ref[...]).astype(jnp.bfloat16)
    gates_ref[...] = jax.nn.sigmoid(_dot(hb, w_cols("g")) + bg_ref[...]).astype(jnp.bfloat16)
    va_ref[...] = _dot(hb, w_cols("va")).astype(jnp.bfloat16)


def _const_spec(shape):
    return pl.BlockSpec(shape, lambda i: (0,) * len(shape))


def _proj(x2, posf, gmix, win, bg, gq, gkv, wuq, wkn, wv, fd, fm):
    t = x2.shape[0]
    tm = TOKEN_TILE
    bf = jnp.bfloat16
    row = lambda w: pl.BlockSpec((tm, w), lambda i: (i, 0))
    consts = (gmix, win, bg, gq, gkv, wuq, wkn, wv, fd, fm)
    out_w = (D_MODEL, D_MODEL, D_MODEL, 2 * D_MODEL, MLA_HEADS * MLA_QK_PAD,
             MLA_HEADS * MLA_QK_PAD, MLA_HEADS * MLA_V)
    return pl.pallas_call(
        _proj_kernel,
        grid=(t // tm,),
        in_specs=[row(D_MODEL), row(1)] + [_const_spec(c.shape) for c in consts],
        out_specs=[row(w) for w in out_w],
        out_shape=[jax.ShapeDtypeStruct((t, w), bf) for w in out_w],
        compiler_params=pltpu.CompilerParams(
            dimension_semantics=("parallel",), vmem_limit_bytes=VMEM_LIMIT),
        name="proj",
    )(x2, posf, *consts)


def _pipelined_tiles(n_tiles, n_chunks, score_chunk, max_merge, max_store, max_load,
                     value_chunk, write_out):
    def fused(t_next, slot_next, t_cur, slot_cur):
        m_cur = None if t_cur is None else max_load(slot_cur)
        m_next, acc = None, None
        for c in range(n_chunks):
            if t_next is not None:
                m_next = max_merge(m_next, score_chunk(t_next, slot_next, c))
            if t_cur is not None:
                acc = value_chunk(slot_cur, c, m_cur, acc)
        if t_next is not None:
            max_store(slot_next, m_next)
        if t_cur is not None:
            write_out(t_cur, acc)

    fused(0, 0, None, None)

    def pair(j, carry):
        t = 2 * j
        fused(t + 1, 1, t, 0)
        fused(t + 2, 0, t + 1, 1)
        return carry

    lax.fori_loop(0, n_tiles // 2 - 1, pair, 0)
    fused(n_tiles - 1, 1, n_tiles - 2, 0)
    fused(None, None, n_tiles - 1, 1)


def _q_rows(t):
    return pl.ds(pl.multiple_of(t * Q_TILE, Q_TILE), Q_TILE)


def _key_cols(c):
    return slice(c * KEY_CHUNK, (c + 1) * KEY_CHUNK)


def _fill_values_ext(vx_ref, v_ref):
    width = v_ref.shape[1]
    vx_ref[:, :width] = v_ref[...]
    lane = lax.broadcasted_iota(jnp.int32, (v_ref.shape[0], vx_ref.shape[1] - width), 1)
    vx_ref[:, width:] = jnp.where(lane == 0, 1.0, 0.0).astype(vx_ref.dtype)


def _prob_values(s_ref, vx_ref, c, m, acc):
    p = jnp.exp2(s_ref[:, _key_cols(c)] - m).astype(jnp.bfloat16)
    part = _dot(p, vx_ref[_key_cols(c), :])
    return part if acc is None else acc + part


def _dattn_kernel(q_ref, k_ref, v_ref, lam_ref, gsub_ref, o_ref,
                  s0_ref, s1_ref, m0_ref, m1_ref, vx_ref):
    s_refs, m_refs = (s0_ref, s1_ref), (m0_ref, m1_ref)
    _fill_values_ext(vx_ref, v_ref)
    lamv = lam_ref[...]
    lam = (jnp.exp(jnp.sum(lamv[0:1] * lamv[1:2], axis=-1, keepdims=True))
           - jnp.exp(jnp.sum(lamv[2:3] * lamv[3:4], axis=-1, keepdims=True)) + LAM_INIT)
    width = v_ref.shape[1]

    def score_chunk(t, slot, c):
        q = q_ref[_q_rows(t), :]
        lane = lax.broadcasted_iota(jnp.int32, q.shape, 1)
        zero = jnp.zeros_like(q)
        k = k_ref[_key_cols(c), :]
        sc0 = lax.dot_general(jnp.where(lane < DA_HEAD_DIM, q, zero), k, _NT,
                              preferred_element_type=jnp.float32)
        sc1 = lax.dot_general(jnp.where(lane >= DA_HEAD_DIM, q, zero), k, _NT,
                              preferred_element_type=jnp.float32)
        s_refs[slot][0, :, _key_cols(c)] = sc0
        s_refs[slot][1, :, _key_cols(c)] = sc1
        return (jnp.max(sc0, axis=-1, keepdims=True), jnp.max(sc1, axis=-1, keepdims=True))

    def max_merge(m, mc):
        return mc if m is None else (jnp.maximum(m[0], mc[0]), jnp.maximum(m[1], mc[1]))

    def max_store(slot, m):
        m_refs[slot][0] = m[0]
        m_refs[slot][1] = m[1]

    def max_load(slot):
        return m_refs[slot][0], m_refs[slot][1]

    def value_chunk(slot, c, m, acc):
        acc0, acc1 = (None, None) if acc is None else acc
        return (_prob_values(s_refs[slot].at[0], vx_ref, c, m[0], acc0),
                _prob_values(s_refs[slot].at[1], vx_ref, c, m[1], acc1))

    def write_out(t, acc):
        ox0, ox1 = acc
        o = (ox0[:, :width] * (1.0 / ox0[:, width:width + 1])
             - ox1[:, :width] * (lam / ox1[:, width:width + 1]))
        o_ref[_q_rows(t), :] = (_rms(o, gsub_ref[...]) * (1.0 - LAM_INIT)).astype(o_ref.dtype)

    _pipelined_tiles(q_ref.shape[0] // Q_TILE, k_ref.shape[0] // KEY_CHUNK,
                     score_chunk, max_merge, max_store, max_load, value_chunk, write_out)


def _dattn(qa, ka, va, lamv, gsub, batch, seq):
    head = lambda b, h: (b, h)
    return pl.pallas_call(
        _dattn_kernel,
        grid=(batch, DA_HEADS),
        in_specs=[
            pl.BlockSpec((seq, LANES), head),
            pl.BlockSpec((seq, LANES), head),
            pl.BlockSpec((seq, LANES), head),
            pl.BlockSpec(lamv.shape, lambda b, h: (0, 0)),
            pl.BlockSpec(gsub.shape, lambda b, h: (0, 0)),
        ],
        out_specs=pl.BlockSpec((seq, LANES), head),
        out_shape=jax.ShapeDtypeStruct(qa.shape, jnp.bfloat16),
        scratch_shapes=[pltpu.VMEM((2, Q_TILE, seq), jnp.float32)] * 2
        + [pltpu.VMEM((2, Q_TILE, 1), jnp.float32)] * 2
        + [pltpu.VMEM((seq, 2 * LANES), jnp.bfloat16)],
        compiler_params=pltpu.CompilerParams(
            dimension_semantics=("parallel", "parallel"), vmem_limit_bytes=VMEM_LIMIT),
        name="dattn",
    )(qa, ka, va, lamv, gsub)


def _mattn_kernel(q_ref, k_ref, v_ref, o_ref, s0_ref, s1_ref, m0_ref, m1_ref, vx_ref):
    s_refs, m_refs = (s0_ref, s1_ref), (m0_ref, m1_ref)
    _fill_values_ext(vx_ref, v_ref)
    width = v_ref.shape[1]

    def score_chunk(t, slot, c):
        sc = lax.dot_general(q_ref[_q_rows(t), :], k_ref[_key_cols(c), :], _NT,
                             preferred_element_type=jnp.float32)
        s_refs[slot][:, _key_cols(c)] = sc
        return jnp.max(sc, axis=-1, keepdims=True)

    def max_merge(m, mc):
        return mc if m is None else jnp.maximum(m, mc)

    def max_store(slot, m):
        m_refs[slot][...] = m

    def max_load(slot):
        return m_refs[slot][...]

    def value_chunk(slot, c, m, acc):
        return _prob_values(s_refs[slot], vx_ref, c, m, acc)

    def write_out(t, ox):
        o_ref[_q_rows(t), :] = (ox[:, :width] * (1.0 / ox[:, width:width + 1])).astype(o_ref.dtype)

    _pipelined_tiles(q_ref.shape[0] // Q_TILE, k_ref.shape[0] // KEY_CHUNK,
                     score_chunk, max_merge, max_store, max_load, value_chunk, write_out)


def _mattn(qm, km, vm, batch, seq):
    head = lambda b, h: (b, h)
    return pl.pallas_call(
        _mattn_kernel,
        grid=(batch, MLA_HEADS),
        in_specs=[
            pl.BlockSpec((seq, MLA_QK_PAD), head),
            pl.BlockSpec((seq, MLA_QK_PAD), head),
            pl.BlockSpec((seq, MLA_V), head),
        ],
        out_specs=pl.BlockSpec((seq, MLA_V), head),
        out_shape=jax.ShapeDtypeStruct(vm.shape, jnp.bfloat16),
        scratch_shapes=[pltpu.VMEM((Q_TILE, seq), jnp.float32)] * 2
        + [pltpu.VMEM((Q_TILE, 1), jnp.float32)] * 2
        + [pltpu.VMEM((seq, 2 * MLA_V), jnp.bfloat16)],
        compiler_params=pltpu.CompilerParams(
            dimension_semantics=("parallel", "parallel"), vmem_limit_bytes=VMEM_LIMIT),
        name="mattn",
    )(qm, km, vm)


def _rows(parts, dtype):
    sub = lax.broadcasted_iota(jnp.int32, (len(parts), parts[0].shape[1]), 0)
    out = jnp.zeros(sub.shape, dtype)
    for k, p in enumerate(parts):
        out = jnp.where(sub == k, p.astype(dtype), out)
    return out


def _merge_kernel(x_ref, oa_ref, ob_ref, gates_ref, woa_ref, wob_ref, wout_ref, gffn_ref,
                  wrt_ref, brt_ref,
                  x1_ref, h2_ref, idx_ref, w_ref, rank_ref, counts_ref, carry_ref, hprev_ref):
    i = pl.program_id(0)

    @pl.when(i == 0)
    def _():
        carry_ref[...] = jnp.zeros_like(carry_ref)
        hprev_ref[...] = jnp.zeros_like(hprev_ref)

    gates = gates_ref[...].astype(jnp.float32)
    half = D_MODEL // 2

    logits = lax.dot_general(wrt_ref[...], hprev_ref[...], _NT, precision=lax.Precision.HIGHEST,
                             preferred_element_type=jnp.float32) + brt_ref[...]
    tm = logits.shape[1]
    sub = lax.broadcasted_iota(jnp.int32, logits.shape, 0)
    vals, idxs, hots = [], [], []
    l = logits

    def pick(l):
        m = jnp.max(l, axis=0, keepdims=True)
        idx = jnp.min(jnp.where(l == m, sub, N_EXPERTS), axis=0, keepdims=True)
        hot = sub == idx
        vals.append(m)
        idxs.append(idx)
        hots.append(hot)
        return jnp.where(hot, -jnp.inf, l)

    ya_lo = gates[:, :half] * _dot(oa_ref[...], woa_ref[:, :half])
    l = pick(l)
    ya_hi = gates[:, half:D_MODEL] * _dot(oa_ref[...], woa_ref[:, half:])
    l = pick(l)
    yb_lo = gates[:, D_MODEL:D_MODEL + half] * _dot(ob_ref[...], wob_ref[:, :half])
    l = pick(l)
    yb_hi = gates[:, D_MODEL + half:] * _dot(ob_ref[...], wob_ref[:, half:])
    l = pick(l)
    merged = jnp.concatenate([ya_lo + yb_lo, ya_hi + yb_hi], axis=1).astype(jnp.bfloat16)

    es = [jnp.exp(v - vals[0]) for v in vals]
    den = es[0] + es[1] + es[2] + es[3]
    w_ref[...] = _rows([e / den for e in es], jnp.float32)
    idx_ref[...] = _rows(idxs, jnp.int32)
    x1_lo = x_ref[:, :half] + _dot(merged, wout_ref[:, :half])

    chosen = (hots[0] | hots[1] | hots[2] | hots[3]).astype(jnp.float32)
    r_i = lax.broadcasted_iota(jnp.int32, (tm, tm), 0)
    c_i = lax.broadcasted_iota(jnp.int32, (tm, tm), 1)
    earlier = (r_i < c_i).astype(jnp.bfloat16)
    prefix = _dot(chosen.astype(jnp.bfloat16), earlier) + carry_ref[...]
    x1_hi = x_ref[:, half:] + _dot(merged, wout_ref[:, half:])
    ranks = [jnp.sum(jnp.where(h, prefix, 0.0), axis=0, keepdims=True) for h in hots]
    rank_ref[...] = _rows(ranks, jnp.int32)
    live = (i > 0).astype(jnp.float32)
    carry = carry_ref[...] + live * jnp.sum(chosen, axis=1, keepdims=True)
    carry_ref[...] = carry
    counts_ref[...] = carry.astype(jnp.int32)

    x1 = jnp.concatenate([x1_lo, x1_hi], axis=1)
    x1_ref[...] = x1
    h2 = _rms(x1, gffn_ref[...])
    h2_ref[...] = h2
    hprev_ref[...] = h2


def _merge(x2, oa, ob, gates, woa, wob, wout, gffn, wrt, brt):
    t = x2.shape[0]
    tm = TOKEN_TILE
    n = t // tm
    row = lambda w: pl.BlockSpec((tm, w), lambda i: (jnp.minimum(i, n - 1), 0))
    col = lambda: pl.BlockSpec((TOP_K, tm), lambda i: (0, jnp.maximum(i - 1, 0)))
    consts = (woa, wob, wout, gffn, wrt, brt)
    return pl.pallas_call(
        _merge_kernel,
        grid=(n + 1,),
        in_specs=[row(D_MODEL), row(D_MODEL), row(D_MODEL), row(2 * D_MODEL)]
        + [_const_spec(c.shape) for c in consts],
        out_specs=[row(D_MODEL), row(D_MODEL), col(), col(), col(),
                   _const_spec((N_EXPERTS, 1))],
        out_shape=[
            jax.ShapeDtypeStruct((t, D_MODEL), jnp.float32),
            jax.ShapeDtypeStruct((t, D_MODEL), jnp.float32),
            jax.ShapeDtypeStruct((TOP_K, t), jnp.int32),
            jax.ShapeDtypeStruct((TOP_K, t), jnp.float32),
            jax.ShapeDtypeStruct((TOP_K, t), jnp.int32),
            jax.ShapeDtypeStruct((N_EXPERTS, 1), jnp.int32),
        ],
        scratch_shapes=[pltpu.VMEM((N_EXPERTS, 1), jnp.float32),
                        pltpu.VMEM((tm, D_MODEL), jnp.float32)],
        compiler_params=pltpu.CompilerParams(
            dimension_semantics=("arbitrary",), vmem_limit_bytes=VMEM_LIMIT),
        name="merge",
    )(x2, oa, ob, gates, *consts)


def _row_copy_wait(src_rows_ref, dst_rows_ref, sem, n):
    for _ in range(n):
        pltpu.make_async_copy(src_rows_ref, dst_rows_ref, sem).wait()


def _dispatch_kernel(dest_ref, pad_lo_ref, pad_hi_ref, nu_ref, h2_ref, xs_ref, zero_ref, sem, zsem):
    i = pl.program_id(0)
    tm = h2_ref.shape[0]
    bm = zero_ref.shape[0]
    n_blocks = xs_ref.shape[0] // bm

    @pl.when(i == 0)
    def _():
        zero_ref[...] = jnp.zeros_like(zero_ref)

        def pad_rows(fn):
            def per_expert(e, c):
                return lax.fori_loop(pad_lo_ref[e], pad_hi_ref[e], fn, c)
            lax.fori_loop(0, N_EXPERTS, per_expert, 0)

        def row_copy(j):
            return pltpu.make_async_copy(zero_ref.at[pl.ds(0, 1)], xs_ref.at[pl.ds(j, 1)], zsem)

        def blk_copy(b):
            return pltpu.make_async_copy(zero_ref, xs_ref.at[pl.ds(pl.multiple_of(b * bm, bm), bm)], zsem)

        def start_row(j, c):
            row_copy(j).start()
            return c

        def wait_row(j, c):
            row_copy(j).wait()
            return c

        def start_blk(b, c):
            blk_copy(b).start()
            return c

        def wait_blk(b, c):
            blk_copy(b).wait()
            return c

        pad_rows(start_row)
        lax.fori_loop(nu_ref[0], n_blocks, start_blk, 0)
        pad_rows(wait_row)
        lax.fori_loop(nu_ref[0], n_blocks, wait_blk, 0)

    def body(r, c):
        base = (i * tm + r) * TOP_K
        for k in range(TOP_K):
            d = dest_ref[base + k]
            pltpu.make_async_copy(h2_ref.at[pl.ds(r, 1)], xs_ref.at[pl.ds(d, 1)], sem).start()
        return c

    lax.fori_loop(0, tm, body, 0, unroll=ROW_DMA_UNROLL)
    _row_copy_wait(h2_ref, xs_ref.at[pl.ds(0, tm)], sem, TOP_K)


def _dispatch(dest, pad_lo, pad_hi, n_used, h2, n_blocks):
    t = h2.shape[0]
    tm = DISPATCH_TILE
    bm = ROW_BLOCK
    return pl.pallas_call(
        _dispatch_kernel,
        grid_spec=pltpu.PrefetchScalarGridSpec(
            num_scalar_prefetch=4,
            grid=(t // tm,),
            in_specs=[pl.BlockSpec((tm, D_MODEL), lambda i, *_: (i, 0))],
            out_specs=pl.BlockSpec(memory_space=pl.ANY),
            scratch_shapes=[pltpu.VMEM((bm, D_MODEL), jnp.float32),
                            pltpu.SemaphoreType.DMA(()), pltpu.SemaphoreType.DMA(())],
        ),
        out_shape=jax.ShapeDtypeStruct((n_blocks * bm, D_MODEL), jnp.float32),
        compiler_params=pltpu.CompilerParams(
            dimension_semantics=("arbitrary",), vmem_limit_bytes=VMEM_LIMIT),
        name="dispatch",
    )(dest, pad_lo, pad_hi, n_used, h2)


def _experts_kernel(be_ref, nu_ref, xs_ref, wg_ref, bg_ref, wu_ref, bu_ref, wd_ref, bd_ref, ys_ref,
                    wbf_ref):
    b = pl.program_id(0)

    @pl.when(jnp.logical_or(b == 0, be_ref[b] != be_ref[jnp.maximum(b - 1, 0)]))
    def _():
        wbf_ref[0] = wg_ref[0].astype(jnp.bfloat16)
        wbf_ref[1] = wu_ref[0].astype(jnp.bfloat16)
        wbf_ref[2] = wd_ref[0].astype(jnp.bfloat16)

    @pl.when(b >= nu_ref[0])
    def _():
        ys_ref[...] = jnp.zeros_like(ys_ref)

    @pl.when(b < nu_ref[0])
    def _():
        xb = xs_ref[...].astype(jnp.bfloat16)
        gate = jnp.minimum(_dot(xb, wbf_ref[0]) + bg_ref[0], SWIGLU_LIMIT)
        up = jnp.clip(_dot(xb, wbf_ref[1]) + bu_ref[0], -SWIGLU_LIMIT, SWIGLU_LIMIT)
        act = (up + 1.0) * (gate * jax.nn.sigmoid(SWIGLU_ALPHA * gate))
        ys_ref[...] = _dot(act.astype(jnp.bfloat16), wbf_ref[2]) + bd_ref[0]


def _experts(block_e, n_used, xs, wg, bg, wu, bu, wd, bd, n_blocks):
    bm = ROW_BLOCK
    rows = lambda b, be, nu: (jnp.maximum(jnp.minimum(b, nu[0] - 1), 0), 0)
    wspec = lambda: pl.BlockSpec((1, D_MODEL, D_FF), lambda b, be, nu: (be[b], 0, 0))
    bspec = lambda: pl.BlockSpec((1, 1, D_FF), lambda b, be, nu: (be[b], 0, 0))
    return pl.pallas_call(
        _experts_kernel,
        grid_spec=pltpu.PrefetchScalarGridSpec(
            num_scalar_prefetch=2,
            grid=(n_blocks,),
            in_specs=[pl.BlockSpec((bm, D_MODEL), rows), wspec(), bspec(), wspec(), bspec(),
                      wspec(), bspec()],
            out_specs=pl.BlockSpec((bm, D_MODEL), lambda b, be, nu: (b, 0)),
            scratch_shapes=[pltpu.VMEM((3, D_MODEL, D_FF), jnp.bfloat16)],
        ),
        out_shape=jax.ShapeDtypeStruct((n_blocks * bm, D_MODEL), jnp.float32),
        compiler_params=pltpu.CompilerParams(
            dimension_semantics=("arbitrary",), vmem_limit_bytes=VMEM_LIMIT),
        name="experts",
    )(block_e, n_used, xs, wg, bg, wu, bu, wd, bd)


def _combine_kernel(dest_ref, ys_ref, x1_ref, w_ref, gfin_ref, o_ref, buf_ref, sem):
    i = pl.program_id(0)
    tm = x1_ref.shape[0]

    def gather(tile):
        slot = lax.rem(tile, 2)

        def body(r, c):
            base = (tile * tm + r) * TOP_K
            for k in range(TOP_K):
                d = dest_ref[base + k]
                pltpu.make_async_copy(ys_ref.at[pl.ds(d, 1)], buf_ref.at[slot, k, pl.ds(r, 1)],
                                      sem.at[slot]).start()
            return c

        lax.fori_loop(0, tm, body, 0, unroll=ROW_DMA_UNROLL)

    @pl.when(i == 0)
    def _():
        gather(i)

    @pl.when(i + 1 < pl.num_programs(0))
    def _():
        gather(i + 1)

    slot = lax.rem(i, 2)
    _row_copy_wait(ys_ref.at[pl.ds(0, tm)], buf_ref.at[slot, 0], sem.at[slot], TOP_K)
    w = w_ref[...]
    y = x1_ref[...]
    for k in range(TOP_K):
        y = y + buf_ref[slot, k] * w[:, k:k + 1]
    o_ref[...] = _rms(y, gfin_ref[...])


def _combine(dest, ys, x1, top_w, gfin):
    t = x1.shape[0]
    tm = TOKEN_TILE
    return pl.pallas_call(
        _combine_kernel,
        grid_spec=pltpu.PrefetchScalarGridSpec(
            num_scalar_prefetch=1,
            grid=(t // tm,),
            in_specs=[pl.BlockSpec(memory_space=pl.ANY),
                      pl.BlockSpec((tm, D_MODEL), lambda i, d: (i, 0)),
                      pl.BlockSpec((tm, TOP_K), lambda i, d: (i, 0)),
                      pl.BlockSpec((1, D_MODEL), lambda i, d: (0, 0))],
            out_specs=pl.BlockSpec((tm, D_MODEL), lambda i, d: (i, 0)),
            scratch_shapes=[pltpu.VMEM((2, TOP_K, tm, D_MODEL), jnp.float32),
                            pltpu.SemaphoreType.DMA((2,))],
        ),
        out_shape=jax.ShapeDtypeStruct((t, D_MODEL), jnp.float32),
        compiler_params=pltpu.CompilerParams(
            dimension_semantics=("arbitrary",), vmem_limit_bytes=VMEM_LIMIT),
        name="combine",
    )(dest, ys, x1, top_w, gfin)


def _rope_lane_table(rot, group):
    half = rot // 2
    inv = ROPE_THETA ** (-jnp.arange(0, rot, 2, dtype=jnp.float32) / rot)
    d = jnp.arange(LANES) % group
    first = jnp.arange(LANES) < (LANES if group < LANES else rot)
    in_lo = (d < half) & first
    in_hi = (d >= half) & (d < rot) & first
    freq = jnp.where(in_lo | in_hi, inv[d % half], 0.0)
    return jnp.stack([freq, in_lo.astype(jnp.float32), in_hi.astype(jnp.float32)]).astype(jnp.float32)


def kernel(x, positions, g_mix, w_in, lam_q1, lam_k1, lam_q2, lam_k2, g_subln, g_q, g_kv, w_uq, w_ukv, w_o_diff, w_o_mla, b_gates, w_out, g_ffn, w_router, b_router, w_gate, b_gate, w_up, b_up, w_down, b_down, g_final):
    batch, seq, d = x.shape
    t = batch * seq
    bf = jnp.bfloat16
    l = 0
    x2 = x.reshape(t, d)
    posf = positions.astype(jnp.float32).reshape(t, 1)

    w = w_in[l]
    win = jnp.concatenate([w[:, :IN_KR_END], jnp.zeros((d, LANES - MLA_ROPE), w.dtype),
                           w[:, IN_KR_END:]], axis=1).astype(bf)
    wuq = jnp.pad(w_uq[l].reshape(MLA_Q_LORA, MLA_HEADS, MLA_NOPE + MLA_ROPE),
                  ((0, 0), (0, 0), (0, MLA_QK_PAD - MLA_NOPE - MLA_ROPE))
                  ).reshape(MLA_Q_LORA, MLA_HEADS * MLA_QK_PAD).astype(bf)
    wukv = w_ukv[l].reshape(MLA_KV_LORA, MLA_HEADS, MLA_NOPE + MLA_V)
    wkn = wukv[:, :, :MLA_NOPE].reshape(MLA_KV_LORA, MLA_HEADS * MLA_NOPE).astype(bf)
    wv = wukv[:, :, MLA_NOPE:].reshape(MLA_KV_LORA, MLA_HEADS * MLA_V).astype(bf)
    fd = _rope_lane_table(DA_ROT, DA_HEAD_DIM)
    fm = _rope_lane_table(MLA_ROPE, LANES)

    qa, ka, va, gates, qm, km, vm = _proj(
        x2, posf, g_mix[l][None], win, b_gates[l][None],
        g_q[l][None], g_kv[l][None], wuq, wkn, wv, fd, fm)

    lamv = jnp.stack([lam_q1[l], lam_k1[l], lam_q2[l], lam_k2[l]]).astype(jnp.float32)
    oa = _dattn(qa, ka, va, lamv, g_subln[l][None], batch, seq)
    ob = _mattn(qm, km, vm, batch, seq)

    x1, h2, e_idx, top_w, rank, counts = _merge(
        x2, oa, ob, gates, w_o_diff[l].astype(bf), w_o_mla[l].astype(bf), w_out[l].astype(bf),
        g_ffn[l][None], w_router[l].T, b_router[l][:, None])

    bm = ROW_BLOCK
    n_blocks = (t * TOP_K) // bm + N_EXPERTS
    counts = counts[:, 0]
    padded = (counts + bm - 1) // bm * bm
    padded_end = jnp.cumsum(padded)
    padded_start = padded_end - padded
    n_used = (padded_end[-1] // bm).astype(jnp.int32)
    blk = jnp.minimum(jnp.arange(n_blocks, dtype=jnp.int32), n_used - 1)
    block_e = jnp.minimum(jnp.sum(padded_end[None, :] <= (blk * bm)[:, None], axis=1),
                          N_EXPERTS - 1).astype(jnp.int32)
    hot = e_idx[:, :, None] == jnp.arange(N_EXPERTS, dtype=jnp.int32)
    dest = (jnp.sum(jnp.where(hot, padded_start, 0), axis=-1) + rank).T.reshape(-1).astype(jnp.int32)

    n_used = n_used.reshape(1)
    xs = _dispatch(dest, (padded_start + counts).astype(jnp.int32), padded_end.astype(jnp.int32),
                   n_used, h2, n_blocks)
    ys = _experts(block_e, n_used, xs, w_gate[l], b_gate[l][:, None, :], w_up[l],
                  b_up[l][:, None, :], w_down[l], b_down[l][:, None, :], n_blocks)
    out = _combine(dest, ys, x1, top_w.T, g_final[None])
    return out.reshape(batch, seq, d)
```

```python
import math

import jax
import jax.numpy as jnp
from jax import lax
from jax.experimental import pallas as pl
from jax.experimental.pallas import tpu as pltpu

D_MODEL = 1024
ROPE_THETA = 500000.0
NORM_EPS = 1e-6
DA_HEADS = 8
DA_HEAD_DIM = 64
DA_ROT = DA_HEAD_DIM // 4
MLA_HEADS = 8
MLA_Q_LORA = 768
MLA_KV_LORA = 512
MLA_NOPE = 128
MLA_ROPE = 64
MLA_V = 128
N_EXPERTS = 32
TOP_K = 4
D_FF = 1024
SWIGLU_ALPHA = 1.702
SWIGLU_LIMIT = 7.0
LAM_INIT = 0.8 - 0.6 * math.exp(-0.3 * 0)

LANES = 128
MLA_QK_PAD = 256
TOKEN_TILE = 256
DISPATCH_TILE = 512
Q_TILE = 256
KEY_CHUNK = 256
ROW_BLOCK = 256
ROW_DMA_UNROLL = 8
VMEM_LIMIT = 56 * 1024 * 1024

LOG2_E = math.log2(math.e)

IN_W = {"qa": DA_HEADS * 2 * DA_HEAD_DIM, "ka": DA_HEADS * 2 * DA_HEAD_DIM,
        "va": DA_HEADS * 2 * DA_HEAD_DIM, "cq": MLA_Q_LORA,
        "ckr": MLA_KV_LORA + LANES,
        "g": 2 * D_MODEL}
IN_OFF = dict(zip(IN_W, [sum(list(IN_W.values())[:n]) for n in range(len(IN_W))]))
IN_KR_END = IN_OFF["ckr"] + MLA_KV_LORA + MLA_ROPE

_NT = (((1,), (1,)), ((), ()))


def _rms(x, g):
    return x * lax.rsqrt(jnp.mean(x * x, axis=-1, keepdims=True) + NORM_EPS) * g


def _dot(a, b):
    return jnp.dot(a, b, preferred_element_type=jnp.float32)


def _rope_tables(pos, freq, m_lo, m_hi):
    ang = pos * freq
    c = jnp.cos(ang)
    s = jnp.sin(ang)
    return c, -s * m_lo, s * m_hi


def _rope_block(xb, tables, half):
    c, s_lo, s_hi = tables
    return (xb * c + pltpu.roll(xb, LANES - half, 1) * s_lo
            + pltpu.roll(xb, half, 1) * s_hi)


def _proj_kernel(x_ref, pos_ref, gmix_ref, win_ref, bg_ref, gq_ref, gkv_ref, wuq_ref, wkn_ref, wv_ref,
                 fd_ref, fm_ref,
                 qa_ref, ka_ref, va_ref, gates_ref, qm_ref, km_ref, vm_ref):
    hb = _rms(x_ref[...], gmix_ref[...]).astype(jnp.bfloat16)
    pos = pos_ref[...]
    td = _rope_tables(pos, fd_ref[0:1, :], fd_ref[1:2, :], fd_ref[2:3, :])
    tmla = _rope_tables(pos, fm_ref[0:1, :], fm_ref[1:2, :], fm_ref[2:3, :])

    da_scale = DA_HEAD_DIM ** -0.5 * LOG2_E
    w_cols = lambda name: win_ref[:, IN_OFF[name]:IN_OFF[name] + IN_W[name]]
    mla_scale = (MLA_NOPE + MLA_ROPE) ** -0.5 * LOG2_E

    cq = _rms(_dot(hb, w_cols("cq")), gq_ref[...]).astype(jnp.bfloat16)
    ck = _dot(hb, w_cols("ckr"))
    kr = _rope_block(ck[:, MLA_KV_LORA:], tmla, MLA_ROPE // 2).astype(jnp.bfloat16)
    ckv = _rms(ck[:, :MLA_KV_LORA], gkv_ref[...]).astype(jnp.bfloat16)

    zq = _dot(hb, w_cols("qa"))
    for j in range(DA_HEADS):
        sl = slice(j * LANES, (j + 1) * LANES)
        qa_ref[:, sl] = (_rope_block(zq[:, sl], td, DA_ROT // 2) * da_scale).astype(jnp.bfloat16)

    qm = _dot(cq, wuq_ref[...])
    for h in range(MLA_HEADS):
        lo = slice(h * MLA_QK_PAD, h * MLA_QK_PAD + LANES)
        hi = slice(h * MLA_QK_PAD + LANES, (h + 1) * MLA_QK_PAD)
        qm_ref[:, lo] = (qm[:, lo] * mla_scale).astype(jnp.bfloat16)
        qm_ref[:, hi] = (_rope_block(qm[:, hi], tmla, MLA_ROPE // 2) * mla_scale).astype(jnp.bfloat16)

    zk = _dot(hb, w_cols("ka"))
    for j in range(DA_HEADS):
        sl = slice(j * LANES, (j + 1) * LANES)
        ka_ref[:, sl] = _rope_block(zk[:, sl], td, DA_ROT // 2).astype(jnp.bfloat16)

    kn = _dot(ckv, wkn_ref[...])
    for h in range(MLA_HEADS):
        km_ref[:, h * MLA_QK_PAD:h * MLA_QK_PAD + LANES] = kn[:, h * LANES:(h + 1) * LANES].astype(jnp.bfloat16)
        km_ref[:, h * MLA_QK_PAD + LANES:(h + 1) * MLA_QK_PAD] = kr
    vm_ref[...] = _dot(ckv, wv_ref[...]).astype(jnp.bfloat16)
    gates_ref[...] = jax.nn.sigmoid(_dot(hb, w_cols("g")) + bg_ref[...]).astype(jnp.bfloat16)
    va_ref[...] = _dot(hb, w_cols("va")).astype(jnp.bfloat16)


def _const_spec(shape):
    return pl.BlockSpec(shape, lambda i: (0,) * len(shape))


def _proj(x2, posf, gmix, win, bg, gq, gkv, wuq, wkn, wv, fd, fm):
    t = x2.shape[0]
    tm = TOKEN_TILE
    bf = jnp.bfloat16
    row = lambda w: pl.BlockSpec((tm, w), lambda i: (i, 0))
    consts = (gmix, win, bg, gq, gkv, wuq, wkn, wv, fd, fm)
    out_w = (D_MODEL, D_MODEL, D_MODEL, 2 * D_MODEL, MLA_HEADS * MLA_QK_PAD,
             MLA_HEADS * MLA_QK_PAD, MLA_HEADS * MLA_V)
    return pl.pallas_call(
        _proj_kernel,
        grid=(t // tm,),
        in_specs=[row(D_MODEL), row(1)] + [_const_spec(c.shape) for c in consts],
        out_specs=[row(w) for w in out_w],
        out_shape=[jax.ShapeDtypeStruct((t, w), bf) for w in out_w],
        compiler_params=pltpu.CompilerParams(
            dimension_semantics=("parallel",), vmem_limit_bytes=VMEM_LIMIT),
        name="proj",
    )(x2, posf, *consts)


def _pipelined_tiles(n_tiles, n_chunks, score_chunk, max_merge, max_store, max_load,
                     value_chunk, write_out):
    def fused(t_next, slot_next, t_cur, slot_cur):
        m_cur = None if t_cur is None else max_load(slot_cur)
        m_next, acc = None, None
        for c in range(n_chunks):
            if t_next is not None:
                m_next = max_merge(m_next, score_chunk(t_next, slot_next, c))
            if t_cur is not None:
                acc = value_chunk(slot_cur, c, m_cur, acc)
        if t_next is not None:
            max_store(slot_next, m_next)
        if t_cur is not None:
            write_out(t_cur, acc)

    fused(0, 0, None, None)

    def pair(j, carry):
        t = 2 * j
        fused(t + 1, 1, t, 0)
        fused(t + 2, 0, t + 1, 1)
        return carry

    lax.fori_loop(0, n_tiles // 2 - 1, pair, 0)
    fused(n_tiles - 1, 1, n_tiles - 2, 0)
    fused(None, None, n_tiles - 1, 1)


def _q_rows(t):
    return pl.ds(pl.multiple_of(t * Q_TILE, Q_TILE), Q_TILE)


def _key_cols(c):
    return slice(c * KEY_CHUNK, (c + 1) * KEY_CHUNK)


def _fold_lanes(x, op):
    out = x[:, :LANES]
    for j in range(1, x.shape[1] // LANES):
        out = op(out, x[:, j * LANES:(j + 1) * LANES])
    return out


def _fill_values_ext(vx_ref, v_ref):
    width = v_ref.shape[1]
    vx_ref[:, :width] = v_ref[...]
    lane = lax.broadcasted_iota(jnp.int32, (v_ref.shape[0], vx_ref.shape[1] - width), 1)
    vx_ref[:, width:] = jnp.where(lane == 0, 1.0, 0.0).astype(vx_ref.dtype)


def _prob_values(s_ref, vx_ref, c, m, acc):
    p = jnp.exp2(s_ref[:, _key_cols(c)] - m).astype(jnp.bfloat16)
    part = _dot(p, vx_ref[_key_cols(c), :])
    return part if acc is None else acc + part


def _dattn_kernel(q_ref, k_ref, v_ref, lam_ref, gsub_ref, o_ref,
                  s0_ref, s1_ref, m0_ref, m1_ref, vx_ref, qm_ref):
    s_refs, m_refs = (s0_ref, s1_ref), (m0_ref, m1_ref)
    _fill_values_ext(vx_ref, v_ref)
    width = v_ref.shape[1]
    lamv = lam_ref[...]
    lam = (jnp.exp(jnp.sum(lamv[0:1] * lamv[1:2], axis=-1, keepdims=True))
           - jnp.exp(jnp.sum(lamv[2:3] * lamv[3:4], axis=-1, keepdims=True)) + LAM_INIT)
    q_all = q_ref[...]
    lane = lax.broadcasted_iota(jnp.int32, q_all.shape, 1)
    qm_ref[0] = jnp.where(lane < DA_HEAD_DIM, q_all, jnp.zeros_like(q_all))
    qm_ref[1] = jnp.where(lane >= DA_HEAD_DIM, q_all, jnp.zeros_like(q_all))

    def score_chunk(t, slot, c):
        k = k_ref[_key_cols(c), :]
        sc0 = lax.dot_general(qm_ref[0, _q_rows(t), :], k, _NT, preferred_element_type=jnp.float32)
        sc1 = lax.dot_general(qm_ref[1, _q_rows(t), :], k, _NT, preferred_element_type=jnp.float32)
        s_refs[slot][0, :, _key_cols(c)] = sc0
        s_refs[slot][1, :, _key_cols(c)] = sc1
        return _fold_lanes(sc0, jnp.maximum), _fold_lanes(sc1, jnp.maximum)

    def max_merge(m, mc):
        return mc if m is None else (jnp.maximum(m[0], mc[0]), jnp.maximum(m[1], mc[1]))

    def max_store(slot, m):
        m_refs[slot][0] = jnp.max(m[0], axis=-1, keepdims=True)
        m_refs[slot][1] = jnp.max(m[1], axis=-1, keepdims=True)

    def max_load(slot):
        return m_refs[slot][0], m_refs[slot][1]

    def value_chunk(slot, c, m, acc):
        acc0, acc1 = (None, None) if acc is None else acc
        return (_prob_values(s_refs[slot].at[0], vx_ref, c, m[0], acc0),
                _prob_values(s_refs[slot].at[1], vx_ref, c, m[1], acc1))

    def write_out(t, acc):
        ox0, ox1 = acc
        o = (ox0[:, :width] * (1.0 / ox0[:, width:width + 1])
             - ox1[:, :width] * (lam / ox1[:, width:width + 1]))
        o_ref[_q_rows(t), :] = (_rms(o, gsub_ref[...]) * (1.0 - LAM_INIT)).astype(o_ref.dtype)

    _pipelined_tiles(q_ref.shape[0] // Q_TILE, k_ref.shape[0] // KEY_CHUNK,
                     score_chunk, max_merge, max_store, max_load, value_chunk, write_out)


def _dattn(qa, ka, va, lamv, gsub, batch, seq):
    head = lambda b, h: (b, h)
    return pl.pallas_call(
        _dattn_kernel,
        grid=(batch, DA_HEADS),
        in_specs=[
            pl.BlockSpec((seq, LANES), head),
            pl.BlockSpec((seq, LANES), head),
            pl.BlockSpec((seq, LANES), head),
            pl.BlockSpec(lamv.shape, lambda b, h: (0, 0)),
            pl.BlockSpec(gsub.shape, lambda b, h: (0, 0)),
        ],
        out_specs=pl.BlockSpec((seq, LANES), head),
        out_shape=jax.ShapeDtypeStruct(qa.shape, jnp.bfloat16),
        scratch_shapes=[pltpu.VMEM((2, Q_TILE, seq), jnp.float32)] * 2
        + [pltpu.VMEM((2, Q_TILE, 1), jnp.float32)] * 2
        + [pltpu.VMEM((seq, 2 * LANES), jnp.bfloat16), pltpu.VMEM((2, seq, LANES), jnp.bfloat16)],
        compiler_params=pltpu.CompilerParams(
            dimension_semantics=("parallel", "parallel"), vmem_limit_bytes=VMEM_LIMIT),
        name="dattn",
    )(qa, ka, va, lamv, gsub)


def _mattn_kernel(q_ref, k_ref, v_ref, o_ref, s0_ref, s1_ref, m0_ref, m1_ref, vx_ref):
    s_refs, m_refs = (s0_ref, s1_ref), (m0_ref, m1_ref)
    _fill_values_ext(vx_ref, v_ref)
    width = v_ref.shape[1]

    def score_chunk(t, slot, c):
        sc = lax.dot_general(q_ref[_q_rows(t), :], k_ref[_key_cols(c), :], _NT,
                             preferred_element_type=jnp.float32)
        s_refs[slot][:, _key_cols(c)] = sc
        return _fold_lanes(sc, jnp.maximum)

    def max_merge(m, mc):
        return mc if m is None else jnp.maximum(m, mc)

    def max_store(slot, m):
        m_refs[slot][...] = jnp.max(m, axis=-1, keepdims=True)

    def max_load(slot):
        return m_refs[slot][...]

    def value_chunk(slot, c, m, acc):
        return _prob_values(s_refs[slot], vx_ref, c, m, acc)

    def write_out(t, ox):
        o_ref[_q_rows(t), :] = (ox[:, :width] * (1.0 / ox[:, width:width + 1])).astype(o_ref.dtype)

    _pipelined_tiles(q_ref.shape[0] // Q_TILE, k_ref.shape[0] // KEY_CHUNK,
                     score_chunk, max_merge, max_store, max_load, value_chunk, write_out)


def _mattn(qm, km, vm, batch, seq):
    head = lambda b, h: (b, h)
    return pl.pallas_call(
        _mattn_kernel,
        grid=(batch, MLA_HEADS),
        in_specs=[
            pl.BlockSpec((seq, MLA_QK_PAD), head),
            pl.BlockSpec((seq, MLA_QK_PAD), head),
            pl.BlockSpec((seq, MLA_V), head),
        ],
        out_specs=pl.BlockSpec((seq, MLA_V), head),
        out_shape=jax.ShapeDtypeStruct(vm.shape, jnp.bfloat16),
        scratch_shapes=[pltpu.VMEM((Q_TILE, seq), jnp.float32)] * 2
        + [pltpu.VMEM((Q_TILE, 1), jnp.float32)] * 2
        + [pltpu.VMEM((seq, 2 * MLA_V), jnp.bfloat16)],
        compiler_params=pltpu.CompilerParams(
            dimension_semantics=("parallel", "parallel"), vmem_limit_bytes=VMEM_LIMIT),
        name="mattn",
    )(qm, km, vm)


def _rows(parts, dtype):
    sub = lax.broadcasted_iota(jnp.int32, (len(parts), parts[0].shape[1]), 0)
    out = jnp.zeros(sub.shape, dtype)
    for k, p in enumerate(parts):
        out = jnp.where(sub == k, p.astype(dtype), out)
    return out


def _merge_kernel(x_ref, oa_ref, ob_ref, gates_ref, woa_ref, wob_ref, wout_ref, gffn_ref,
                  wrt_ref, brt_ref,
                  x1_ref, h2_ref, idx_ref, w_ref, rank_ref, counts_ref, carry_ref, hprev_ref):
    i = pl.program_id(0)

    @pl.when(i == 0)
    def _():
        carry_ref[...] = jnp.zeros_like(carry_ref)
        hprev_ref[...] = jnp.zeros_like(hprev_ref)

    gates = gates_ref[...].astype(jnp.float32)
    half = D_MODEL // 2

    logits = lax.dot_general(wrt_ref[...], hprev_ref[...], _NT, precision=lax.Precision.HIGHEST,
                             preferred_element_type=jnp.float32) + brt_ref[...]
    tm = logits.shape[1]
    sub = lax.broadcasted_iota(jnp.int32, logits.shape, 0)
    vals, idxs, hots = [], [], []
    l = logits

    def pick(l):
        m = jnp.max(l, axis=0, keepdims=True)
        idx = jnp.min(jnp.where(l == m, sub, N_EXPERTS), axis=0, keepdims=True)
        hot = sub == idx
        vals.append(m)
        idxs.append(idx)
        hots.append(hot)
        return jnp.where(hot, -jnp.inf, l)

    ya_lo = gates[:, :half] * _dot(oa_ref[...], woa_ref[:, :half])
    l = pick(l)
    ya_hi = gates[:, half:D_MODEL] * _dot(oa_ref[...], woa_ref[:, half:])
    l = pick(l)
    yb_lo = gates[:, D_MODEL:D_MODEL + half] * _dot(ob_ref[...], wob_ref[:, :half])
    l = pick(l)
    yb_hi = gates[:, D_MODEL + half:] * _dot(ob_ref[...], wob_ref[:, half:])
    l = pick(l)
    merged = jnp.concatenate([ya_lo + yb_lo, ya_hi + yb_hi], axis=1).astype(jnp.bfloat16)

    es = [jnp.exp(v - vals[0]) for v in vals]
    den = es[0] + es[1] + es[2] + es[3]
    w_ref[...] = _rows([e / den for e in es], jnp.float32)
    idx_ref[...] = _rows(idxs, jnp.int32)
    x1_lo = x_ref[:, :half] + _dot(merged, wout_ref[:, :half])

    chosen = (hots[0] | hots[1] | hots[2] | hots[3]).astype(jnp.float32)
    r_i = lax.broadcasted_iota(jnp.int32, (tm, tm), 0)
    c_i = lax.broadcasted_iota(jnp.int32, (tm, tm), 1)
    earlier = (r_i < c_i).astype(jnp.bfloat16)
    prefix = _dot(chosen.astype(jnp.bfloat16), earlier) + carry_ref[...]
    x1_hi = x_ref[:, half:] + _dot(merged, wout_ref[:, half:])
    ranks = [jnp.sum(jnp.where(h, prefix, 0.0), axis=0, keepdims=True) for h in hots]
    rank_ref[...] = _rows(ranks, jnp.int32)
    live = (i > 0).astype(jnp.float32)
    carry = carry_ref[...] + live * jnp.sum(chosen, axis=1, keepdims=True)
    carry_ref[...] = carry
    counts_ref[...] = carry.astype(jnp.int32)

    x1 = jnp.concatenate([x1_lo, x1_hi], axis=1)
    x1_ref[...] = x1
    h2 = _rms(x1, gffn_ref[...])
    h2_ref[...] = h2
    hprev_ref[...] = h2


def _merge(x2, oa, ob, gates, woa, wob, wout, gffn, wrt, brt):
    t = x2.shape[0]
    tm = TOKEN_TILE
    n = t // tm
    row = lambda w: pl.BlockSpec((tm, w), lambda i: (jnp.minimum(i, n - 1), 0))
    col = lambda: pl.BlockSpec((TOP_K, tm), lambda i: (0, jnp.maximum(i - 1, 0)))
    consts = (woa, wob, wout, gffn, wrt, brt)
    return pl.pallas_call(
        _merge_kernel,
        grid=(n + 1,),
        in_specs=[row(D_MODEL), row(D_MODEL), row(D_MODEL), row(2 * D_MODEL)]
        + [_const_spec(c.shape) for c in consts],
        out_specs=[row(D_MODEL), row(D_MODEL), col(), col(), col(),
                   _const_spec((N_EXPERTS, 1))],
        out_shape=[
            jax.ShapeDtypeStruct((t, D_MODEL), jnp.float32),
            jax.ShapeDtypeStruct((t, D_MODEL), jnp.float32),
            jax.ShapeDtypeStruct((TOP_K, t), jnp.int32),
            jax.ShapeDtypeStruct((TOP_K, t), jnp.float32),
            jax.ShapeDtypeStruct((TOP_K, t), jnp.int32),
            jax.ShapeDtypeStruct((N_EXPERTS, 1), jnp.int32),
        ],
        scratch_shapes=[pltpu.VMEM((N_EXPERTS, 1), jnp.float32),
                        pltpu.VMEM((tm, D_MODEL), jnp.float32)],
        compiler_params=pltpu.CompilerParams(
            dimension_semantics=("arbitrary",), vmem_limit_bytes=VMEM_LIMIT),
        name="merge",
    )(x2, oa, ob, gates, *consts)


def _row_copy_wait(src_rows_ref, dst_rows_ref, sem, n):
    for _ in range(n):
        pltpu.make_async_copy(src_rows_ref, dst_rows_ref, sem).wait()


def _dispatch_kernel(dest_ref, pad_lo_ref, pad_hi_ref, nu_ref, h2_ref, xs_ref, zero_ref, sem, zsem):
    i = pl.program_id(0)
    tm = h2_ref.shape[0]
    bm = zero_ref.shape[0]
    n_blocks = xs_ref.shape[0] // bm

    @pl.when(i == 0)
    def _():
        zero_ref[...] = jnp.zeros_like(zero_ref)

        def pad_rows(fn):
            def per_expert(e, c):
                return lax.fori_loop(pad_lo_ref[e], pad_hi_ref[e], fn, c)
            lax.fori_loop(0, N_EXPERTS, per_expert, 0)

        def row_copy(j):
            return pltpu.make_async_copy(zero_ref.at[pl.ds(0, 1)], xs_ref.at[pl.ds(j, 1)], zsem)

        def blk_copy(b):
            return pltpu.make_async_copy(zero_ref, xs_ref.at[pl.ds(pl.multiple_of(b * bm, bm), bm)], zsem)

        def start_row(j, c):
            row_copy(j).start()
            return c

        def wait_row(j, c):
            row_copy(j).wait()
            return c

        def start_blk(b, c):
            blk_copy(b).start()
            return c

        def wait_blk(b, c):
            blk_copy(b).wait()
            return c

        pad_rows(start_row)
        lax.fori_loop(nu_ref[0], n_blocks, start_blk, 0)
        pad_rows(wait_row)
        lax.fori_loop(nu_ref[0], n_blocks, wait_blk, 0)

    def body(r, c):
        base = (i * tm + r) * TOP_K
        for k in range(TOP_K):
            d = dest_ref[base + k]
            pltpu.make_async_copy(h2_ref.at[pl.ds(r, 1)], xs_ref.at[pl.ds(d, 1)], sem).start()
        return c

    lax.fori_loop(0, tm, body, 0, unroll=ROW_DMA_UNROLL)
    _row_copy_wait(h2_ref, xs_ref.at[pl.ds(0, tm)], sem, TOP_K)


def _dispatch(dest, pad_lo, pad_hi, n_used, h2, n_blocks):
    t = h2.shape[0]
    tm = DISPATCH_TILE
    bm = ROW_BLOCK
    return pl.pallas_call(
        _dispatch_kernel,
        grid_spec=pltpu.PrefetchScalarGridSpec(
            num_scalar_prefetch=4,
            grid=(t // tm,),
            in_specs=[pl.BlockSpec((tm, D_MODEL), lambda i, *_: (i, 0))],
            out_specs=pl.BlockSpec(memory_space=pl.ANY),
            scratch_shapes=[pltpu.VMEM((bm, D_MODEL), jnp.float32),
                            pltpu.SemaphoreType.DMA(()), pltpu.SemaphoreType.DMA(())],
        ),
        out_shape=jax.ShapeDtypeStruct((n_blocks * bm, D_MODEL), jnp.float32),
        compiler_params=pltpu.CompilerParams(
            dimension_semantics=("arbitrary",), vmem_limit_bytes=VMEM_LIMIT),
        name="dispatch",
    )(dest, pad_lo, pad_hi, n_used, h2)


def _experts_kernel(be_ref, nu_ref, nxt_ref, grp_ref, xs_ref, wg_ref, bg_ref, wu_ref, bu_ref, wd_ref,
                    bd_ref, ys_ref, stage_ref, wbf_ref, sem):
    b = pl.program_id(0)
    e = be_ref[b]
    slot = lax.rem(grp_ref[b], 2)
    w_hbm = (wg_ref, wu_ref, wd_ref)

    def weight_copies(expert, slot):
        return [pltpu.make_async_copy(w.at[expert], stage_ref.at[slot, j], sem.at[slot, j])
                for j, w in enumerate(w_hbm)]

    @pl.when(b == 0)
    def _():
        for cp in weight_copies(e, slot):
            cp.start()

    @pl.when(jnp.logical_or(b == 0, e != be_ref[jnp.maximum(b - 1, 0)]))
    def _():
        for j, cp in enumerate(weight_copies(e, slot)):
            cp.wait()
            wbf_ref[j] = stage_ref[slot, j].astype(jnp.bfloat16)

        @pl.when(nxt_ref[b] >= 0)
        def _():
            for cp in weight_copies(nxt_ref[b], 1 - slot):
                cp.start()

    @pl.when(b >= nu_ref[0])
    def _():
        ys_ref[...] = jnp.zeros_like(ys_ref)

    @pl.when(b < nu_ref[0])
    def _():
        xb = xs_ref[...].astype(jnp.bfloat16)
        gate = jnp.minimum(_dot(xb, wbf_ref[0]) + bg_ref[0], SWIGLU_LIMIT)
        up = jnp.clip(_dot(xb, wbf_ref[1]) + bu_ref[0], -SWIGLU_LIMIT, SWIGLU_LIMIT)
        act = (up + 1.0) * (gate * jax.nn.sigmoid(SWIGLU_ALPHA * gate))
        ys_ref[...] = _dot(act.astype(jnp.bfloat16), wbf_ref[2]) + bd_ref[0]


def _experts(block_e, n_used, next_e, group, xs, wg, bg, wu, bu, wd, bd, n_blocks):
    bm = ROW_BLOCK
    rows = lambda b, be, nu, *_: (jnp.maximum(jnp.minimum(b, nu[0] - 1), 0), 0)
    hbm = lambda: pl.BlockSpec(memory_space=pl.ANY)
    bspec = lambda: pl.BlockSpec((1, 1, D_FF), lambda b, be, *_: (be[b], 0, 0))
    return pl.pallas_call(
        _experts_kernel,
        grid_spec=pltpu.PrefetchScalarGridSpec(
            num_scalar_prefetch=4,
            grid=(n_blocks,),
            in_specs=[pl.BlockSpec((bm, D_MODEL), rows), hbm(), bspec(), hbm(), bspec(),
                      hbm(), bspec()],
            out_specs=pl.BlockSpec((bm, D_MODEL), lambda b, *_: (b, 0)),
            scratch_shapes=[pltpu.VMEM((2, 3, D_MODEL, D_FF), jnp.float32),
                            pltpu.VMEM((3, D_MODEL, D_FF), jnp.bfloat16),
                            pltpu.SemaphoreType.DMA((2, 3))],
        ),
        out_shape=jax.ShapeDtypeStruct((n_blocks * bm, D_MODEL), jnp.float32),
        compiler_params=pltpu.CompilerParams(
            dimension_semantics=("arbitrary",), vmem_limit_bytes=VMEM_LIMIT),
        name="experts",
    )(block_e, n_used, next_e, group, xs, wg, bg, wu, bu, wd, bd)


def _combine_kernel(dest_ref, ys_ref, x1_ref, w_ref, gfin_ref, o_ref, buf_ref, sem):
    i = pl.program_id(0)
    tm = x1_ref.shape[0]

    def gather(tile):
        slot = lax.rem(tile, 2)

        def body(r, c):
            base = (tile * tm + r) * TOP_K
            for k in range(TOP_K):
                d = dest_ref[base + k]
                pltpu.make_async_copy(ys_ref.at[pl.ds(d, 1)], buf_ref.at[slot, k, pl.ds(r, 1)],
                                      sem.at[slot]).start()
            return c

        lax.fori_loop(0, tm, body, 0, unroll=ROW_DMA_UNROLL)

    @pl.when(i == 0)
    def _():
        gather(i)

    @pl.when(i + 1 < pl.num_programs(0))
    def _():
        gather(i + 1)

    slot = lax.rem(i, 2)
    _row_copy_wait(ys_ref.at[pl.ds(0, tm)], buf_ref.at[slot, 0], sem.at[slot], TOP_K)
    w = w_ref[...]
    y = x1_ref[...]
    for k in range(TOP_K):
        y = y + buf_ref[slot, k] * w[:, k:k + 1]
    o_ref[...] = _rms(y, gfin_ref[...])


def _combine(dest, ys, x1, top_w, gfin):
    t = x1.shape[0]
    tm = TOKEN_TILE
    return pl.pallas_call(
        _combine_kernel,
        grid_spec=pltpu.PrefetchScalarGridSpec(
            num_scalar_prefetch=1,
            grid=(t // tm,),
            in_specs=[pl.BlockSpec(memory_space=pl.ANY),
                      pl.BlockSpec((tm, D_MODEL), lambda i, d: (i, 0)),
                      pl.BlockSpec((tm, TOP_K), lambda i, d: (i, 0)),
                      pl.BlockSpec((1, D_MODEL), lambda i, d: (0, 0))],
            out_specs=pl.BlockSpec((tm, D_MODEL), lambda i, d: (i, 0)),
            scratch_shapes=[pltpu.VMEM((2, TOP_K, tm, D_MODEL), jnp.float32),
                            pltpu.SemaphoreType.DMA((2,))],
        ),
        out_shape=jax.ShapeDtypeStruct((t, D_MODEL), jnp.float32),
        compiler_params=pltpu.CompilerParams(
            dimension_semantics=("arbitrary",), vmem_limit_bytes=VMEM_LIMIT),
        name="combine",
    )(dest, ys, x1, top_w, gfin)


def _rope_lane_table(rot, group):
    half = rot // 2
    inv = ROPE_THETA ** (-jnp.arange(0, rot, 2, dtype=jnp.float32) / rot)
    d = jnp.arange(LANES) % group
    first = jnp.arange(LANES) < (LANES if group < LANES else rot)
    in_lo = (d < half) & first
    in_hi = (d >= half) & (d < rot) & first
    freq = jnp.where(in_lo | in_hi, inv[d % half], 0.0)
    return jnp.stack([freq, in_lo.astype(jnp.float32), in_hi.astype(jnp.float32)]).astype(jnp.float32)


def kernel(x, positions, g_mix, w_in, lam_q1, lam_k1, lam_q2, lam_k2, g_subln, g_q, g_kv, w_uq, w_ukv, w_o_diff, w_o_mla, b_gates, w_out, g_ffn, w_router, b_router, w_gate, b_gate, w_up, b_up, w_down, b_down, g_final):
    batch, seq, d = x.shape
    t = batch * seq
    bf = jnp.bfloat16
    l = 0
    x2 = x.reshape(t, d)
    posf = positions.astype(jnp.float32).reshape(t, 1)

    w = w_in[l]
    win = jnp.concatenate([w[:, :IN_KR_END], jnp.zeros((d, LANES - MLA_ROPE), w.dtype),
                           w[:, IN_KR_END:]], axis=1).astype(bf)
    wuq = jnp.pad(w_uq[l].reshape(MLA_Q_LORA, MLA_HEADS, MLA_NOPE + MLA_ROPE),
                  ((0, 0), (0, 0), (0, MLA_QK_PAD - MLA_NOPE - MLA_ROPE))
                  ).reshape(MLA_Q_LORA, MLA_HEADS * MLA_QK_PAD).astype(bf)
    wukv = w_ukv[l].reshape(MLA_KV_LORA, MLA_HEADS, MLA_NOPE + MLA_V)
    wkn = wukv[:, :, :MLA_NOPE].reshape(MLA_KV_LORA, MLA_HEADS * MLA_NOPE).astype(bf)
    wv = wukv[:, :, MLA_NOPE:].reshape(MLA_KV_LORA, MLA_HEADS * MLA_V).astype(bf)
    fd = _rope_lane_table(DA_ROT, DA_HEAD_DIM)
    fm = _rope_lane_table(MLA_ROPE, LANES)

    qa, ka, va, gates, qm, km, vm = _proj(
        x2, posf, g_mix[l][None], win, b_gates[l][None],
        g_q[l][None], g_kv[l][None], wuq, wkn, wv, fd, fm)

    lamv = jnp.stack([lam_q1[l], lam_k1[l], lam_q2[l], lam_k2[l]]).astype(jnp.float32)
    oa = _dattn(qa, ka, va, lamv, g_subln[l][None], batch, seq)
    ob = _mattn(qm, km, vm, batch, seq)

    x1, h2, e_idx, top_w, rank, counts = _merge(
        x2, oa, ob, gates, w_o_diff[l].astype(bf), w_o_mla[l].astype(bf), w_out[l].astype(bf),
        g_ffn[l][None], w_router[l].T, b_router[l][:, None])

    bm = ROW_BLOCK
    n_blocks = (t * TOP_K) // bm + N_EXPERTS
    counts = counts[:, 0]
    padded = (counts + bm - 1) // bm * bm
    padded_end = jnp.cumsum(padded)
    padded_start = padded_end - padded
    n_used = (padded_end[-1] // bm).astype(jnp.int32)
    blk = jnp.minimum(jnp.arange(n_blocks, dtype=jnp.int32), n_used - 1)
    block_e = jnp.minimum(jnp.sum(padded_end[None, :] <= (blk * bm)[:, None], axis=1),
                          N_EXPERTS - 1).astype(jnp.int32)
    hot = e_idx[:, :, None] == jnp.arange(N_EXPERTS, dtype=jnp.int32)
    dest = (jnp.sum(jnp.where(hot, padded_start, 0), axis=-1) + rank).T.reshape(-1).astype(jnp.int32)

    n_used = n_used.reshape(1)
    xs = _dispatch(dest, (padded_start + counts).astype(jnp.int32), padded_end.astype(jnp.int32),
                   n_used, h2, n_blocks)
    later = block_e[None, :] > block_e[:, None]
    next_e = jnp.min(jnp.where(later, block_e[None, :], N_EXPERTS), axis=1)
    next_e = jnp.where(next_e == N_EXPERTS, -1, next_e).astype(jnp.int32)
    group = jnp.cumsum(jnp.concatenate([jnp.zeros((1,), jnp.int32),
                                        (block_e[1:] != block_e[:-1]).astype(jnp.int32)]))
    ys = _experts(block_e, n_used, next_e, group.astype(jnp.int32), xs, w_gate[l],
                  b_gate[l][:, None, :], w_up[l], b_up[l][:, None, :], w_down[l],
                  b_down[l][:, None, :], n_blocks)
    out = _combine(dest, ys, x1, top_w.T, g_final[None])
    return out.reshape(batch, seq, d)
```

```python
import math

import jax
import jax.numpy as jnp
from jax import lax
from jax.experimental import pallas as pl
from jax.experimental.pallas import tpu as pltpu

D_MODEL = 1024
ROPE_THETA = 500000.0
NORM_EPS = 1e-6
DA_HEADS = 8
DA_HEAD_DIM = 64
DA_ROT = DA_HEAD_DIM // 4
MLA_HEADS = 8
MLA_Q_LORA = 768
MLA_KV_LORA = 512
MLA_NOPE = 128
MLA_ROPE = 64
MLA_V = 128
N_EXPERTS = 32
TOP_K = 4
D_FF = 1024
SWIGLU_ALPHA = 1.702
SWIGLU_LIMIT = 7.0
LAM_INIT = 0.8 - 0.6 * math.exp(-0.3 * 0)

LANES = 128
MLA_QK_PAD = 256
TOKEN_TILE = 256
DISPATCH_TILE = 512
Q_TILE = 256
SCORE_SLOTS = 3
KEY_CHUNK = 256
ROW_BLOCK = 256
ROW_DMA_UNROLL = 8
VMEM_LIMIT = 56 * 1024 * 1024

LOG2_E = math.log2(math.e)

IN_W = {"qa": DA_HEADS * 2 * DA_HEAD_DIM, "ka": DA_HEADS * 2 * DA_HEAD_DIM,
        "va": DA_HEADS * 2 * DA_HEAD_DIM, "cq": MLA_Q_LORA,
        "ckr": MLA_KV_LORA + LANES,
        "g": 2 * D_MODEL}
IN_OFF = dict(zip(IN_W, [sum(list(IN_W.values())[:n]) for n in range(len(IN_W))]))
IN_KR_END = IN_OFF["ckr"] + MLA_KV_LORA + MLA_ROPE

_NT = (((1,), (1,)), ((), ()))


def _rms(x, g):
    return x * lax.rsqrt(jnp.mean(x * x, axis=-1, keepdims=True) + NORM_EPS) * g


def _dot(a, b):
    return jnp.dot(a, b, preferred_element_type=jnp.float32)


def _rope_tables(pos, freq, m_lo, m_hi):
    ang = pos * freq
    c = jnp.cos(ang)
    s = jnp.sin(ang)
    return c, -s * m_lo, s * m_hi


def _rope_block(xb, tables, half):
    c, s_lo, s_hi = tables
    return (xb * c + pltpu.roll(xb, LANES - half, 1) * s_lo
            + pltpu.roll(xb, half, 1) * s_hi)


def _proj_kernel(x_ref, pos_ref, gmix_ref, win_ref, bg_ref, gq_ref, gkv_ref, wuq_ref, wkn_ref, wv_ref,
                 fd_ref, fm_ref,
                 qa_ref, ka_ref, va_ref, gates_ref, qm_ref, km_ref, vm_ref):
    hb = _rms(x_ref[...], gmix_ref[...]).astype(jnp.bfloat16)
    pos = pos_ref[...]
    td = _rope_tables(pos, fd_ref[0:1, :], fd_ref[1:2, :], fd_ref[2:3, :])
    tmla = _rope_tables(pos, fm_ref[0:1, :], fm_ref[1:2, :], fm_ref[2:3, :])

    da_scale = DA_HEAD_DIM ** -0.5 * LOG2_E
    w_cols = lambda name: win_ref[:, IN_OFF[name]:IN_OFF[name] + IN_W[name]]
    mla_scale = (MLA_NOPE + MLA_ROPE) ** -0.5 * LOG2_E

    cq = _rms(_dot(hb, w_cols("cq")), gq_ref[...]).astype(jnp.bfloat16)
    ck = _dot(hb, w_cols("ckr"))
    kr = _rope_block(ck[:, MLA_KV_LORA:], tmla, MLA_ROPE // 2).astype(jnp.bfloat16)
    ckv = _rms(ck[:, :MLA_KV_LORA], gkv_ref[...]).astype(jnp.bfloat16)

    zq = _dot(hb, w_cols("qa"))
    for j in range(DA_HEADS):
        sl = slice(j * LANES, (j + 1) * LANES)
        qa_ref[:, sl] = (_rope_block(zq[:, sl], td, DA_ROT // 2) * da_scale).astype(jnp.bfloat16)

    qm = _dot(cq, wuq_ref[...])
    for h in range(MLA_HEADS):
        lo = slice(h * MLA_QK_PAD, h * MLA_QK_PAD + LANES)
        hi = slice(h * MLA_QK_PAD + LANES, (h + 1) * MLA_QK_PAD)
        qm_ref[:, lo] = (qm[:, lo] * mla_scale).astype(jnp.bfloat16)
        qm_ref[:, hi] = (_rope_block(qm[:, hi], tmla, MLA_ROPE // 2) * mla_scale).astype(jnp.bfloat16)

    zk = _dot(hb, w_cols("ka"))
    for j in range(DA_HEADS):
        sl = slice(j * LANES, (j + 1) * LANES)
        ka_ref[:, sl] = _rope_block(zk[:, sl], td, DA_ROT // 2).astype(jnp.bfloat16)

    kn = _dot(ckv, wkn_ref[...])
    for h in range(MLA_HEADS):
        km_ref[:, h * MLA_QK_PAD:h * MLA_QK_PAD + LANES] = kn[:, h * LANES:(h + 1) * LANES].astype(jnp.bfloat16)
        km_ref[:, h * MLA_QK_PAD + LANES:(h + 1) * MLA_QK_PAD] = kr
    vm_ref[...] = _dot(ckv, wv_ref[...]).astype(jnp.bfloat16)
    gates_ref[...] = jax.nn.sigmoid(_dot(hb, w_cols("g")) + bg_ref[...]).astype(jnp.bfloat16)
    va_ref[...] = _dot(hb, w_cols("va")).astype(jnp.bfloat16)


def _const_spec(shape):
    return pl.BlockSpec(shape, lambda i: (0,) * len(shape))


def _proj(x2, posf, gmix, win, bg, gq, gkv, wuq, wkn, wv, fd, fm):
    t = x2.shape[0]
    tm = TOKEN_TILE
    bf = jnp.bfloat16
    row = lambda w: pl.BlockSpec((tm, w), lambda i: (i, 0))
    consts = (gmix, win, bg, gq, gkv, wuq, wkn, wv, fd, fm)
    out_w = (D_MODEL, D_MODEL, D_MODEL, 2 * D_MODEL, MLA_HEADS * MLA_QK_PAD,
             MLA_HEADS * MLA_QK_PAD, MLA_HEADS * MLA_V)
    return pl.pallas_call(
        _proj_kernel,
        grid=(t // tm,),
        in_specs=[row(D_MODEL), row(1)] + [_const_spec(c.shape) for c in consts],
        out_specs=[row(w) for w in out_w],
        out_shape=[jax.ShapeDtypeStruct((t, w), bf) for w in out_w],
        compiler_params=pltpu.CompilerParams(
            dimension_semantics=("parallel",), vmem_limit_bytes=VMEM_LIMIT),
        name="proj",
    )(x2, posf, *consts)


def _pipelined_tiles(n_tiles, n_chunks, score_chunk, max_merge, max_store, max_load,
                     value_chunk, write_out):
    def fused(t_next, slot_next, t_cur, slot_cur):
        m_cur = None if t_cur is None else max_load(slot_cur)
        m_next, acc = None, None
        for c in range(n_chunks):
            if t_next is not None:
                m_next = max_merge(m_next, score_chunk(t_next, slot_next, c))
            if t_cur is not None:
                acc = value_chunk(slot_cur, c, m_cur, acc)
        if t_next is not None:
            max_store(slot_next, m_next)
        if t_cur is not None:
            write_out(t_cur, acc)

    fused(0, 0, None, None)
    fused(1, 1, None, None)
    n_triples = (n_tiles - 2) // SCORE_SLOTS

    def triple(j, carry):
        t = SCORE_SLOTS * j
        for i in range(SCORE_SLOTS):
            fused(t + i + 2, (i + 2) % SCORE_SLOTS, t + i, i)
        return carry

    lax.fori_loop(0, n_triples, triple, 0)
    for t in range(SCORE_SLOTS * n_triples, n_tiles):
        ahead = t + 2 if t + 2 < n_tiles else None
        fused(ahead, (t + 2) % SCORE_SLOTS, t, t % SCORE_SLOTS)


def _q_rows(t):
    if isinstance(t, int):
        return pl.ds(t * Q_TILE, Q_TILE)
    return pl.ds(pl.multiple_of(t * Q_TILE, Q_TILE), Q_TILE)


def _key_cols(c):
    return slice(c * KEY_CHUNK, (c + 1) * KEY_CHUNK)


def _fold_lanes(x, op):
    out = x[:, :LANES]
    for j in range(1, x.shape[1] // LANES):
        out = op(out, x[:, j * LANES:(j + 1) * LANES])
    return out


def _fill_values_ext(vx_ref, v_ref):
    width = v_ref.shape[1]
    vx_ref[:, :width] = v_ref[...]
    lane = lax.broadcasted_iota(jnp.int32, (v_ref.shape[0], vx_ref.shape[1] - width), 1)
    vx_ref[:, width:] = jnp.where(lane == 0, 1.0, 0.0).astype(vx_ref.dtype)


def _prob_values(s_ref, vx_ref, c, m, acc):
    p = jnp.exp2(s_ref[:, _key_cols(c)] - m).astype(jnp.bfloat16)
    part = _dot(p, vx_ref[_key_cols(c), :])
    return part if acc is None else acc + part


def _dattn_kernel(q_ref, k_ref, v_ref, lam_ref, gsub_ref, o_ref,
                  s0_ref, s1_ref, s2_ref, m0_ref, m1_ref, m2_ref, vx_ref, qm_ref):
    s_refs, m_refs = (s0_ref, s1_ref, s2_ref), (m0_ref, m1_ref, m2_ref)
    _fill_values_ext(vx_ref, v_ref)
    width = v_ref.shape[1]
    lamv = lam_ref[...]
    lam = (jnp.exp(jnp.sum(lamv[0:1] * lamv[1:2], axis=-1, keepdims=True))
           - jnp.exp(jnp.sum(lamv[2:3] * lamv[3:4], axis=-1, keepdims=True)) + LAM_INIT)
    q_all = q_ref[...]
    lane = lax.broadcasted_iota(jnp.int32, q_all.shape, 1)
    qm_ref[0] = jnp.where(lane < DA_HEAD_DIM, q_all, jnp.zeros_like(q_all))
    qm_ref[1] = jnp.where(lane >= DA_HEAD_DIM, q_all, jnp.zeros_like(q_all))

    def score_chunk(t, slot, c):
        k = k_ref[_key_cols(c), :]
        sc0 = lax.dot_general(qm_ref[0, _q_rows(t), :], k, _NT, preferred_element_type=jnp.float32)
        sc1 = lax.dot_general(qm_ref[1, _q_rows(t), :], k, _NT, preferred_element_type=jnp.float32)
        s_refs[slot][0, :, _key_cols(c)] = sc0
        s_refs[slot][1, :, _key_cols(c)] = sc1
        return _fold_lanes(sc0, jnp.maximum), _fold_lanes(sc1, jnp.maximum)

    def max_merge(m, mc):
        return mc if m is None else (jnp.maximum(m[0], mc[0]), jnp.maximum(m[1], mc[1]))

    def max_store(slot, m):
        m_refs[slot][0] = jnp.max(m[0], axis=-1, keepdims=True)
        m_refs[slot][1] = jnp.max(m[1], axis=-1, keepdims=True)

    def max_load(slot):
        return m_refs[slot][0], m_refs[slot][1]

    def value_chunk(slot, c, m, acc):
        acc0, acc1 = (None, None) if acc is None else acc
        return (_prob_values(s_refs[slot].at[0], vx_ref, c, m[0], acc0),
                _prob_values(s_refs[slot].at[1], vx_ref, c, m[1], acc1))

    def write_out(t, acc):
        ox0, ox1 = acc
        o = (ox0[:, :width] * (1.0 / ox0[:, width:width + 1])
             - ox1[:, :width] * (lam / ox1[:, width:width + 1]))
        o_ref[_q_rows(t), :] = (_rms(o, gsub_ref[...]) * (1.0 - LAM_INIT)).astype(o_ref.dtype)

    _pipelined_tiles(q_ref.shape[0] // Q_TILE, k_ref.shape[0] // KEY_CHUNK,
                     score_chunk, max_merge, max_store, max_load, value_chunk, write_out)


def _dattn(qa, ka, va, lamv, gsub, batch, seq):
    head = lambda b, h: (b, h)
    return pl.pallas_call(
        _dattn_kernel,
        grid=(batch, DA_HEADS),
        in_specs=[
            pl.BlockSpec((seq, LANES), head),
            pl.BlockSpec((seq, LANES), head),
            pl.BlockSpec((seq, LANES), head),
            pl.BlockSpec(lamv.shape, lambda b, h: (0, 0)),
            pl.BlockSpec(gsub.shape, lambda b, h: (0, 0)),
        ],
        out_specs=pl.BlockSpec((seq, LANES), head),
        out_shape=jax.ShapeDtypeStruct(qa.shape, jnp.bfloat16),
        scratch_shapes=[pltpu.VMEM((2, Q_TILE, seq), jnp.float32)] * SCORE_SLOTS
        + [pltpu.VMEM((2, Q_TILE, 1), jnp.float32)] * SCORE_SLOTS
        + [pltpu.VMEM((seq, 2 * LANES), jnp.bfloat16), pltpu.VMEM((2, seq, LANES), jnp.bfloat16)],
        compiler_params=pltpu.CompilerParams(
            dimension_semantics=("parallel", "parallel"), vmem_limit_bytes=VMEM_LIMIT),
        name="dattn",
    )(qa, ka, va, lamv, gsub)


def _mattn_kernel(q_ref, k_ref, v_ref, o_ref, s0_ref, s1_ref, s2_ref, m0_ref, m1_ref, m2_ref,
                  vx_ref):
    s_refs, m_refs = (s0_ref, s1_ref, s2_ref), (m0_ref, m1_ref, m2_ref)
    _fill_values_ext(vx_ref, v_ref)
    width = v_ref.shape[1]

    def score_chunk(t, slot, c):
        sc = lax.dot_general(q_ref[_q_rows(t), :], k_ref[_key_cols(c), :], _NT,
                             preferred_element_type=jnp.float32)
        s_refs[slot][:, _key_cols(c)] = sc
        return _fold_lanes(sc, jnp.maximum)

    def max_merge(m, mc):
        return mc if m is None else jnp.maximum(m, mc)

    def max_store(slot, m):
        m_refs[slot][...] = jnp.max(m, axis=-1, keepdims=True)

    def max_load(slot):
        return m_refs[slot][...]

    def value_chunk(slot, c, m, acc):
        return _prob_values(s_refs[slot], vx_ref, c, m, acc)

    def write_out(t, ox):
        o_ref[_q_rows(t), :] = (ox[:, :width] * (1.0 / ox[:, width:width + 1])).astype(o_ref.dtype)

    _pipelined_tiles(q_ref.shape[0] // Q_TILE, k_ref.shape[0] // KEY_CHUNK,
                     score_chunk, max_merge, max_store, max_load, value_chunk, write_out)


def _mattn(qm, km, vm, batch, seq):
    head = lambda b, h: (b, h)
    return pl.pallas_call(
        _mattn_kernel,
        grid=(batch, MLA_HEADS),
        in_specs=[
            pl.BlockSpec((seq, MLA_QK_PAD), head),
            pl.BlockSpec((seq, MLA_QK_PAD), head),
            pl.BlockSpec((seq, MLA_V), head),
        ],
        out_specs=pl.BlockSpec((seq, MLA_V), head),
        out_shape=jax.ShapeDtypeStruct(vm.shape, jnp.bfloat16),
        scratch_shapes=[pltpu.VMEM((Q_TILE, seq), jnp.float32)] * SCORE_SLOTS
        + [pltpu.VMEM((Q_TILE, 1), jnp.float32)] * SCORE_SLOTS
        + [pltpu.VMEM((seq, 2 * MLA_V), jnp.bfloat16)],
        compiler_params=pltpu.CompilerParams(
            dimension_semantics=("parallel", "parallel"), vmem_limit_bytes=VMEM_LIMIT),
        name="mattn",
    )(qm, km, vm)


def _rows(parts, dtype):
    sub = lax.broadcasted_iota(jnp.int32, (len(parts), parts[0].shape[1]), 0)
    out = jnp.zeros(sub.shape, dtype)
    for k, p in enumerate(parts):
        out = jnp.where(sub == k, p.astype(dtype), out)
    return out


def _merge_kernel(x_ref, oa_ref, ob_ref, gates_ref, woa_ref, wob_ref, wout_ref, gffn_ref,
                  wrt_ref, brt_ref,
                  x1_ref, h2_ref, idx_ref, w_ref, rank_ref, counts_ref, carry_ref, hprev_ref):
    i = pl.program_id(0)

    @pl.when(i == 0)
    def _():
        carry_ref[...] = jnp.zeros_like(carry_ref)
        hprev_ref[...] = jnp.zeros_like(hprev_ref)

    gates = gates_ref[...].astype(jnp.float32)
    half = D_MODEL // 2

    logits = lax.dot_general(wrt_ref[...], hprev_ref[...], _NT, precision=lax.Precision.HIGHEST,
                             preferred_element_type=jnp.float32) + brt_ref[...]
    tm = logits.shape[1]
    sub = lax.broadcasted_iota(jnp.int32, logits.shape, 0)
    vals, idxs, hots = [], [], []
    l = logits

    def pick(l):
        m = jnp.max(l, axis=0, keepdims=True)
        idx = jnp.min(jnp.where(l == m, sub, N_EXPERTS), axis=0, keepdims=True)
        hot = sub == idx
        vals.append(m)
        idxs.append(idx)
        hots.append(hot)
        return jnp.where(hot, -jnp.inf, l)

    ya_lo = gates[:, :half] * _dot(oa_ref[...], woa_ref[:, :half])
    l = pick(l)
    ya_hi = gates[:, half:D_MODEL] * _dot(oa_ref[...], woa_ref[:, half:])
    l = pick(l)
    yb_lo = gates[:, D_MODEL:D_MODEL + half] * _dot(ob_ref[...], wob_ref[:, :half])
    l = pick(l)
    yb_hi = gates[:, D_MODEL + half:] * _dot(ob_ref[...], wob_ref[:, half:])
    l = pick(l)
    merged = jnp.concatenate([ya_lo + yb_lo, ya_hi + yb_hi], axis=1).astype(jnp.bfloat16)

    es = [jnp.exp(v - vals[0]) for v in vals]
    den = es[0] + es[1] + es[2] + es[3]
    w_ref[...] = _rows([e / den for e in es], jnp.float32)
    idx_ref[...] = _rows(idxs, jnp.int32)
    x1_lo = x_ref[:, :half] + _dot(merged, wout_ref[:, :half])

    chosen = (hots[0] | hots[1] | hots[2] | hots[3]).astype(jnp.float32)
    r_i = lax.broadcasted_iota(jnp.int32, (tm, tm), 0)
    c_i = lax.broadcasted_iota(jnp.int32, (tm, tm), 1)
    earlier = (r_i < c_i).astype(jnp.bfloat16)
    prefix = _dot(chosen.astype(jnp.bfloat16), earlier) + carry_ref[...]
    x1_hi = x_ref[:, half:] + _dot(merged, wout_ref[:, half:])
    ranks = [jnp.sum(jnp.where(h, prefix, 0.0), axis=0, keepdims=True) for h in hots]
    rank_ref[...] = _rows(ranks, jnp.int32)
    live = (i > 0).astype(jnp.float32)
    carry = carry_ref[...] + live * jnp.sum(chosen, axis=1, keepdims=True)
    carry_ref[...] = carry
    counts_ref[...] = carry.astype(jnp.int32)

    x1 = jnp.concatenate([x1_lo, x1_hi], axis=1)
    x1_ref[...] = x1
    h2 = _rms(x1, gffn_ref[...])
    h2_ref[...] = h2
    hprev_ref[...] = h2


def _merge(x2, oa, ob, gates, woa, wob, wout, gffn, wrt, brt):
    t = x2.shape[0]
    tm = TOKEN_TILE
    n = t // tm
    row = lambda w: pl.BlockSpec((tm, w), lambda i: (jnp.minimum(i, n - 1), 0))
    col = lambda: pl.BlockSpec((TOP_K, tm), lambda i: (0, jnp.maximum(i - 1, 0)))
    consts = (woa, wob, wout, gffn, wrt, brt)
    return pl.pallas_call(
        _merge_kernel,
        grid=(n + 1,),
        in_specs=[row(D_MODEL), row(D_MODEL), row(D_MODEL), row(2 * D_MODEL)]
        + [_const_spec(c.shape) for c in consts],
        out_specs=[row(D_MODEL), row(D_MODEL), col(), col(), col(),
                   _const_spec((N_EXPERTS, 1))],
        out_shape=[
            jax.ShapeDtypeStruct((t, D_MODEL), jnp.float32),
            jax.ShapeDtypeStruct((t, D_MODEL), jnp.float32),
            jax.ShapeDtypeStruct((TOP_K, t), jnp.int32),
            jax.ShapeDtypeStruct((TOP_K, t), jnp.float32),
            jax.ShapeDtypeStruct((TOP_K, t), jnp.int32),
            jax.ShapeDtypeStruct((N_EXPERTS, 1), jnp.int32),
        ],
        scratch_shapes=[pltpu.VMEM((N_EXPERTS, 1), jnp.float32),
                        pltpu.VMEM((tm, D_MODEL), jnp.float32)],
        compiler_params=pltpu.CompilerParams(
            dimension_semantics=("arbitrary",), vmem_limit_bytes=VMEM_LIMIT),
        name="merge",
    )(x2, oa, ob, gates, *consts)


def _row_copy_wait(src_rows_ref, dst_rows_ref, sem, n):
    for _ in range(n):
        pltpu.make_async_copy(src_rows_ref, dst_rows_ref, sem).wait()


def _dispatch_kernel(dest_ref, pad_lo_ref, pad_hi_ref, nu_ref, h2_ref, xs_ref, zero_ref, sem, zsem):
    i = pl.program_id(0)
    tm = h2_ref.shape[0]
    bm = zero_ref.shape[0]
    n_blocks = xs_ref.shape[0] // bm

    @pl.when(i == 0)
    def _():
        zero_ref[...] = jnp.zeros_like(zero_ref)

        def pad_rows(fn):
            def per_expert(e, c):
                return lax.fori_loop(pad_lo_ref[e], pad_hi_ref[e], fn, c)
            lax.fori_loop(0, N_EXPERTS, per_expert, 0)

        def row_copy(j):
            return pltpu.make_async_copy(zero_ref.at[pl.ds(0, 1)], xs_ref.at[pl.ds(j, 1)], zsem)

        def blk_copy(b):
            return pltpu.make_async_copy(zero_ref, xs_ref.at[pl.ds(pl.multiple_of(b * bm, bm), bm)], zsem)

        def start_row(j, c):
            row_copy(j).start()
            return c

        def wait_row(j, c):
            row_copy(j).wait()
            return c

        def start_blk(b, c):
            blk_copy(b).start()
            return c

        def wait_blk(b, c):
            blk_copy(b).wait()
            return c

        pad_rows(start_row)
        lax.fori_loop(nu_ref[0], n_blocks, start_blk, 0)
        pad_rows(wait_row)
        lax.fori_loop(nu_ref[0], n_blocks, wait_blk, 0)

    def body(r, c):
        base = (i * tm + r) * TOP_K
        for k in range(TOP_K):
            d = dest_ref[base + k]
            pltpu.make_async_copy(h2_ref.at[pl.ds(r, 1)], xs_ref.at[pl.ds(d, 1)], sem).start()
        return c

    lax.fori_loop(0, tm, body, 0, unroll=ROW_DMA_UNROLL)
    _row_copy_wait(h2_ref, xs_ref.at[pl.ds(0, tm)], sem, TOP_K)


def _dispatch(dest, pad_lo, pad_hi, n_used, h2, n_blocks):
    t = h2.shape[0]
    tm = DISPATCH_TILE
    bm = ROW_BLOCK
    return pl.pallas_call(
        _dispatch_kernel,
        grid_spec=pltpu.PrefetchScalarGridSpec(
            num_scalar_prefetch=4,
            grid=(t // tm,),
            in_specs=[pl.BlockSpec((tm, D_MODEL), lambda i, *_: (i, 0))],
            out_specs=pl.BlockSpec(memory_space=pl.ANY),
            scratch_shapes=[pltpu.VMEM((bm, D_MODEL), jnp.float32),
                            pltpu.SemaphoreType.DMA(()), pltpu.SemaphoreType.DMA(())],
        ),
        out_shape=jax.ShapeDtypeStruct((n_blocks * bm, D_MODEL), jnp.float32),
        compiler_params=pltpu.CompilerParams(
            dimension_semantics=("arbitrary",), vmem_limit_bytes=VMEM_LIMIT),
        name="dispatch",
    )(dest, pad_lo, pad_hi, n_used, h2)


def _experts_kernel(be_ref, nu_ref, nxt_ref, grp_ref, xs_ref, wg_ref, bg_ref, wu_ref, bu_ref, wd_ref,
                    bd_ref, ys_ref, stage_ref, wbf_ref, sem):
    b = pl.program_id(0)
    e = be_ref[b]
    slot = lax.rem(grp_ref[b], 2)
    w_hbm = (wg_ref, wu_ref, wd_ref)

    def weight_copies(expert, slot):
        return [pltpu.make_async_copy(w.at[expert], stage_ref.at[slot, j], sem.at[slot, j])
                for j, w in enumerate(w_hbm)]

    @pl.when(b == 0)
    def _():
        for cp in weight_copies(e, slot):
            cp.start()

    @pl.when(jnp.logical_or(b == 0, e != be_ref[jnp.maximum(b - 1, 0)]))
    def _():
        for j, cp in enumerate(weight_copies(e, slot)):
            cp.wait()
            wbf_ref[j] = stage_ref[slot, j].astype(jnp.bfloat16)

        @pl.when(nxt_ref[b] >= 0)
        def _():
            for cp in weight_copies(nxt_ref[b], 1 - slot):
                cp.start()

    @pl.when(b >= nu_ref[0])
    def _():
        ys_ref[...] = jnp.zeros_like(ys_ref)

    @pl.when(b < nu_ref[0])
    def _():
        xb = xs_ref[...].astype(jnp.bfloat16)
        gate = jnp.minimum(_dot(xb, wbf_ref[0]) + bg_ref[0], SWIGLU_LIMIT)
        up = jnp.clip(_dot(xb, wbf_ref[1]) + bu_ref[0], -SWIGLU_LIMIT, SWIGLU_LIMIT)
        act = (up + 1.0) * (gate * jax.nn.sigmoid(SWIGLU_ALPHA * gate))
        ys_ref[...] = _dot(act.astype(jnp.bfloat16), wbf_ref[2]) + bd_ref[0]


def _experts(block_e, n_used, next_e, group, xs, wg, bg, wu, bu, wd, bd, n_blocks):
    bm = ROW_BLOCK
    rows = lambda b, be, nu, *_: (jnp.maximum(jnp.minimum(b, nu[0] - 1), 0), 0)
    hbm = lambda: pl.BlockSpec(memory_space=pl.ANY)
    bspec = lambda: pl.BlockSpec((1, 1, D_FF), lambda b, be, *_: (be[b], 0, 0))
    return pl.pallas_call(
        _experts_kernel,
        grid_spec=pltpu.PrefetchScalarGridSpec(
            num_scalar_prefetch=4,
            grid=(n_blocks,),
            in_specs=[pl.BlockSpec((bm, D_MODEL), rows), hbm(), bspec(), hbm(), bspec(),
                      hbm(), bspec()],
            out_specs=pl.BlockSpec((bm, D_MODEL), lambda b, *_: (b, 0)),
            scratch_shapes=[pltpu.VMEM((2, 3, D_MODEL, D_FF), jnp.float32),
                            pltpu.VMEM((3, D_MODEL, D_FF), jnp.bfloat16),
                            pltpu.SemaphoreType.DMA((2, 3))],
        ),
        out_shape=jax.ShapeDtypeStruct((n_blocks * bm, D_MODEL), jnp.float32),
        compiler_params=pltpu.CompilerParams(
            dimension_semantics=("arbitrary",), vmem_limit_bytes=VMEM_LIMIT),
        name="experts",
    )(block_e, n_used, next_e, group, xs, wg, bg, wu, bu, wd, bd)


def _combine_kernel(dest_ref, ys_ref, x1_ref, w_ref, gfin_ref, o_ref, buf_ref, sem):
    i = pl.program_id(0)
    tm = x1_ref.shape[0]

    def gather(tile):
        slot = lax.rem(tile, 2)

        def body(r, c):
            base = (tile * tm + r) * TOP_K
            for k in range(TOP_K):
                d = dest_ref[base + k]
                pltpu.make_async_copy(ys_ref.at[pl.ds(d, 1)], buf_ref.at[slot, k, pl.ds(r, 1)],
                                      sem.at[slot]).start()
            return c

        lax.fori_loop(0, tm, body, 0, unroll=ROW_DMA_UNROLL)

    @pl.when(i == 0)
    def _():
        gather(i)

    @pl.when(i + 1 < pl.num_programs(0))
    def _():
        gather(i + 1)

    slot = lax.rem(i, 2)
    _row_copy_wait(ys_ref.at[pl.ds(0, tm)], buf_ref.at[slot, 0], sem.at[slot], TOP_K)
    w = w_ref[...]
    y = x1_ref[...]
    for k in range(TOP_K):
        y = y + buf_ref[slot, k] * w[:, k:k + 1]
    o_ref[...] = _rms(y, gfin_ref[...])


def _combine(dest, ys, x1, top_w, gfin):
    t = x1.shape[0]
    tm = TOKEN_TILE
    return pl.pallas_call(
        _combine_kernel,
        grid_spec=pltpu.PrefetchScalarGridSpec(
            num_scalar_prefetch=1,
            grid=(t // tm,),
            in_specs=[pl.BlockSpec(memory_space=pl.ANY),
                      pl.BlockSpec((tm, D_MODEL), lambda i, d: (i, 0)),
                      pl.BlockSpec((tm, TOP_K), lambda i, d: (i, 0)),
                      pl.BlockSpec((1, D_MODEL), lambda i, d: (0, 0))],
            out_specs=pl.BlockSpec((tm, D_MODEL), lambda i, d: (i, 0)),
            scratch_shapes=[pltpu.VMEM((2, TOP_K, tm, D_MODEL), jnp.float32),
                            pltpu.SemaphoreType.DMA((2,))],
        ),
        out_shape=jax.ShapeDtypeStruct((t, D_MODEL), jnp.float32),
        compiler_params=pltpu.CompilerParams(
            dimension_semantics=("arbitrary",), vmem_limit_bytes=VMEM_LIMIT),
        name="combine",
    )(dest, ys, x1, top_w, gfin)


def _rope_lane_table(rot, group):
    half = rot // 2
    inv = ROPE_THETA ** (-jnp.arange(0, rot, 2, dtype=jnp.float32) / rot)
    d = jnp.arange(LANES) % group
    first = jnp.arange(LANES) < (LANES if group < LANES else rot)
    in_lo = (d < half) & first
    in_hi = (d >= half) & (d < rot) & first
    freq = jnp.where(in_lo | in_hi, inv[d % half], 0.0)
    return jnp.stack([freq, in_lo.astype(jnp.float32), in_hi.astype(jnp.float32)]).astype(jnp.float32)


def kernel(x, positions, g_mix, w_in, lam_q1, lam_k1, lam_q2, lam_k2, g_subln, g_q, g_kv, w_uq, w_ukv, w_o_diff, w_o_mla, b_gates, w_out, g_ffn, w_router, b_router, w_gate, b_gate, w_up, b_up, w_down, b_down, g_final):
    batch, seq, d = x.shape
    t = batch * seq
    bf = jnp.bfloat16
    l = 0
    x2 = x.reshape(t, d)
    posf = positions.astype(jnp.float32).reshape(t, 1)

    w = w_in[l]
    win = jnp.concatenate([w[:, :IN_KR_END], jnp.zeros((d, LANES - MLA_ROPE), w.dtype),
                           w[:, IN_KR_END:]], axis=1).astype(bf)
    wuq = jnp.pad(w_uq[l].reshape(MLA_Q_LORA, MLA_HEADS, MLA_NOPE + MLA_ROPE),
                  ((0, 0), (0, 0), (0, MLA_QK_PAD - MLA_NOPE - MLA_ROPE))
                  ).reshape(MLA_Q_LORA, MLA_HEADS * MLA_QK_PAD).astype(bf)
    wukv = w_ukv[l].reshape(MLA_KV_LORA, MLA_HEADS, MLA_NOPE + MLA_V)
    wkn = wukv[:, :, :MLA_NOPE].reshape(MLA_KV_LORA, MLA_HEADS * MLA_NOPE).astype(bf)
    wv = wukv[:, :, MLA_NOPE:].reshape(MLA_KV_LORA, MLA_HEADS * MLA_V).astype(bf)
    fd = _rope_lane_table(DA_ROT, DA_HEAD_DIM)
    fm = _rope_lane_table(MLA_ROPE, LANES)

    qa, ka, va, gates, qm, km, vm = _proj(
        x2, posf, g_mix[l][None], win, b_gates[l][None],
        g_q[l][None], g_kv[l][None], wuq, wkn, wv, fd, fm)

    lamv = jnp.stack([lam_q1[l], lam_k1[l], lam_q2[l], lam_k2[l]]).astype(jnp.float32)
    oa = _dattn(qa, ka, va, lamv, g_subln[l][None], batch, seq)
    ob = _mattn(qm, km, vm, batch, seq)

    x1, h2, e_idx, top_w, rank, counts = _merge(
        x2, oa, ob, gates, w_o_diff[l].astype(bf), w_o_mla[l].astype(bf), w_out[l].astype(bf),
        g_ffn[l][None], w_router[l].T, b_router[l][:, None])

    bm = ROW_BLOCK
    n_blocks = (t * TOP_K) // bm + N_EXPERTS
    counts = counts[:, 0]
    padded = (counts + bm - 1) // bm * bm
    padded_end = jnp.cumsum(padded)
    padded_start = padded_end - padded
    n_used = (padded_end[-1] // bm).astype(jnp.int32)
    blk = jnp.minimum(jnp.arange(n_blocks, dtype=jnp.int32), n_used - 1)
    block_e = jnp.minimum(jnp.sum(padded_end[None, :] <= (blk * bm)[:, None], axis=1),
                          N_EXPERTS - 1).astype(jnp.int32)
    hot = e_idx[:, :, None] == jnp.arange(N_EXPERTS, dtype=jnp.int32)
    dest = (jnp.sum(jnp.where(hot, padded_start, 0), axis=-1) + rank).T.reshape(-1).astype(jnp.int32)

    n_used = n_used.reshape(1)
    xs = _dispatch(dest, (padded_start + counts).astype(jnp.int32), padded_end.astype(jnp.int32),
                   n_used, h2, n_blocks)
    later = block_e[None, :] > block_e[:, None]
    next_e = jnp.min(jnp.where(later, block_e[None, :], N_EXPERTS), axis=1)
    next_e = jnp.where(next_e == N_EXPERTS, -1, next_e).astype(jnp.int32)
    group = jnp.cumsum(jnp.concatenate([jnp.zeros((1,), jnp.int32),
                                        (block_e[1:] != block_e[:-1]).astype(jnp.int32)]))
    ys = _experts(block_e, n_used, next_e, group.astype(jnp.int32), xs, w_gate[l],
                  b_gate[l][:, None, :], w_up[l], b_up[l][:, None, :], w_down[l],
                  b_down[l][:, None, :], n_blocks)
    out = _combine(dest, ys, x1, top_w.T, g_final[None])
    return out.reshape(batch, seq, d)
```

```python
import math

import jax
import jax.numpy as jnp
from jax import lax
from jax.experimental import pallas as pl
from jax.experimental.pallas import tpu as pltpu

D_MODEL = 1024
ROPE_THETA = 500000.0
NORM_EPS = 1e-6
DA_HEADS = 8
DA_HEAD_DIM = 64
DA_ROT = DA_HEAD_DIM // 4
MLA_HEADS = 8
MLA_Q_LORA = 768
MLA_KV_LORA = 512
MLA_NOPE = 128
MLA_ROPE = 64
MLA_V = 128
N_EXPERTS = 32
TOP_K = 4
D_FF = 1024
SWIGLU_ALPHA = 1.702
SWIGLU_LIMIT = 7.0
LAM_INIT = 0.8 - 0.6 * math.exp(-0.3 * 0)

LANES = 128
SUBLANES = 8
PV_ROWS = LANES + 16
MLA_QK_PAD = 256
TOKEN_TILE = 256
DISPATCH_TILE = 512
Q_TILE = 256
SCORE_SLOTS = 3
KEY_CHUNK = 256
ROW_BLOCK = 256
ROW_DMA_UNROLL = 8
VMEM_LIMIT = 56 * 1024 * 1024

LOG2_E = math.log2(math.e)

IN_W = {"qa": DA_HEADS * 2 * DA_HEAD_DIM, "ka": DA_HEADS * 2 * DA_HEAD_DIM,
        "va": DA_HEADS * 2 * DA_HEAD_DIM, "cq": MLA_Q_LORA,
        "ckr": MLA_KV_LORA + LANES,
        "g": 2 * D_MODEL}
IN_OFF = dict(zip(IN_W, [sum(list(IN_W.values())[:n]) for n in range(len(IN_W))]))
IN_KR_END = IN_OFF["ckr"] + MLA_KV_LORA + MLA_ROPE

_NT = (((1,), (1,)), ((), ()))


def _rms(x, g):
    return x * lax.rsqrt(jnp.mean(x * x, axis=-1, keepdims=True) + NORM_EPS) * g


def _dot(a, b):
    return jnp.dot(a, b, preferred_element_type=jnp.float32)


def _rope_tables(pos, freq, m_lo, m_hi):
    ang = pos * freq
    c = jnp.cos(ang)
    s = jnp.sin(ang)
    return c, -s * m_lo, s * m_hi


def _rope_block(xb, tables, half):
    c, s_lo, s_hi = tables
    return (xb * c + pltpu.roll(xb, LANES - half, 1) * s_lo
            + pltpu.roll(xb, half, 1) * s_hi)


def _proj_kernel(x_ref, pos_ref, gmix_ref, win_ref, bg_ref, gq_ref, gkv_ref, wuq_ref, wkn_ref,
                 wvat_ref, wvt_ref, fd_ref, fm_ref,
                 qa_ref, ka_ref, vat_ref, gates_ref, qm_ref, km_ref, vmt_ref):
    hb = _rms(x_ref[...], gmix_ref[...]).astype(jnp.bfloat16)
    tm = hb.shape[0]

    da_scale = DA_HEAD_DIM ** -0.5 * LOG2_E
    w_cols = lambda name: win_ref[:, IN_OFF[name]:IN_OFF[name] + IN_W[name]]
    mla_scale = (MLA_NOPE + MLA_ROPE) ** -0.5 * LOG2_E

    def in_cols(name, lo, width):
        return win_ref[:, IN_OFF[name] + lo:IN_OFF[name] + lo + width]

    n_groups = 4
    heads = DA_HEADS // n_groups

    cq = _rms(_dot(hb, w_cols("cq")), gq_ref[...]).astype(jnp.bfloat16)
    ck = _dot(hb, w_cols("ckr"))
    td_parts, tmla_parts = [], []
    for g in range(n_groups):
        rows = slice(g * tm // n_groups, (g + 1) * tm // n_groups)
        pos = pos_ref[rows, :]
        td_parts.append(_rope_tables(pos, fd_ref[0:1, :], fd_ref[1:2, :], fd_ref[2:3, :]))
        tmla_parts.append(_rope_tables(pos, fm_ref[0:1, :], fm_ref[1:2, :], fm_ref[2:3, :]))
        cols = slice(g * heads * LANES, (g + 1) * heads * LANES)
        gw = 2 * D_MODEL // n_groups
        gcols = slice(g * gw, (g + 1) * gw)
        gates_ref[:, gcols] = jax.nn.sigmoid(
            _dot(hb, in_cols("g", g * gw, gw)) + bg_ref[:, gcols]).astype(jnp.bfloat16)
        vat_ref[cols, :] = lax.dot_general(wvat_ref[cols, :], hb, _NT,
                                           preferred_element_type=jnp.float32).astype(jnp.bfloat16)
    td = tuple(jnp.concatenate([p[k] for p in td_parts], axis=0) for k in range(3))
    tmla = tuple(jnp.concatenate([p[k] for p in tmla_parts], axis=0) for k in range(3))
    kr = _rope_block(ck[:, MLA_KV_LORA:], tmla, MLA_ROPE // 2).astype(jnp.bfloat16)
    ckv = _rms(ck[:, :MLA_KV_LORA], gkv_ref[...]).astype(jnp.bfloat16)

    for g in range(n_groups):
        zq = _dot(hb, in_cols("qa", g * heads * LANES, heads * LANES))
        for j in range(heads):
            dst = slice((g * heads + j) * LANES, (g * heads + j + 1) * LANES)
            blk = zq[:, j * LANES:(j + 1) * LANES]
            qa_ref[:, dst] = (_rope_block(blk, td, DA_ROT // 2) * da_scale).astype(jnp.bfloat16)

        qm = _dot(cq, wuq_ref[:, g * heads * MLA_QK_PAD:(g + 1) * heads * MLA_QK_PAD])
        for j in range(heads):
            h = g * heads + j
            lo = slice(j * MLA_QK_PAD, j * MLA_QK_PAD + LANES)
            hi = slice(j * MLA_QK_PAD + LANES, (j + 1) * MLA_QK_PAD)
            qm_ref[:, h * MLA_QK_PAD:h * MLA_QK_PAD + LANES] = (qm[:, lo] * mla_scale).astype(jnp.bfloat16)
            qm_ref[:, h * MLA_QK_PAD + LANES:(h + 1) * MLA_QK_PAD] = (
                _rope_block(qm[:, hi], tmla, MLA_ROPE // 2) * mla_scale).astype(jnp.bfloat16)

        zk = _dot(hb, in_cols("ka", g * heads * LANES, heads * LANES))
        for j in range(heads):
            dst = slice((g * heads + j) * LANES, (g * heads + j + 1) * LANES)
            blk = zk[:, j * LANES:(j + 1) * LANES]
            ka_ref[:, dst] = _rope_block(blk, td, DA_ROT // 2).astype(jnp.bfloat16)

        kn = _dot(ckv, wkn_ref[:, g * heads * LANES:(g + 1) * heads * LANES])
        for j in range(heads):
            h = g * heads + j
            km_ref[:, h * MLA_QK_PAD:h * MLA_QK_PAD + LANES] = kn[:, j * LANES:(j + 1) * LANES].astype(jnp.bfloat16)
            km_ref[:, h * MLA_QK_PAD + LANES:(h + 1) * MLA_QK_PAD] = kr

        cols = slice(g * heads * LANES, (g + 1) * heads * LANES)
        vmt_ref[cols, :] = lax.dot_general(wvt_ref[cols, :], ckv, _NT,
                                           preferred_element_type=jnp.float32).astype(jnp.bfloat16)


def _const_spec(shape):
    return pl.BlockSpec(shape, lambda i: (0,) * len(shape))


def _proj(x2, posf, gmix, win, bg, gq, gkv, wuq, wkn, wvat, wvt, fd, fm):
    t = x2.shape[0]
    tm = TOKEN_TILE
    bf = jnp.bfloat16
    row = lambda w: pl.BlockSpec((tm, w), lambda i: (i, 0))
    col = lambda w: pl.BlockSpec((w, tm), lambda i: (0, i))
    consts = (gmix, win, bg, gq, gkv, wuq, wkn, wvat, wvt, fd, fm)
    outs = ((row, D_MODEL), (row, D_MODEL), (col, D_MODEL), (row, 2 * D_MODEL),
            (row, MLA_HEADS * MLA_QK_PAD), (row, MLA_HEADS * MLA_QK_PAD), (col, MLA_HEADS * MLA_V))
    return pl.pallas_call(
        _proj_kernel,
        grid=(t // tm,),
        in_specs=[row(D_MODEL), row(1)] + [_const_spec(c.shape) for c in consts],
        out_specs=[kind(w) for kind, w in outs],
        out_shape=[jax.ShapeDtypeStruct((t, w) if kind is row else (w, t), bf) for kind, w in outs],
        compiler_params=pltpu.CompilerParams(
            dimension_semantics=("parallel",), vmem_limit_bytes=VMEM_LIMIT),
        name="proj",
    )(x2, posf, *consts)


def _pipelined_tiles(n_tiles, n_chunks, score_chunk, max_merge, max_store, max_load,
                     value_chunk, write_out):
    def fused(t_next, slot_next, t_cur, slot_cur):
        m_cur = None if t_cur is None else max_load(slot_cur)
        m_next, acc = None, None
        for c in range(n_chunks):
            if t_next is not None:
                m_next = max_merge(m_next, score_chunk(t_next, slot_next, c))
            if t_cur is not None:
                acc = value_chunk(slot_cur, c, m_cur, acc)
        if t_next is not None:
            max_store(slot_next, m_next)
        if t_cur is not None:
            write_out(t_cur, acc)

    fused(0, 0, None, None)
    fused(1, 1, None, None)
    n_triples = (n_tiles - 2) // SCORE_SLOTS

    def triple(j, carry):
        t = SCORE_SLOTS * j
        for i in range(SCORE_SLOTS):
            fused(t + i + 2, (i + 2) % SCORE_SLOTS, t + i, i)
        return carry

    lax.fori_loop(0, n_triples, triple, 0)
    for t in range(SCORE_SLOTS * n_triples, n_tiles):
        ahead = t + 2 if t + 2 < n_tiles else None
        fused(ahead, (t + 2) % SCORE_SLOTS, t, t % SCORE_SLOTS)


def _q_rows(t):
    if isinstance(t, int):
        return pl.ds(t * Q_TILE, Q_TILE)
    return pl.ds(pl.multiple_of(t * Q_TILE, Q_TILE), Q_TILE)


def _key_cols(c):
    return slice(c * KEY_CHUNK, (c + 1) * KEY_CHUNK)


def _fold_keys(x, op):
    out = x[:SUBLANES]
    for j in range(1, x.shape[0] // SUBLANES):
        out = op(out, x[j * SUBLANES:(j + 1) * SUBLANES])
    return out


def _fill_values_ext(vx_ref, vt_ref):
    width = vt_ref.shape[0]
    vx_ref[:width, :] = vt_ref[...]
    sub = lax.broadcasted_iota(jnp.int32, (vx_ref.shape[0] - width, vt_ref.shape[1]), 0)
    vx_ref[width:, :] = jnp.where(sub == 0, 1.0, 0.0).astype(vx_ref.dtype)


def _prob_values(s_ref, vx_ref, c, m, acc):
    p = jnp.exp2(s_ref[_key_cols(c), :] - m).astype(jnp.bfloat16)
    part = _dot(vx_ref[:, _key_cols(c)], p)
    return part if acc is None else acc + part


def _dattn_kernel(q_ref, k_ref, vt_ref, lam_ref, gsub_ref, o_ref,
                  s0_ref, s1_ref, s2_ref, m0_ref, m1_ref, m2_ref, vx_ref, qm_ref):
    s_refs, m_refs = (s0_ref, s1_ref, s2_ref), (m0_ref, m1_ref, m2_ref)
    _fill_values_ext(vx_ref, vt_ref)
    width = vt_ref.shape[0]
    lamv = lam_ref[...]
    lam = (jnp.exp(jnp.sum(lamv[0:1] * lamv[1:2], axis=-1, keepdims=True))
           - jnp.exp(jnp.sum(lamv[2:3] * lamv[3:4], axis=-1, keepdims=True)) + LAM_INIT)
    q_all = q_ref[...]
    lane = lax.broadcasted_iota(jnp.int32, q_all.shape, 1)
    qm_ref[0] = jnp.where(lane < DA_HEAD_DIM, q_all, jnp.zeros_like(q_all))
    qm_ref[1] = jnp.where(lane >= DA_HEAD_DIM, q_all, jnp.zeros_like(q_all))

    def score_chunk(t, slot, c):
        k = k_ref[_key_cols(c), :]
        sc0 = lax.dot_general(k, qm_ref[0, _q_rows(t), :], _NT, preferred_element_type=jnp.float32)
        sc1 = lax.dot_general(k, qm_ref[1, _q_rows(t), :], _NT, preferred_element_type=jnp.float32)
        s_refs[slot][0, _key_cols(c), :] = sc0
        s_refs[slot][1, _key_cols(c), :] = sc1
        return _fold_keys(sc0, jnp.maximum), _fold_keys(sc1, jnp.maximum)

    def max_merge(m, mc):
        return mc if m is None else (jnp.maximum(m[0], mc[0]), jnp.maximum(m[1], mc[1]))

    def max_store(slot, m):
        m_refs[slot][0] = jnp.max(m[0], axis=0, keepdims=True)
        m_refs[slot][1] = jnp.max(m[1], axis=0, keepdims=True)

    def max_load(slot):
        return m_refs[slot][0], m_refs[slot][1]

    def value_chunk(slot, c, m, acc):
        acc0, acc1 = (None, None) if acc is None else acc
        return (_prob_values(s_refs[slot].at[0], vx_ref, c, m[0], acc0),
                _prob_values(s_refs[slot].at[1], vx_ref, c, m[1], acc1))

    def write_out(t, acc):
        ox0, ox1 = acc
        ot = (ox0[:width] * (1.0 / ox0[width:width + 1])
              - ox1[:width] * (lam / ox1[width:width + 1]))
        o = ot.T
        o_ref[_q_rows(t), :] = (_rms(o, gsub_ref[...]) * (1.0 - LAM_INIT)).astype(o_ref.dtype)

    _pipelined_tiles(q_ref.shape[0] // Q_TILE, k_ref.shape[0] // KEY_CHUNK,
                     score_chunk, max_merge, max_store, max_load, value_chunk, write_out)


def _dattn(qa, ka, vat, lamv, gsub, batch, seq):
    head = lambda b, h: (b, h)
    return pl.pallas_call(
        _dattn_kernel,
        grid=(batch, DA_HEADS),
        in_specs=[
            pl.BlockSpec((seq, LANES), head),
            pl.BlockSpec((seq, LANES), head),
            pl.BlockSpec((LANES, seq), lambda b, h: (h, b)),
            pl.BlockSpec(lamv.shape, lambda b, h: (0, 0)),
            pl.BlockSpec(gsub.shape, lambda b, h: (0, 0)),
        ],
        out_specs=pl.BlockSpec((seq, LANES), head),
        out_shape=jax.ShapeDtypeStruct(qa.shape, jnp.bfloat16),
        scratch_shapes=[pltpu.VMEM((2, seq, Q_TILE), jnp.float32)] * SCORE_SLOTS
        + [pltpu.VMEM((2, 1, Q_TILE), jnp.float32)] * SCORE_SLOTS
        + [pltpu.VMEM((PV_ROWS, seq), jnp.bfloat16), pltpu.VMEM((2, seq, LANES), jnp.bfloat16)],
        compiler_params=pltpu.CompilerParams(
            dimension_semantics=("parallel", "parallel"), vmem_limit_bytes=VMEM_LIMIT),
        name="dattn",
    )(qa, ka, vat, lamv, gsub)


def _mattn_kernel(q_ref, k_ref, vt_ref, o_ref, s0_ref, s1_ref, s2_ref, m0_ref, m1_ref, m2_ref,
                  vx_ref):
    s_refs, m_refs = (s0_ref, s1_ref, s2_ref), (m0_ref, m1_ref, m2_ref)
    _fill_values_ext(vx_ref, vt_ref)
    width = vt_ref.shape[0]

    def score_chunk(t, slot, c):
        sc = lax.dot_general(k_ref[_key_cols(c), :], q_ref[_q_rows(t), :], _NT,
                             preferred_element_type=jnp.float32)
        s_refs[slot][_key_cols(c), :] = sc
        return _fold_keys(sc, jnp.maximum)

    def max_merge(m, mc):
        return mc if m is None else jnp.maximum(m, mc)

    def max_store(slot, m):
        m_refs[slot][...] = jnp.max(m, axis=0, keepdims=True)

    def max_load(slot):
        return m_refs[slot][...]

    def value_chunk(slot, c, m, acc):
        return _prob_values(s_refs[slot], vx_ref, c, m, acc)

    def write_out(t, ox):
        ot = ox[:width] * (1.0 / ox[width:width + 1])
        o_ref[_q_rows(t), :] = ot.T.astype(o_ref.dtype)

    _pipelined_tiles(q_ref.shape[0] // Q_TILE, k_ref.shape[0] // KEY_CHUNK,
                     score_chunk, max_merge, max_store, max_load, value_chunk, write_out)


def _mattn(qm, km, vmt, batch, seq):
    head = lambda b, h: (b, h)
    return pl.pallas_call(
        _mattn_kernel,
        grid=(batch, MLA_HEADS),
        in_specs=[
            pl.BlockSpec((seq, MLA_QK_PAD), head),
            pl.BlockSpec((seq, MLA_QK_PAD), head),
            pl.BlockSpec((MLA_V, seq), lambda b, h: (h, b)),
        ],
        out_specs=pl.BlockSpec((seq, MLA_V), head),
        out_shape=jax.ShapeDtypeStruct((batch * seq, MLA_HEADS * MLA_V), jnp.bfloat16),
        scratch_shapes=[pltpu.VMEM((seq, Q_TILE), jnp.float32)] * SCORE_SLOTS
        + [pltpu.VMEM((1, Q_TILE), jnp.float32)] * SCORE_SLOTS
        + [pltpu.VMEM((PV_ROWS, seq), jnp.bfloat16)],
        compiler_params=pltpu.CompilerParams(
            dimension_semantics=("parallel", "parallel"), vmem_limit_bytes=VMEM_LIMIT),
        name="mattn",
    )(qm, km, vmt)


def _rows(parts, dtype):
    sub = lax.broadcasted_iota(jnp.int32, (len(parts), parts[0].shape[1]), 0)
    out = jnp.zeros(sub.shape, dtype)
    for k, p in enumerate(parts):
        out = jnp.where(sub == k, p.astype(dtype), out)
    return out


def _merge_kernel(x_ref, oa_ref, ob_ref, gates_ref, woa_ref, wob_ref, wout_ref, gffn_ref,
                  wrt_ref, brt_ref,
                  x1_ref, h2_ref, idx_ref, w_ref, rank_ref, counts_ref, carry_ref, hprev_ref):
    i = pl.program_id(0)

    @pl.when(i == 0)
    def _():
        carry_ref[...] = jnp.zeros_like(carry_ref)
        hprev_ref[...] = jnp.zeros_like(hprev_ref)

    gates = gates_ref[...].astype(jnp.float32)
    half = D_MODEL // 2

    logits = lax.dot_general(wrt_ref[...], hprev_ref[...], _NT, precision=lax.Precision.HIGHEST,
                             preferred_element_type=jnp.float32) + brt_ref[...]
    tm = logits.shape[1]
    sub = lax.broadcasted_iota(jnp.int32, logits.shape, 0)
    vals, idxs, hots = [], [], []
    l = logits

    def pick(l):
        m = jnp.max(l, axis=0, keepdims=True)
        idx = jnp.min(jnp.where(l == m, sub, N_EXPERTS), axis=0, keepdims=True)
        hot = sub == idx
        vals.append(m)
        idxs.append(idx)
        hots.append(hot)
        return jnp.where(hot, -jnp.inf, l)

    ya_lo = gates[:, :half] * _dot(oa_ref[...], woa_ref[:, :half])
    l = pick(l)
    ya_hi = gates[:, half:D_MODEL] * _dot(oa_ref[...], woa_ref[:, half:])
    l = pick(l)
    yb_lo = gates[:, D_MODEL:D_MODEL + half] * _dot(ob_ref[...], wob_ref[:, :half])
    l = pick(l)
    yb_hi = gates[:, D_MODEL + half:] * _dot(ob_ref[...], wob_ref[:, half:])
    l = pick(l)
    merged = jnp.concatenate([ya_lo + yb_lo, ya_hi + yb_hi], axis=1).astype(jnp.bfloat16)

    es = [jnp.exp(v - vals[0]) for v in vals]
    den = es[0] + es[1] + es[2] + es[3]
    w_ref[...] = _rows([e / den for e in es], jnp.float32)
    idx_ref[...] = _rows(idxs, jnp.int32)
    x1_lo = x_ref[:, :half] + _dot(merged, wout_ref[:, :half])

    chosen = (hots[0] | hots[1] | hots[2] | hots[3]).astype(jnp.float32)
    r_i = lax.broadcasted_iota(jnp.int32, (tm, tm), 0)
    c_i = lax.broadcasted_iota(jnp.int32, (tm, tm), 1)
    earlier = (r_i < c_i).astype(jnp.bfloat16)
    prefix = _dot(chosen.astype(jnp.bfloat16), earlier) + carry_ref[...]
    x1_hi = x_ref[:, half:] + _dot(merged, wout_ref[:, half:])
    ranks = [jnp.sum(jnp.where(h, prefix, 0.0), axis=0, keepdims=True) for h in hots]
    rank_ref[...] = _rows(ranks, jnp.int32)
    live = (i > 0).astype(jnp.float32)
    carry = carry_ref[...] + live * jnp.sum(chosen, axis=1, keepdims=True)
    carry_ref[...] = carry
    counts_ref[...] = carry.astype(jnp.int32)

    x1 = jnp.concatenate([x1_lo, x1_hi], axis=1)
    x1_ref[...] = x1
    h2 = _rms(x1, gffn_ref[...])
    h2_ref[...] = h2
    hprev_ref[...] = h2


def _merge(x2, oa, ob, gates, woa, wob, wout, gffn, wrt, brt):
    t = x2.shape[0]
    tm = TOKEN_TILE
    n = t // tm
    row = lambda w: pl.BlockSpec((tm, w), lambda i: (jnp.minimum(i, n - 1), 0))
    col = lambda: pl.BlockSpec((TOP_K, tm), lambda i: (0, jnp.maximum(i - 1, 0)))
    consts = (woa, wob, wout, gffn, wrt, brt)
    return pl.pallas_call(
        _merge_kernel,
        grid=(n + 1,),
        in_specs=[row(D_MODEL), row(D_MODEL), row(D_MODEL), row(2 * D_MODEL)]
        + [_const_spec(c.shape) for c in consts],
        out_specs=[row(D_MODEL), row(D_MODEL), col(), col(), col(),
                   _const_spec((N_EXPERTS, 1))],
        out_shape=[
            jax.ShapeDtypeStruct((t, D_MODEL), jnp.float32),
            jax.ShapeDtypeStruct((t, D_MODEL), jnp.float32),
            jax.ShapeDtypeStruct((TOP_K, t), jnp.int32),
            jax.ShapeDtypeStruct((TOP_K, t), jnp.float32),
            jax.ShapeDtypeStruct((TOP_K, t), jnp.int32),
            jax.ShapeDtypeStruct((N_EXPERTS, 1), jnp.int32),
        ],
        scratch_shapes=[pltpu.VMEM((N_EXPERTS, 1), jnp.float32),
                        pltpu.VMEM((tm, D_MODEL), jnp.float32)],
        compiler_params=pltpu.CompilerParams(
            dimension_semantics=("arbitrary",), vmem_limit_bytes=VMEM_LIMIT),
        name="merge",
    )(x2, oa, ob, gates, *consts)


def _row_copy_wait(src_rows_ref, dst_rows_ref, sem, n):
    for _ in range(n):
        pltpu.make_async_copy(src_rows_ref, dst_rows_ref, sem).wait()


def _dispatch_kernel(dest_ref, pad_lo_ref, pad_hi_ref, nu_ref, h2_ref, xs_ref, zero_ref, sem, zsem):
    i = pl.program_id(0)
    tm = h2_ref.shape[0]
    bm = zero_ref.shape[0]
    n_blocks = xs_ref.shape[0] // bm

    @pl.when(i == 0)
    def _():
        zero_ref[...] = jnp.zeros_like(zero_ref)

        def pad_rows(fn):
            def per_expert(e, c):
                return lax.fori_loop(pad_lo_ref[e], pad_hi_ref[e], fn, c)
            lax.fori_loop(0, N_EXPERTS, per_expert, 0)

        def row_copy(j):
            return pltpu.make_async_copy(zero_ref.at[pl.ds(0, 1)], xs_ref.at[pl.ds(j, 1)], zsem)

        def blk_copy(b):
            return pltpu.make_async_copy(zero_ref, xs_ref.at[pl.ds(pl.multiple_of(b * bm, bm), bm)], zsem)

        def start_row(j, c):
            row_copy(j).start()
            return c

        def wait_row(j, c):
            row_copy(j).wait()
            return c

        def start_blk(b, c):
            blk_copy(b).start()
            return c

        def wait_blk(b, c):
            blk_copy(b).wait()
            return c

        pad_rows(start_row)
        lax.fori_loop(nu_ref[0], n_blocks, start_blk, 0)
        pad_rows(wait_row)
        lax.fori_loop(nu_ref[0], n_blocks, wait_blk, 0)

    def body(r, c):
        base = (i * tm + r) * TOP_K
        for k in range(TOP_K):
            d = dest_ref[base + k]
            pltpu.make_async_copy(h2_ref.at[pl.ds(r, 1)], xs_ref.at[pl.ds(d, 1)], sem).start()
        return c

    lax.fori_loop(0, tm, body, 0, unroll=ROW_DMA_UNROLL)
    _row_copy_wait(h2_ref, xs_ref.at[pl.ds(0, tm)], sem, TOP_K)


def _dispatch(dest, pad_lo, pad_hi, n_used, h2, n_blocks):
    t = h2.shape[0]
    tm = DISPATCH_TILE
    bm = ROW_BLOCK
    return pl.pallas_call(
        _dispatch_kernel,
        grid_spec=pltpu.PrefetchScalarGridSpec(
            num_scalar_prefetch=4,
            grid=(t // tm,),
            in_specs=[pl.BlockSpec((tm, D_MODEL), lambda i, *_: (i, 0))],
            out_specs=pl.BlockSpec(memory_space=pl.ANY),
            scratch_shapes=[pltpu.VMEM((bm, D_MODEL), jnp.float32),
                            pltpu.SemaphoreType.DMA(()), pltpu.SemaphoreType.DMA(())],
        ),
        out_shape=jax.ShapeDtypeStruct((n_blocks * bm, D_MODEL), jnp.float32),
        compiler_params=pltpu.CompilerParams(
            dimension_semantics=("arbitrary",), vmem_limit_bytes=VMEM_LIMIT),
        name="dispatch",
    )(dest, pad_lo, pad_hi, n_used, h2)


def _experts_kernel(be_ref, nu_ref, nxt_ref, grp_ref, xs_ref, wg_ref, bg_ref, wu_ref, bu_ref, wd_ref,
                    bd_ref, ys_ref, stage_ref, wbf_ref, sem):
    b = pl.program_id(0)
    e = be_ref[b]
    slot = lax.rem(grp_ref[b], 2)
    w_hbm = (wg_ref, wu_ref, wd_ref)

    def weight_copies(expert, slot):
        return [pltpu.make_async_copy(w.at[expert], stage_ref.at[slot, j], sem.at[slot, j])
                for j, w in enumerate(w_hbm)]

    @pl.when(b == 0)
    def _():
        for cp in weight_copies(e, slot):
            cp.start()

    @pl.when(jnp.logical_or(b == 0, e != be_ref[jnp.maximum(b - 1, 0)]))
    def _():
        for j, cp in enumerate(weight_copies(e, slot)):
            cp.wait()
            wbf_ref[j] = stage_ref[slot, j].astype(jnp.bfloat16)

        @pl.when(nxt_ref[b] >= 0)
        def _():
            for cp in weight_copies(nxt_ref[b], 1 - slot):
                cp.start()

    @pl.when(b >= nu_ref[0])
    def _():
        ys_ref[...] = jnp.zeros_like(ys_ref)

    @pl.when(b < nu_ref[0])
    def _():
        xb = xs_ref[...].astype(jnp.bfloat16)
        gate = jnp.minimum(_dot(xb, wbf_ref[0]) + bg_ref[0], SWIGLU_LIMIT)
        up = jnp.clip(_dot(xb, wbf_ref[1]) + bu_ref[0], -SWIGLU_LIMIT, SWIGLU_LIMIT)
        act = (up + 1.0) * (gate * jax.nn.sigmoid(SWIGLU_ALPHA * gate))
        ys_ref[...] = _dot(act.astype(jnp.bfloat16), wbf_ref[2]) + bd_ref[0]


def _experts(block_e, n_used, next_e, group, xs, wg, bg, wu, bu, wd, bd, n_blocks):
    bm = ROW_BLOCK
    rows = lambda b, be, nu, *_: (jnp.maximum(jnp.minimum(b, nu[0] - 1), 0), 0)
    hbm = lambda: pl.BlockSpec(memory_space=pl.ANY)
    bspec = lambda: pl.BlockSpec((1, 1, D_FF), lambda b, be, *_: (be[b], 0, 0))
    return pl.pallas_call(
        _experts_kernel,
        grid_spec=pltpu.PrefetchScalarGridSpec(
            num_scalar_prefetch=4,
            grid=(n_blocks,),
            in_specs=[pl.BlockSpec((bm, D_MODEL), rows), hbm(), bspec(), hbm(), bspec(),
                      hbm(), bspec()],
            out_specs=pl.BlockSpec((bm, D_MODEL), lambda b, *_: (b, 0)),
            scratch_shapes=[pltpu.VMEM((2, 3, D_MODEL, D_FF), jnp.float32),
                            pltpu.VMEM((3, D_MODEL, D_FF), jnp.bfloat16),
                            pltpu.SemaphoreType.DMA((2, 3))],
        ),
        out_shape=jax.ShapeDtypeStruct((n_blocks * bm, D_MODEL), jnp.float32),
        compiler_params=pltpu.CompilerParams(
            dimension_semantics=("arbitrary",), vmem_limit_bytes=VMEM_LIMIT),
        name="experts",
    )(block_e, n_used, next_e, group, xs, wg, bg, wu, bu, wd, bd)


def _combine_kernel(dest_ref, ys_ref, x1_ref, w_ref, gfin_ref, o_ref, buf_ref, sem):
    i = pl.program_id(0)
    tm = x1_ref.shape[0]

    def gather(tile):
        slot = lax.rem(tile, 2)

        def body(r, c):
            base = (tile * tm + r) * TOP_K
            for k in range(TOP_K):
                d = dest_ref[base + k]
                pltpu.make_async_copy(ys_ref.at[pl.ds(d, 1)], buf_ref.at[slot, k, pl.ds(r, 1)],
                                      sem.at[slot]).start()
            return c

        lax.fori_loop(0, tm, body, 0, unroll=ROW_DMA_UNROLL)

    @pl.when(i == 0)
    def _():
        gather(i)

    @pl.when(i + 1 < pl.num_programs(0))
    def _():
        gather(i + 1)

    slot = lax.rem(i, 2)
    _row_copy_wait(ys_ref.at[pl.ds(0, tm)], buf_ref.at[slot, 0], sem.at[slot], TOP_K)
    w = w_ref[...]
    y = x1_ref[...]
    for k in range(TOP_K):
        y = y + buf_ref[slot, k] * w[:, k:k + 1]
    o_ref[...] = _rms(y, gfin_ref[...])


def _combine(dest, ys, x1, top_w, gfin):
    t = x1.shape[0]
    tm = TOKEN_TILE
    return pl.pallas_call(
        _combine_kernel,
        grid_spec=pltpu.PrefetchScalarGridSpec(
            num_scalar_prefetch=1,
            grid=(t // tm,),
            in_specs=[pl.BlockSpec(memory_space=pl.ANY),
                      pl.BlockSpec((tm, D_MODEL), lambda i, d: (i, 0)),
                      pl.BlockSpec((tm, TOP_K), lambda i, d: (i, 0)),
                      pl.BlockSpec((1, D_MODEL), lambda i, d: (0, 0))],
            out_specs=pl.BlockSpec((tm, D_MODEL), lambda i, d: (i, 0)),
            scratch_shapes=[pltpu.VMEM((2, TOP_K, tm, D_MODEL), jnp.float32),
                            pltpu.SemaphoreType.DMA((2,))],
        ),
        out_shape=jax.ShapeDtypeStruct((t, D_MODEL), jnp.float32),
        compiler_params=pltpu.CompilerParams(
            dimension_semantics=("arbitrary",), vmem_limit_bytes=VMEM_LIMIT),
        name="combine",
    )(dest, ys, x1, top_w, gfin)


def _rope_lane_table(rot, group):
    half = rot // 2
    inv = ROPE_THETA ** (-jnp.arange(0, rot, 2, dtype=jnp.float32) / rot)
    d = jnp.arange(LANES) % group
    first = jnp.arange(LANES) < (LANES if group < LANES else rot)
    in_lo = (d < half) & first
    in_hi = (d >= half) & (d < rot) & first
    freq = jnp.where(in_lo | in_hi, inv[d % half], 0.0)
    return jnp.stack([freq, in_lo.astype(jnp.float32), in_hi.astype(jnp.float32)]).astype(jnp.float32)


def kernel(x, positions, g_mix, w_in, lam_q1, lam_k1, lam_q2, lam_k2, g_subln, g_q, g_kv, w_uq, w_ukv, w_o_diff, w_o_mla, b_gates, w_out, g_ffn, w_router, b_router, w_gate, b_gate, w_up, b_up, w_down, b_down, g_final):
    batch, seq, d = x.shape
    t = batch * seq
    bf = jnp.bfloat16
    l = 0
    x2 = x.reshape(t, d)
    posf = positions.astype(jnp.float32).reshape(t, 1)

    w = w_in[l]
    win = jnp.concatenate([w[:, :IN_KR_END], jnp.zeros((d, LANES - MLA_ROPE), w.dtype),
                           w[:, IN_KR_END:]], axis=1).astype(bf)
    wuq = jnp.pad(w_uq[l].reshape(MLA_Q_LORA, MLA_HEADS, MLA_NOPE + MLA_ROPE),
                  ((0, 0), (0, 0), (0, MLA_QK_PAD - MLA_NOPE - MLA_ROPE))
                  ).reshape(MLA_Q_LORA, MLA_HEADS * MLA_QK_PAD).astype(bf)
    wukv = w_ukv[l].reshape(MLA_KV_LORA, MLA_HEADS, MLA_NOPE + MLA_V)
    wkn = wukv[:, :, :MLA_NOPE].reshape(MLA_KV_LORA, MLA_HEADS * MLA_NOPE).astype(bf)
    wvt = wukv[:, :, MLA_NOPE:].reshape(MLA_KV_LORA, MLA_HEADS * MLA_V).T.astype(bf)
    wvat = w[:, IN_OFF["va"]:IN_OFF["va"] + IN_W["va"]].T.astype(bf)
    fd = _rope_lane_table(DA_ROT, DA_HEAD_DIM)
    fm = _rope_lane_table(MLA_ROPE, LANES)

    qa, ka, vat, gates, qm, km, vmt = _proj(
        x2, posf, g_mix[l][None], win, b_gates[l][None],
        g_q[l][None], g_kv[l][None], wuq, wkn, wvat, wvt, fd, fm)

    lamv = jnp.stack([lam_q1[l], lam_k1[l], lam_q2[l], lam_k2[l]]).astype(jnp.float32)
    oa = _dattn(qa, ka, vat, lamv, g_subln[l][None], batch, seq)
    ob = _mattn(qm, km, vmt, batch, seq)

    x1, h2, e_idx, top_w, rank, counts = _merge(
        x2, oa, ob, gates, w_o_diff[l].astype(bf), w_o_mla[l].astype(bf), w_out[l].astype(bf),
        g_ffn[l][None], w_router[l].T, b_router[l][:, None])

    bm = ROW_BLOCK
    n_blocks = (t * TOP_K) // bm + N_EXPERTS
    counts = counts[:, 0]
    padded = (counts + bm - 1) // bm * bm
    padded_end = jnp.cumsum(padded)
    padded_start = padded_end - padded
    n_used = (padded_end[-1] // bm).astype(jnp.int32)
    blk = jnp.minimum(jnp.arange(n_blocks, dtype=jnp.int32), n_used - 1)
    block_e = jnp.minimum(jnp.sum(padded_end[None, :] <= (blk * bm)[:, None], axis=1),
                          N_EXPERTS - 1).astype(jnp.int32)
    hot = e_idx[:, :, None] == jnp.arange(N_EXPERTS, dtype=jnp.int32)
    dest = (jnp.sum(jnp.where(hot, padded_start, 0), axis=-1) + rank).T.reshape(-1).astype(jnp.int32)

    n_used = n_used.reshape(1)
    xs = _dispatch(dest, (padded_start + counts).astype(jnp.int32), padded_end.astype(jnp.int32),
                   n_used, h2, n_blocks)
    later = block_e[None, :] > block_e[:, None]
    next_e = jnp.min(jnp.where(later, block_e[None, :], N_EXPERTS), axis=1)
    next_e = jnp.where(next_e == N_EXPERTS, -1, next_e).astype(jnp.int32)
    group = jnp.cumsum(jnp.concatenate([jnp.zeros((1,), jnp.int32),
                                        (block_e[1:] != block_e[:-1]).astype(jnp.int32)]))
    ys = _experts(block_e, n_used, next_e, group.astype(jnp.int32), xs, w_gate[l],
                  b_gate[l][:, None, :], w_up[l], b_up[l][:, None, :], w_down[l],
                  b_down[l][:, None, :], n_blocks)
    out = _combine(dest, ys, x1, top_w.T, g_final[None])
    return out.reshape(batch, seq, d)
```

```python
import math

import jax
import jax.numpy as jnp
from jax import lax
from jax.experimental import pallas as pl
from jax.experimental.pallas import tpu as pltpu

D_MODEL = 1024
ROPE_THETA = 500000.0
NORM_EPS = 1e-6
DA_HEADS = 8
DA_HEAD_DIM = 64
DA_ROT = DA_HEAD_DIM // 4
MLA_HEADS = 8
MLA_Q_LORA = 768
MLA_KV_LORA = 512
MLA_NOPE = 128
MLA_ROPE = 64
MLA_V = 128
N_EXPERTS = 32
TOP_K = 4
D_FF = 1024
SWIGLU_ALPHA = 1.702
SWIGLU_LIMIT = 7.0
LAM_INIT = 0.8 - 0.6 * math.exp(-0.3 * 0)

LANES = 128
SUBLANES = 8
PV_ROWS = LANES + 16
MLA_QK_PAD = 256
TOKEN_TILE = 256
MERGE_TILE = 512
DISPATCH_TILE = 512
Q_TILE = 256
SCORE_SLOTS = 3
KEY_CHUNK = 256
ROW_BLOCK = 256
ROW_DMA_UNROLL = 8
VMEM_LIMIT = 56 * 1024 * 1024

LOG2_E = math.log2(math.e)

IN_W = {"qa": DA_HEADS * 2 * DA_HEAD_DIM, "ka": DA_HEADS * 2 * DA_HEAD_DIM,
        "va": DA_HEADS * 2 * DA_HEAD_DIM, "cq": MLA_Q_LORA,
        "ckr": MLA_KV_LORA + LANES,
        "g": 2 * D_MODEL}
IN_OFF = dict(zip(IN_W, [sum(list(IN_W.values())[:n]) for n in range(len(IN_W))]))
IN_KR_END = IN_OFF["ckr"] + MLA_KV_LORA + MLA_ROPE

_NT = (((1,), (1,)), ((), ()))


def _rms(x, g):
    return x * lax.rsqrt(jnp.mean(x * x, axis=-1, keepdims=True) + NORM_EPS) * g


def _dot(a, b):
    return jnp.dot(a, b, preferred_element_type=jnp.float32)


def _rope_tables(pos, freq, m_lo, m_hi):
    ang = pos * freq
    c = jnp.cos(ang)
    s = jnp.sin(ang)
    return c, -s * m_lo, s * m_hi


def _rope_block(xb, tables, half):
    c, s_lo, s_hi = tables
    return (xb * c + pltpu.roll(xb, LANES - half, 1) * s_lo
            + pltpu.roll(xb, half, 1) * s_hi)


def _proj_kernel(x_ref, pos_ref, gmix_ref, win_ref, bg_ref, gq_ref, gkv_ref, wuq_ref, wkn_ref,
                 wvat_ref, wvt_ref, fd_ref, fm_ref,
                 qa_ref, ka_ref, vat_ref, gates_ref, qm_ref, km_ref, vmt_ref):
    hb = _rms(x_ref[...], gmix_ref[...]).astype(jnp.bfloat16)
    tm = hb.shape[0]

    da_scale = DA_HEAD_DIM ** -0.5 * LOG2_E
    w_cols = lambda name: win_ref[:, IN_OFF[name]:IN_OFF[name] + IN_W[name]]
    mla_scale = (MLA_NOPE + MLA_ROPE) ** -0.5 * LOG2_E

    def in_cols(name, lo, width):
        return win_ref[:, IN_OFF[name] + lo:IN_OFF[name] + lo + width]

    n_groups = 4
    heads = DA_HEADS // n_groups

    cq = _rms(_dot(hb, w_cols("cq")), gq_ref[...]).astype(jnp.bfloat16)
    ck = _dot(hb, w_cols("ckr"))
    td_parts, tmla_parts = [], []
    for g in range(n_groups):
        rows = slice(g * tm // n_groups, (g + 1) * tm // n_groups)
        pos = pos_ref[rows, :]
        td_parts.append(_rope_tables(pos, fd_ref[0:1, :], fd_ref[1:2, :], fd_ref[2:3, :]))
        tmla_parts.append(_rope_tables(pos, fm_ref[0:1, :], fm_ref[1:2, :], fm_ref[2:3, :]))
        cols = slice(g * heads * LANES, (g + 1) * heads * LANES)
        gw = 2 * D_MODEL // n_groups
        gcols = slice(g * gw, (g + 1) * gw)
        gates_ref[:, gcols] = jax.nn.sigmoid(
            _dot(hb, in_cols("g", g * gw, gw)) + bg_ref[:, gcols]).astype(jnp.bfloat16)
        vat_ref[cols, :] = lax.dot_general(wvat_ref[cols, :], hb, _NT,
                                           preferred_element_type=jnp.float32).astype(jnp.bfloat16)
    td = tuple(jnp.concatenate([p[k] for p in td_parts], axis=0) for k in range(3))
    tmla = tuple(jnp.concatenate([p[k] for p in tmla_parts], axis=0) for k in range(3))
    kr = _rope_block(ck[:, MLA_KV_LORA:], tmla, MLA_ROPE // 2).astype(jnp.bfloat16)
    ckv = _rms(ck[:, :MLA_KV_LORA], gkv_ref[...]).astype(jnp.bfloat16)

    for g in range(n_groups):
        zq = _dot(hb, in_cols("qa", g * heads * LANES, heads * LANES))
        for j in range(heads):
            dst = slice((g * heads + j) * LANES, (g * heads + j + 1) * LANES)
            blk = zq[:, j * LANES:(j + 1) * LANES]
            qa_ref[:, dst] = (_rope_block(blk, td, DA_ROT // 2) * da_scale).astype(jnp.bfloat16)

        qm = _dot(cq, wuq_ref[:, g * heads * MLA_QK_PAD:(g + 1) * heads * MLA_QK_PAD])
        for j in range(heads):
            h = g * heads + j
            lo = slice(j * MLA_QK_PAD, j * MLA_QK_PAD + LANES)
            hi = slice(j * MLA_QK_PAD + LANES, (j + 1) * MLA_QK_PAD)
            qm_ref[:, h * MLA_QK_PAD:h * MLA_QK_PAD + LANES] = (qm[:, lo] * mla_scale).astype(jnp.bfloat16)
            qm_ref[:, h * MLA_QK_PAD + LANES:(h + 1) * MLA_QK_PAD] = (
                _rope_block(qm[:, hi], tmla, MLA_ROPE // 2) * mla_scale).astype(jnp.bfloat16)

        zk = _dot(hb, in_cols("ka", g * heads * LANES, heads * LANES))
        for j in range(heads):
            dst = slice((g * heads + j) * LANES, (g * heads + j + 1) * LANES)
            blk = zk[:, j * LANES:(j + 1) * LANES]
            ka_ref[:, dst] = _rope_block(blk, td, DA_ROT // 2).astype(jnp.bfloat16)

        kn = _dot(ckv, wkn_ref[:, g * heads * LANES:(g + 1) * heads * LANES])
        for j in range(heads):
            h = g * heads + j
            km_ref[:, h * MLA_QK_PAD:h * MLA_QK_PAD + LANES] = kn[:, j * LANES:(j + 1) * LANES].astype(jnp.bfloat16)
            km_ref[:, h * MLA_QK_PAD + LANES:(h + 1) * MLA_QK_PAD] = kr

        cols = slice(g * heads * LANES, (g + 1) * heads * LANES)
        vmt_ref[cols, :] = lax.dot_general(wvt_ref[cols, :], ckv, _NT,
                                           preferred_element_type=jnp.float32).astype(jnp.bfloat16)


def _const_spec(shape):
    return pl.BlockSpec(shape, lambda i: (0,) * len(shape))


def _proj(x2, posf, gmix, win, bg, gq, gkv, wuq, wkn, wvat, wvt, fd, fm):
    t = x2.shape[0]
    tm = TOKEN_TILE
    bf = jnp.bfloat16
    row = lambda w: pl.BlockSpec((tm, w), lambda i: (i, 0))
    col = lambda w: pl.BlockSpec((w, tm), lambda i: (0, i))
    consts = (gmix, win, bg, gq, gkv, wuq, wkn, wvat, wvt, fd, fm)
    outs = ((row, D_MODEL), (row, D_MODEL), (col, D_MODEL), (row, 2 * D_MODEL),
            (row, MLA_HEADS * MLA_QK_PAD), (row, MLA_HEADS * MLA_QK_PAD), (col, MLA_HEADS * MLA_V))
    return pl.pallas_call(
        _proj_kernel,
        grid=(t // tm,),
        in_specs=[row(D_MODEL), row(1)] + [_const_spec(c.shape) for c in consts],
        out_specs=[kind(w) for kind, w in outs],
        out_shape=[jax.ShapeDtypeStruct((t, w) if kind is row else (w, t), bf) for kind, w in outs],
        compiler_params=pltpu.CompilerParams(
            dimension_semantics=("parallel",), vmem_limit_bytes=VMEM_LIMIT),
        name="proj",
    )(x2, posf, *consts)


def _pipelined_tiles(n_tiles, n_chunks, score_chunk, max_merge, max_store, max_load,
                     value_chunk, write_out):
    def fused(t_next, slot_next, t_cur, slot_cur):
        m_cur = None if t_cur is None else max_load(slot_cur)
        m_next, acc = None, None
        for c in range(n_chunks):
            if t_next is not None:
                m_next = max_merge(m_next, score_chunk(t_next, slot_next, c))
            if t_cur is not None:
                acc = value_chunk(slot_cur, c, m_cur, acc)
        if t_next is not None:
            max_store(slot_next, m_next)
        if t_cur is not None:
            write_out(t_cur, acc)

    fused(0, 0, None, None)
    fused(1, 1, None, None)
    n_triples = (n_tiles - 2) // SCORE_SLOTS

    def triple(j, carry):
        t = SCORE_SLOTS * j
        for i in range(SCORE_SLOTS):
            fused(t + i + 2, (i + 2) % SCORE_SLOTS, t + i, i)
        return carry

    lax.fori_loop(0, n_triples, triple, 0)
    for t in range(SCORE_SLOTS * n_triples, n_tiles):
        ahead = t + 2 if t + 2 < n_tiles else None
        fused(ahead, (t + 2) % SCORE_SLOTS, t, t % SCORE_SLOTS)


def _q_rows(t):
    if isinstance(t, int):
        return pl.ds(t * Q_TILE, Q_TILE)
    return pl.ds(pl.multiple_of(t * Q_TILE, Q_TILE), Q_TILE)


def _key_cols(c):
    return slice(c * KEY_CHUNK, (c + 1) * KEY_CHUNK)


def _fold_keys(x, op):
    out = x[:SUBLANES]
    for j in range(1, x.shape[0] // SUBLANES):
        out = op(out, x[j * SUBLANES:(j + 1) * SUBLANES])
    return out


def _fill_values_ext(vx_ref, vt_ref):
    width = vt_ref.shape[0]
    vx_ref[:width, :] = vt_ref[...]
    sub = lax.broadcasted_iota(jnp.int32, (vx_ref.shape[0] - width, vt_ref.shape[1]), 0)
    vx_ref[width:, :] = jnp.where(sub == 0, 1.0, 0.0).astype(vx_ref.dtype)


def _prob_values(s_ref, vx_ref, c, m, acc):
    p = jnp.exp2(s_ref[_key_cols(c), :] - m).astype(jnp.bfloat16)
    part = _dot(vx_ref[:, _key_cols(c)], p)
    return part if acc is None else acc + part


def _dattn_kernel(q_ref, k_ref, vt_ref, lam_ref, gsub_ref, o_ref,
                  s0_ref, s1_ref, s2_ref, m0_ref, m1_ref, m2_ref, vx_ref, qm_ref):
    s_refs, m_refs = (s0_ref, s1_ref, s2_ref), (m0_ref, m1_ref, m2_ref)
    _fill_values_ext(vx_ref, vt_ref)
    width = vt_ref.shape[0]
    lamv = lam_ref[...]
    lam = (jnp.exp(jnp.sum(lamv[0:1] * lamv[1:2], axis=-1, keepdims=True))
           - jnp.exp(jnp.sum(lamv[2:3] * lamv[3:4], axis=-1, keepdims=True)) + LAM_INIT)
    q_all = q_ref[...]
    lane = lax.broadcasted_iota(jnp.int32, q_all.shape, 1)
    qm_ref[0] = jnp.where(lane < DA_HEAD_DIM, q_all, jnp.zeros_like(q_all))
    qm_ref[1] = jnp.where(lane >= DA_HEAD_DIM, q_all, jnp.zeros_like(q_all))

    def score_chunk(t, slot, c):
        k = k_ref[_key_cols(c), :]
        sc0 = lax.dot_general(k, qm_ref[0, _q_rows(t), :], _NT, preferred_element_type=jnp.float32)
        sc1 = lax.dot_general(k, qm_ref[1, _q_rows(t), :], _NT, preferred_element_type=jnp.float32)
        s_refs[slot][0, _key_cols(c), :] = sc0
        s_refs[slot][1, _key_cols(c), :] = sc1
        return _fold_keys(sc0, jnp.maximum), _fold_keys(sc1, jnp.maximum)

    def max_merge(m, mc):
        return mc if m is None else (jnp.maximum(m[0], mc[0]), jnp.maximum(m[1], mc[1]))

    def max_store(slot, m):
        m_refs[slot][0] = jnp.max(m[0], axis=0, keepdims=True)
        m_refs[slot][1] = jnp.max(m[1], axis=0, keepdims=True)

    def max_load(slot):
        return m_refs[slot][0], m_refs[slot][1]

    def value_chunk(slot, c, m, acc):
        acc0, acc1 = (None, None) if acc is None else acc
        return (_prob_values(s_refs[slot].at[0], vx_ref, c, m[0], acc0),
                _prob_values(s_refs[slot].at[1], vx_ref, c, m[1], acc1))

    def write_out(t, acc):
        ox0, ox1 = acc
        ot = (ox0[:width] * (1.0 / ox0[width:width + 1])
              - ox1[:width] * (lam / ox1[width:width + 1]))
        o = ot.T
        o_ref[_q_rows(t), :] = (_rms(o, gsub_ref[...]) * (1.0 - LAM_INIT)).astype(o_ref.dtype)

    _pipelined_tiles(q_ref.shape[0] // Q_TILE, k_ref.shape[0] // KEY_CHUNK,
                     score_chunk, max_merge, max_store, max_load, value_chunk, write_out)


def _dattn(qa, ka, vat, lamv, gsub, batch, seq):
    head = lambda b, h: (b, h)
    return pl.pallas_call(
        _dattn_kernel,
        grid=(batch, DA_HEADS),
        in_specs=[
            pl.BlockSpec((seq, LANES), head),
            pl.BlockSpec((seq, LANES), head),
            pl.BlockSpec((LANES, seq), lambda b, h: (h, b)),
            pl.BlockSpec(lamv.shape, lambda b, h: (0, 0)),
            pl.BlockSpec(gsub.shape, lambda b, h: (0, 0)),
        ],
        out_specs=pl.BlockSpec((seq, LANES), head),
        out_shape=jax.ShapeDtypeStruct(qa.shape, jnp.bfloat16),
        scratch_shapes=[pltpu.VMEM((2, seq, Q_TILE), jnp.float32)] * SCORE_SLOTS
        + [pltpu.VMEM((2, 1, Q_TILE), jnp.float32)] * SCORE_SLOTS
        + [pltpu.VMEM((PV_ROWS, seq), jnp.bfloat16), pltpu.VMEM((2, seq, LANES), jnp.bfloat16)],
        compiler_params=pltpu.CompilerParams(
            dimension_semantics=("parallel", "parallel"), vmem_limit_bytes=VMEM_LIMIT),
        name="dattn",
    )(qa, ka, vat, lamv, gsub)


def _mattn_kernel(q_ref, k_ref, vt_ref, o_ref, s0_ref, s1_ref, s2_ref, m0_ref, m1_ref, m2_ref,
                  vx_ref):
    s_refs, m_refs = (s0_ref, s1_ref, s2_ref), (m0_ref, m1_ref, m2_ref)
    _fill_values_ext(vx_ref, vt_ref)
    width = vt_ref.shape[0]

    def score_chunk(t, slot, c):
        sc = lax.dot_general(k_ref[_key_cols(c), :], q_ref[_q_rows(t), :], _NT,
                             preferred_element_type=jnp.float32)
        s_refs[slot][_key_cols(c), :] = sc
        return _fold_keys(sc, jnp.maximum)

    def max_merge(m, mc):
        return mc if m is None else jnp.maximum(m, mc)

    def max_store(slot, m):
        m_refs[slot][...] = jnp.max(m, axis=0, keepdims=True)

    def max_load(slot):
        return m_refs[slot][...]

    def value_chunk(slot, c, m, acc):
        return _prob_values(s_refs[slot], vx_ref, c, m, acc)

    def write_out(t, ox):
        ot = ox[:width] * (1.0 / ox[width:width + 1])
        o_ref[_q_rows(t), :] = ot.T.astype(o_ref.dtype)

    _pipelined_tiles(q_ref.shape[0] // Q_TILE, k_ref.shape[0] // KEY_CHUNK,
                     score_chunk, max_merge, max_store, max_load, value_chunk, write_out)


def _mattn(qm, km, vmt, batch, seq):
    head = lambda b, h: (b, h)
    return pl.pallas_call(
        _mattn_kernel,
        grid=(batch, MLA_HEADS),
        in_specs=[
            pl.BlockSpec((seq, MLA_QK_PAD), head),
            pl.BlockSpec((seq, MLA_QK_PAD), head),
            pl.BlockSpec((MLA_V, seq), lambda b, h: (h, b)),
        ],
        out_specs=pl.BlockSpec((seq, MLA_V), head),
        out_shape=jax.ShapeDtypeStruct((batch * seq, MLA_HEADS * MLA_V), jnp.bfloat16),
        scratch_shapes=[pltpu.VMEM((seq, Q_TILE), jnp.float32)] * SCORE_SLOTS
        + [pltpu.VMEM((1, Q_TILE), jnp.float32)] * SCORE_SLOTS
        + [pltpu.VMEM((PV_ROWS, seq), jnp.bfloat16)],
        compiler_params=pltpu.CompilerParams(
            dimension_semantics=("parallel", "parallel"), vmem_limit_bytes=VMEM_LIMIT),
        name="mattn",
    )(qm, km, vmt)


def _rows(parts, dtype):
    sub = lax.broadcasted_iota(jnp.int32, (len(parts), parts[0].shape[1]), 0)
    out = jnp.zeros(sub.shape, dtype)
    for k, p in enumerate(parts):
        out = jnp.where(sub == k, p.astype(dtype), out)
    return out


def _merge_kernel(x_ref, oa_ref, ob_ref, gates_ref, woa_ref, wob_ref, wout_ref, gffn_ref,
                  wrt_ref, brt_ref,
                  x1_ref, h2_ref, idx_ref, w_ref, rank_ref, counts_ref, carry_ref, hprev_ref):
    i = pl.program_id(0)

    @pl.when(i == 0)
    def _():
        carry_ref[...] = jnp.zeros_like(carry_ref)
        hprev_ref[...] = jnp.zeros_like(hprev_ref)

    gates = gates_ref[...].astype(jnp.float32)
    half = D_MODEL // 2

    logits = lax.dot_general(wrt_ref[...], hprev_ref[...], _NT, precision=lax.Precision.HIGHEST,
                             preferred_element_type=jnp.float32) + brt_ref[...]
    tm = logits.shape[1]
    sub = lax.broadcasted_iota(jnp.int32, logits.shape, 0)
    vals, idxs, hots = [], [], []
    l = logits

    def pick(l):
        m = jnp.max(l, axis=0, keepdims=True)
        idx = jnp.min(jnp.where(l == m, sub, N_EXPERTS), axis=0, keepdims=True)
        hot = sub == idx
        vals.append(m)
        idxs.append(idx)
        hots.append(hot)
        return jnp.where(hot, -jnp.inf, l)

    ya_lo = gates[:, :half] * _dot(oa_ref[...], woa_ref[:, :half])
    l = pick(l)
    ya_hi = gates[:, half:D_MODEL] * _dot(oa_ref[...], woa_ref[:, half:])
    l = pick(l)
    yb_lo = gates[:, D_MODEL:D_MODEL + half] * _dot(ob_ref[...], wob_ref[:, :half])
    l = pick(l)
    yb_hi = gates[:, D_MODEL + half:] * _dot(ob_ref[...], wob_ref[:, half:])
    l = pick(l)
    merged = jnp.concatenate([ya_lo + yb_lo, ya_hi + yb_hi], axis=1).astype(jnp.bfloat16)

    es = [jnp.exp(v - vals[0]) for v in vals]
    den = es[0] + es[1] + es[2] + es[3]
    w_ref[...] = _rows([e / den for e in es], jnp.float32)
    idx_ref[...] = _rows(idxs, jnp.int32)
    x1_lo = x_ref[:, :half] + _dot(merged, wout_ref[:, :half])

    chosen = (hots[0] | hots[1] | hots[2] | hots[3]).astype(jnp.float32)
    r_i = lax.broadcasted_iota(jnp.int32, (tm, tm), 0)
    c_i = lax.broadcasted_iota(jnp.int32, (tm, tm), 1)
    earlier = (r_i < c_i).astype(jnp.bfloat16)
    prefix = _dot(chosen.astype(jnp.bfloat16), earlier) + carry_ref[...]
    x1_hi = x_ref[:, half:] + _dot(merged, wout_ref[:, half:])
    ranks = [jnp.sum(jnp.where(h, prefix, 0.0), axis=0, keepdims=True) for h in hots]
    rank_ref[...] = _rows(ranks, jnp.int32)
    live = (i > 0).astype(jnp.float32)
    carry = carry_ref[...] + live * jnp.sum(chosen, axis=1, keepdims=True)
    carry_ref[...] = carry
    counts_ref[...] = carry.astype(jnp.int32)

    x1 = jnp.concatenate([x1_lo, x1_hi], axis=1)
    x1_ref[...] = x1
    h2 = _rms(x1, gffn_ref[...])
    h2_ref[...] = h2
    hprev_ref[...] = h2


def _merge(x2, oa, ob, gates, woa, wob, wout, gffn, wrt, brt):
    t = x2.shape[0]
    tm = MERGE_TILE
    n = t // tm
    row = lambda w: pl.BlockSpec((tm, w), lambda i: (jnp.minimum(i, n - 1), 0))
    col = lambda: pl.BlockSpec((TOP_K, tm), lambda i: (0, jnp.maximum(i - 1, 0)))
    consts = (woa, wob, wout, gffn, wrt, brt)
    return pl.pallas_call(
        _merge_kernel,
        grid=(n + 1,),
        in_specs=[row(D_MODEL), row(D_MODEL), row(D_MODEL), row(2 * D_MODEL)]
        + [_const_spec(c.shape) for c in consts],
        out_specs=[row(D_MODEL), row(D_MODEL), col(), col(), col(),
                   _const_spec((N_EXPERTS, 1))],
        out_shape=[
            jax.ShapeDtypeStruct((t, D_MODEL), jnp.float32),
            jax.ShapeDtypeStruct((t, D_MODEL), jnp.float32),
            jax.ShapeDtypeStruct((TOP_K, t), jnp.int32),
            jax.ShapeDtypeStruct((TOP_K, t), jnp.float32),
            jax.ShapeDtypeStruct((TOP_K, t), jnp.int32),
            jax.ShapeDtypeStruct((N_EXPERTS, 1), jnp.int32),
        ],
        scratch_shapes=[pltpu.VMEM((N_EXPERTS, 1), jnp.float32),
                        pltpu.VMEM((tm, D_MODEL), jnp.float32)],
        compiler_params=pltpu.CompilerParams(
            dimension_semantics=("arbitrary",), vmem_limit_bytes=VMEM_LIMIT),
        name="merge",
    )(x2, oa, ob, gates, *consts)


def _row_copy_wait(src_rows_ref, dst_rows_ref, sem, n):
    for _ in range(n):
        pltpu.make_async_copy(src_rows_ref, dst_rows_ref, sem).wait()


def _dispatch_kernel(dest_ref, pad_lo_ref, pad_hi_ref, nu_ref, h2_ref, xs_ref, zero_ref, sem, zsem):
    i = pl.program_id(0)
    tm = h2_ref.shape[0]
    bm = zero_ref.shape[0]
    n_blocks = xs_ref.shape[0] // bm

    @pl.when(i == 0)
    def _():
        zero_ref[...] = jnp.zeros_like(zero_ref)

        def pad_rows(fn):
            def per_expert(e, c):
                return lax.fori_loop(pad_lo_ref[e], pad_hi_ref[e], fn, c)
            lax.fori_loop(0, N_EXPERTS, per_expert, 0)

        def row_copy(j):
            return pltpu.make_async_copy(zero_ref.at[pl.ds(0, 1)], xs_ref.at[pl.ds(j, 1)], zsem)

        def blk_copy(b):
            return pltpu.make_async_copy(zero_ref, xs_ref.at[pl.ds(pl.multiple_of(b * bm, bm), bm)], zsem)

        def start_row(j, c):
            row_copy(j).start()
            return c

        def wait_row(j, c):
            row_copy(j).wait()
            return c

        def start_blk(b, c):
            blk_copy(b).start()
            return c

        def wait_blk(b, c):
            blk_copy(b).wait()
            return c

        pad_rows(start_row)
        lax.fori_loop(nu_ref[0], n_blocks, start_blk, 0)
        pad_rows(wait_row)
        lax.fori_loop(nu_ref[0], n_blocks, wait_blk, 0)

    def body(r, c):
        base = (i * tm + r) * TOP_K
        for k in range(TOP_K):
            d = dest_ref[base + k]
            pltpu.make_async_copy(h2_ref.at[pl.ds(r, 1)], xs_ref.at[pl.ds(d, 1)], sem).start()
        return c

    lax.fori_loop(0, tm, body, 0, unroll=ROW_DMA_UNROLL)
    _row_copy_wait(h2_ref, xs_ref.at[pl.ds(0, tm)], sem, TOP_K)


def _dispatch(dest, pad_lo, pad_hi, n_used, h2, n_blocks):
    t = h2.shape[0]
    tm = DISPATCH_TILE
    bm = ROW_BLOCK
    return pl.pallas_call(
        _dispatch_kernel,
        grid_spec=pltpu.PrefetchScalarGridSpec(
            num_scalar_prefetch=4,
            grid=(t // tm,),
            in_specs=[pl.BlockSpec((tm, D_MODEL), lambda i, *_: (i, 0))],
            out_specs=pl.BlockSpec(memory_space=pl.ANY),
            scratch_shapes=[pltpu.VMEM((bm, D_MODEL), jnp.float32),
                            pltpu.SemaphoreType.DMA(()), pltpu.SemaphoreType.DMA(())],
        ),
        out_shape=jax.ShapeDtypeStruct((n_blocks * bm, D_MODEL), jnp.float32),
        compiler_params=pltpu.CompilerParams(
            dimension_semantics=("arbitrary",), vmem_limit_bytes=VMEM_LIMIT),
        name="dispatch",
    )(dest, pad_lo, pad_hi, n_used, h2)


def _experts_kernel(be_ref, nu_ref, nxt_ref, grp_ref, xs_ref, wg_ref, bg_ref, wu_ref, bu_ref, wd_ref,
                    bd_ref, ys_ref, stage_ref, wbf_ref, sem):
    b = pl.program_id(0)
    e = be_ref[b]
    slot = lax.rem(grp_ref[b], 2)
    w_hbm = (wg_ref, wu_ref, wd_ref)

    def weight_copies(expert, slot):
        return [pltpu.make_async_copy(w.at[expert], stage_ref.at[slot, j], sem.at[slot, j])
                for j, w in enumerate(w_hbm)]

    @pl.when(b == 0)
    def _():
        for cp in weight_copies(e, slot):
            cp.start()

    @pl.when(jnp.logical_or(b == 0, e != be_ref[jnp.maximum(b - 1, 0)]))
    def _():
        for j, cp in enumerate(weight_copies(e, slot)):
            cp.wait()
            wbf_ref[j] = stage_ref[slot, j].astype(jnp.bfloat16)

        @pl.when(nxt_ref[b] >= 0)
        def _():
            for cp in weight_copies(nxt_ref[b], 1 - slot):
                cp.start()

    @pl.when(b >= nu_ref[0])
    def _():
        ys_ref[...] = jnp.zeros_like(ys_ref)

    @pl.when(b < nu_ref[0])
    def _():
        xb = xs_ref[...].astype(jnp.bfloat16)
        gate = jnp.minimum(_dot(xb, wbf_ref[0]) + bg_ref[0], SWIGLU_LIMIT)
        up = jnp.clip(_dot(xb, wbf_ref[1]) + bu_ref[0], -SWIGLU_LIMIT, SWIGLU_LIMIT)
        act = (up + 1.0) * (gate * jax.nn.sigmoid(SWIGLU_ALPHA * gate))
        ys_ref[...] = _dot(act.astype(jnp.bfloat16), wbf_ref[2]) + bd_ref[0]


def _experts(block_e, n_used, next_e, group, xs, wg, bg, wu, bu, wd, bd, n_blocks):
    bm = ROW_BLOCK
    rows = lambda b, be, nu, *_: (jnp.maximum(jnp.minimum(b, nu[0] - 1), 0), 0)
    hbm = lambda: pl.BlockSpec(memory_space=pl.ANY)
    bspec = lambda: pl.BlockSpec((1, 1, D_FF), lambda b, be, *_: (be[b], 0, 0))
    return pl.pallas_call(
        _experts_kernel,
        grid_spec=pltpu.PrefetchScalarGridSpec(
            num_scalar_prefetch=4,
            grid=(n_blocks,),
            in_specs=[pl.BlockSpec((bm, D_MODEL), rows), hbm(), bspec(), hbm(), bspec(),
                      hbm(), bspec()],
            out_specs=pl.BlockSpec((bm, D_MODEL), lambda b, *_: (b, 0)),
            scratch_shapes=[pltpu.VMEM((2, 3, D_MODEL, D_FF), jnp.float32),
                            pltpu.VMEM((3, D_MODEL, D_FF), jnp.bfloat16),
                            pltpu.SemaphoreType.DMA((2, 3))],
        ),
        out_shape=jax.ShapeDtypeStruct((n_blocks * bm, D_MODEL), jnp.float32),
        compiler_params=pltpu.CompilerParams(
            dimension_semantics=("arbitrary",), vmem_limit_bytes=VMEM_LIMIT),
        name="experts",
    )(block_e, n_used, next_e, group, xs, wg, bg, wu, bu, wd, bd)


def _combine_kernel(dest_ref, ys_ref, x1_ref, w_ref, gfin_ref, o_ref, buf_ref, sem):
    i = pl.program_id(0)
    tm = x1_ref.shape[0]

    def gather(tile):
        slot = lax.rem(tile, 2)

        def body(r, c):
            base = (tile * tm + r) * TOP_K
            for k in range(TOP_K):
                d = dest_ref[base + k]
                pltpu.make_async_copy(ys_ref.at[pl.ds(d, 1)], buf_ref.at[slot, k, pl.ds(r, 1)],
                                      sem.at[slot]).start()
            return c

        lax.fori_loop(0, tm, body, 0, unroll=ROW_DMA_UNROLL)

    @pl.when(i == 0)
    def _():
        gather(i)

    @pl.when(i + 1 < pl.num_programs(0))
    def _():
        gather(i + 1)

    slot = lax.rem(i, 2)
    _row_copy_wait(ys_ref.at[pl.ds(0, tm)], buf_ref.at[slot, 0], sem.at[slot], TOP_K)
    w = w_ref[...]
    y = x1_ref[...]
    for k in range(TOP_K):
        y = y + buf_ref[slot, k] * w[:, k:k + 1]
    o_ref[...] = _rms(y, gfin_ref[...])


def _combine(dest, ys, x1, top_w, gfin):
    t = x1.shape[0]
    tm = TOKEN_TILE
    return pl.pallas_call(
        _combine_kernel,
        grid_spec=pltpu.PrefetchScalarGridSpec(
            num_scalar_prefetch=1,
            grid=(t // tm,),
            in_specs=[pl.BlockSpec(memory_space=pl.ANY),
                      pl.BlockSpec((tm, D_MODEL), lambda i, d: (i, 0)),
                      pl.BlockSpec((tm, TOP_K), lambda i, d: (i, 0)),
                      pl.BlockSpec((1, D_MODEL), lambda i, d: (0, 0))],
            out_specs=pl.BlockSpec((tm, D_MODEL), lambda i, d: (i, 0)),
            scratch_shapes=[pltpu.VMEM((2, TOP_K, tm, D_MODEL), jnp.float32),
                            pltpu.SemaphoreType.DMA((2,))],
        ),
        out_shape=jax.ShapeDtypeStruct((t, D_MODEL), jnp.float32),
        compiler_params=pltpu.CompilerParams(
            dimension_semantics=("arbitrary",), vmem_limit_bytes=VMEM_LIMIT),
        name="combine",
    )(dest, ys, x1, top_w, gfin)


def _rope_lane_table(rot, group):
    half = rot // 2
    inv = ROPE_THETA ** (-jnp.arange(0, rot, 2, dtype=jnp.float32) / rot)
    d = jnp.arange(LANES) % group
    first = jnp.arange(LANES) < (LANES if group < LANES else rot)
    in_lo = (d < half) & first
    in_hi = (d >= half) & (d < rot) & first
    freq = jnp.where(in_lo | in_hi, inv[d % half], 0.0)
    return jnp.stack([freq, in_lo.astype(jnp.float32), in_hi.astype(jnp.float32)]).astype(jnp.float32)


def kernel(x, positions, g_mix, w_in, lam_q1, lam_k1, lam_q2, lam_k2, g_subln, g_q, g_kv, w_uq, w_ukv, w_o_diff, w_o_mla, b_gates, w_out, g_ffn, w_router, b_router, w_gate, b_gate, w_up, b_up, w_down, b_down, g_final):
    batch, seq, d = x.shape
    t = batch * seq
    bf = jnp.bfloat16
    l = 0
    x2 = x.reshape(t, d)
    posf = positions.astype(jnp.float32).reshape(t, 1)

    w = w_in[l]
    win = jnp.concatenate([w[:, :IN_KR_END], jnp.zeros((d, LANES - MLA_ROPE), w.dtype),
                           w[:, IN_KR_END:]], axis=1).astype(bf)
    wuq = jnp.pad(w_uq[l].reshape(MLA_Q_LORA, MLA_HEADS, MLA_NOPE + MLA_ROPE),
                  ((0, 0), (0, 0), (0, MLA_QK_PAD - MLA_NOPE - MLA_ROPE))
                  ).reshape(MLA_Q_LORA, MLA_HEADS * MLA_QK_PAD).astype(bf)
    wukv = w_ukv[l].reshape(MLA_KV_LORA, MLA_HEADS, MLA_NOPE + MLA_V)
    wkn = wukv[:, :, :MLA_NOPE].reshape(MLA_KV_LORA, MLA_HEADS * MLA_NOPE).astype(bf)
    wvt = wukv[:, :, MLA_NOPE:].reshape(MLA_KV_LORA, MLA_HEADS * MLA_V).T.astype(bf)
    wvat = w[:, IN_OFF["va"]:IN_OFF["va"] + IN_W["va"]].T.astype(bf)
    fd = _rope_lane_table(DA_ROT, DA_HEAD_DIM)
    fm = _rope_lane_table(MLA_ROPE, LANES)

    qa, ka, vat, gates, qm, km, vmt = _proj(
        x2, posf, g_mix[l][None], win, b_gates[l][None],
        g_q[l][None], g_kv[l][None], wuq, wkn, wvat, wvt, fd, fm)

    lamv = jnp.stack([lam_q1[l], lam_k1[l], lam_q2[l], lam_k2[l]]).astype(jnp.float32)
    oa = _dattn(qa, ka, vat, lamv, g_subln[l][None], batch, seq)
    ob = _mattn(qm, km, vmt, batch, seq)

    x1, h2, e_idx, top_w, rank, counts = _merge(
        x2, oa, ob, gates, w_o_diff[l].astype(bf), w_o_mla[l].astype(bf), w_out[l].astype(bf),
        g_ffn[l][None], w_router[l].T, b_router[l][:, None])

    bm = ROW_BLOCK
    n_blocks = (t * TOP_K) // bm + N_EXPERTS
    counts = counts[:, 0]
    padded = (counts + bm - 1) // bm * bm
    padded_end = jnp.cumsum(padded)
    padded_start = padded_end - padded
    n_used = (padded_end[-1] // bm).astype(jnp.int32)
    blk = jnp.minimum(jnp.arange(n_blocks, dtype=jnp.int32), n_used - 1)
    block_e = jnp.minimum(jnp.sum(padded_end[None, :] <= (blk * bm)[:, None], axis=1),
                          N_EXPERTS - 1).astype(jnp.int32)
    hot = e_idx[:, :, None] == jnp.arange(N_EXPERTS, dtype=jnp.int32)
    dest = (jnp.sum(jnp.where(hot, padded_start, 0), axis=-1) + rank).T.reshape(-1).astype(jnp.int32)

    n_used = n_used.reshape(1)
    xs = _dispatch(dest, (padded_start + counts).astype(jnp.int32), padded_end.astype(jnp.int32),
                   n_used, h2, n_blocks)
    later = block_e[None, :] > block_e[:, None]
    next_e = jnp.min(jnp.where(later, block_e[None, :], N_EXPERTS), axis=1)
    next_e = jnp.where(next_e == N_EXPERTS, -1, next_e).astype(jnp.int32)
    group = jnp.cumsum(jnp.concatenate([jnp.zeros((1,), jnp.int32),
                                        (block_e[1:] != block_e[:-1]).astype(jnp.int32)]))
    ys = _experts(block_e, n_used, next_e, group.astype(jnp.int32), xs, w_gate[l],
                  b_gate[l][:, None, :], w_up[l], b_up[l][:, None, :], w_down[l],
                  b_down[l][:, None, :], n_blocks)
    out = _combine(dest, ys, x1, top_w.T, g_final[None])
    return out.reshape(batch, seq, d)
```

```python
import math

import jax
import jax.numpy as jnp
from jax import lax
from jax.experimental import pallas as pl
from jax.experimental.pallas import tpu as pltpu

D_MODEL = 1024
ROPE_THETA = 500000.0
NORM_EPS = 1e-6
DA_HEADS = 8
DA_HEAD_DIM = 64
DA_ROT = DA_HEAD_DIM // 4
MLA_HEADS = 8
MLA_Q_LORA = 768
MLA_KV_LORA = 512
MLA_NOPE = 128
MLA_ROPE = 64
MLA_V = 128
N_EXPERTS = 32
TOP_K = 4
D_FF = 1024
SWIGLU_ALPHA = 1.702
SWIGLU_LIMIT = 7.0
LAM_INIT = 0.8 - 0.6 * math.exp(-0.3 * 0)

LANES = 128
SUBLANES = 8
PV_ROWS = LANES + 16
MLA_QK_PAD = 256
TOKEN_TILE = 256
DISPATCH_TILE = 512
Q_TILE = 256
SCORE_SLOTS = 3
KEY_CHUNK = 256
ROW_BLOCK = 256
BLOCKS_PER_STEP = 4
ROW_DMA_UNROLL = 8
VMEM_LIMIT = 56 * 1024 * 1024

LOG2_E = math.log2(math.e)

IN_W = {"qa": DA_HEADS * 2 * DA_HEAD_DIM, "ka": DA_HEADS * 2 * DA_HEAD_DIM,
        "va": DA_HEADS * 2 * DA_HEAD_DIM, "cq": MLA_Q_LORA,
        "ckr": MLA_KV_LORA + LANES,
        "g": 2 * D_MODEL}
IN_OFF = dict(zip(IN_W, [sum(list(IN_W.values())[:n]) for n in range(len(IN_W))]))
IN_KR_END = IN_OFF["ckr"] + MLA_KV_LORA + MLA_ROPE

_NT = (((1,), (1,)), ((), ()))


def _rms(x, g):
    return x * lax.rsqrt(jnp.mean(x * x, axis=-1, keepdims=True) + NORM_EPS) * g


def _dot(a, b):
    return jnp.dot(a, b, preferred_element_type=jnp.float32)


def _rope_tables(pos, freq, m_lo, m_hi):
    ang = pos * freq
    c = jnp.cos(ang)
    s = jnp.sin(ang)
    return c, -s * m_lo, s * m_hi


def _rope_block(xb, tables, half):
    c, s_lo, s_hi = tables
    return (xb * c + pltpu.roll(xb, LANES - half, 1) * s_lo
            + pltpu.roll(xb, half, 1) * s_hi)


def _proj_kernel(x_ref, pos_ref, gmix_ref, win_ref, bg_ref, gq_ref, gkv_ref, wuq_ref, wkn_ref,
                 wvat_ref, wvt_ref, fd_ref, fm_ref,
                 qa_ref, ka_ref, vat_ref, gates_ref, qm_ref, km_ref, vmt_ref):
    hb = _rms(x_ref[...], gmix_ref[...]).astype(jnp.bfloat16)
    tm = hb.shape[0]

    da_scale = DA_HEAD_DIM ** -0.5 * LOG2_E
    w_cols = lambda name: win_ref[:, IN_OFF[name]:IN_OFF[name] + IN_W[name]]
    mla_scale = (MLA_NOPE + MLA_ROPE) ** -0.5 * LOG2_E

    def in_cols(name, lo, width):
        return win_ref[:, IN_OFF[name] + lo:IN_OFF[name] + lo + width]

    n_groups = 4
    heads = DA_HEADS // n_groups

    cq = _rms(_dot(hb, w_cols("cq")), gq_ref[...]).astype(jnp.bfloat16)
    ck = _dot(hb, w_cols("ckr"))
    td_parts, tmla_parts = [], []
    for g in range(n_groups):
        rows = slice(g * tm // n_groups, (g + 1) * tm // n_groups)
        pos = pos_ref[rows, :]
        td_parts.append(_rope_tables(pos, fd_ref[0:1, :], fd_ref[1:2, :], fd_ref[2:3, :]))
        tmla_parts.append(_rope_tables(pos, fm_ref[0:1, :], fm_ref[1:2, :], fm_ref[2:3, :]))
        cols = slice(g * heads * LANES, (g + 1) * heads * LANES)
        gw = 2 * D_MODEL // n_groups
        gcols = slice(g * gw, (g + 1) * gw)
        gates_ref[:, gcols] = jax.nn.sigmoid(
            _dot(hb, in_cols("g", g * gw, gw)) + bg_ref[:, gcols]).astype(jnp.bfloat16)
        vat_ref[cols, :] = lax.dot_general(wvat_ref[cols, :], hb, _NT,
                                           preferred_element_type=jnp.float32).astype(jnp.bfloat16)
    td = tuple(jnp.concatenate([p[k] for p in td_parts], axis=0) for k in range(3))
    tmla = tuple(jnp.concatenate([p[k] for p in tmla_parts], axis=0) for k in range(3))
    kr = _rope_block(ck[:, MLA_KV_LORA:], tmla, MLA_ROPE // 2).astype(jnp.bfloat16)
    ckv = _rms(ck[:, :MLA_KV_LORA], gkv_ref[...]).astype(jnp.bfloat16)

    for g in range(n_groups):
        zq = _dot(hb, in_cols("qa", g * heads * LANES, heads * LANES))
        for j in range(heads):
            dst = slice((g * heads + j) * LANES, (g * heads + j + 1) * LANES)
            blk = zq[:, j * LANES:(j + 1) * LANES]
            qa_ref[:, dst] = (_rope_block(blk, td, DA_ROT // 2) * da_scale).astype(jnp.bfloat16)

        qm = _dot(cq, wuq_ref[:, g * heads * MLA_QK_PAD:(g + 1) * heads * MLA_QK_PAD])
        for j in range(heads):
            h = g * heads + j
            lo = slice(j * MLA_QK_PAD, j * MLA_QK_PAD + LANES)
            hi = slice(j * MLA_QK_PAD + LANES, (j + 1) * MLA_QK_PAD)
            qm_ref[:, h * MLA_QK_PAD:h * MLA_QK_PAD + LANES] = (qm[:, lo] * mla_scale).astype(jnp.bfloat16)
            qm_ref[:, h * MLA_QK_PAD + LANES:(h + 1) * MLA_QK_PAD] = (
                _rope_block(qm[:, hi], tmla, MLA_ROPE // 2) * mla_scale).astype(jnp.bfloat16)

        zk = _dot(hb, in_cols("ka", g * heads * LANES, heads * LANES))
        for j in range(heads):
            dst = slice((g * heads + j) * LANES, (g * heads + j + 1) * LANES)
            blk = zk[:, j * LANES:(j + 1) * LANES]
            ka_ref[:, dst] = _rope_block(blk, td, DA_ROT // 2).astype(jnp.bfloat16)

        kn = _dot(ckv, wkn_ref[:, g * heads * LANES:(g + 1) * heads * LANES])
        for j in range(heads):
            h = g * heads + j
            km_ref[:, h * MLA_QK_PAD:h * MLA_QK_PAD + LANES] = kn[:, j * LANES:(j + 1) * LANES].astype(jnp.bfloat16)
            km_ref[:, h * MLA_QK_PAD + LANES:(h + 1) * MLA_QK_PAD] = kr

        cols = slice(g * heads * LANES, (g + 1) * heads * LANES)
        vmt_ref[cols, :] = lax.dot_general(wvt_ref[cols, :], ckv, _NT,
                                           preferred_element_type=jnp.float32).astype(jnp.bfloat16)


def _const_spec(shape):
    return pl.BlockSpec(shape, lambda i: (0,) * len(shape))


def _proj(x2, posf, gmix, win, bg, gq, gkv, wuq, wkn, wvat, wvt, fd, fm):
    t = x2.shape[0]
    tm = TOKEN_TILE
    bf = jnp.bfloat16
    row = lambda w: pl.BlockSpec((tm, w), lambda i: (i, 0))
    col = lambda w: pl.BlockSpec((w, tm), lambda i: (0, i))
    consts = (gmix, win, bg, gq, gkv, wuq, wkn, wvat, wvt, fd, fm)
    outs = ((row, D_MODEL), (row, D_MODEL), (col, D_MODEL), (row, 2 * D_MODEL),
            (row, MLA_HEADS * MLA_QK_PAD), (row, MLA_HEADS * MLA_QK_PAD), (col, MLA_HEADS * MLA_V))
    return pl.pallas_call(
        _proj_kernel,
        grid=(t // tm,),
        in_specs=[row(D_MODEL), row(1)] + [_const_spec(c.shape) for c in consts],
        out_specs=[kind(w) for kind, w in outs],
        out_shape=[jax.ShapeDtypeStruct((t, w) if kind is row else (w, t), bf) for kind, w in outs],
        compiler_params=pltpu.CompilerParams(
            dimension_semantics=("parallel",), vmem_limit_bytes=VMEM_LIMIT),
        name="proj",
    )(x2, posf, *consts)


def _pipelined_tiles(n_tiles, n_chunks, score_chunk, max_merge, max_store, max_load,
                     value_chunk, write_out):
    def fused(t_next, slot_next, t_cur, slot_cur):
        m_cur = None if t_cur is None else max_load(slot_cur)
        m_next, acc = None, None
        for c in range(n_chunks):
            if t_next is not None:
                m_next = max_merge(m_next, score_chunk(t_next, slot_next, c))
            if t_cur is not None:
                acc = value_chunk(slot_cur, c, m_cur, acc)
        if t_next is not None:
            max_store(slot_next, m_next)
        if t_cur is not None:
            write_out(t_cur, acc)

    fused(0, 0, None, None)
    fused(1, 1, None, None)
    n_triples = (n_tiles - 2) // SCORE_SLOTS

    def triple(j, carry):
        t = SCORE_SLOTS * j
        for i in range(SCORE_SLOTS):
            fused(t + i + 2, (i + 2) % SCORE_SLOTS, t + i, i)
        return carry

    lax.fori_loop(0, n_triples, triple, 0)
    for t in range(SCORE_SLOTS * n_triples, n_tiles):
        ahead = t + 2 if t + 2 < n_tiles else None
        fused(ahead, (t + 2) % SCORE_SLOTS, t, t % SCORE_SLOTS)


def _q_rows(t):
    if isinstance(t, int):
        return pl.ds(t * Q_TILE, Q_TILE)
    return pl.ds(pl.multiple_of(t * Q_TILE, Q_TILE), Q_TILE)


def _key_cols(c):
    return slice(c * KEY_CHUNK, (c + 1) * KEY_CHUNK)


def _fold_keys(x, op):
    out = x[:SUBLANES]
    for j in range(1, x.shape[0] // SUBLANES):
        out = op(out, x[j * SUBLANES:(j + 1) * SUBLANES])
    return out


def _fill_values_ext(vx_ref, vt_ref):
    width = vt_ref.shape[0]
    vx_ref[:width, :] = vt_ref[...]
    sub = lax.broadcasted_iota(jnp.int32, (vx_ref.shape[0] - width, vt_ref.shape[1]), 0)
    vx_ref[width:, :] = jnp.where(sub == 0, 1.0, 0.0).astype(vx_ref.dtype)


def _prob_values(s_ref, vx_ref, c, m, acc):
    p = jnp.exp2(s_ref[_key_cols(c), :] - m).astype(jnp.bfloat16)
    part = _dot(vx_ref[:, _key_cols(c)], p)
    return part if acc is None else acc + part


def _dattn_kernel(q_ref, k_ref, vt_ref, lam_ref, gsub_ref, o_ref,
                  s0_ref, s1_ref, s2_ref, m0_ref, m1_ref, m2_ref, vx_ref, qm_ref):
    s_refs, m_refs = (s0_ref, s1_ref, s2_ref), (m0_ref, m1_ref, m2_ref)
    _fill_values_ext(vx_ref, vt_ref)
    width = vt_ref.shape[0]
    lamv = lam_ref[...]
    lam = (jnp.exp(jnp.sum(lamv[0:1] * lamv[1:2], axis=-1, keepdims=True))
           - jnp.exp(jnp.sum(lamv[2:3] * lamv[3:4], axis=-1, keepdims=True)) + LAM_INIT)
    q_all = q_ref[...]
    lane = lax.broadcasted_iota(jnp.int32, q_all.shape, 1)
    qm_ref[0] = jnp.where(lane < DA_HEAD_DIM, q_all, jnp.zeros_like(q_all))
    qm_ref[1] = jnp.where(lane >= DA_HEAD_DIM, q_all, jnp.zeros_like(q_all))

    def score_chunk(t, slot, c):
        k = k_ref[_key_cols(c), :]
        sc0 = lax.dot_general(k, qm_ref[0, _q_rows(t), :], _NT, preferred_element_type=jnp.float32)
        sc1 = lax.dot_general(k, qm_ref[1, _q_rows(t), :], _NT, preferred_element_type=jnp.float32)
        s_refs[slot][0, _key_cols(c), :] = sc0
        s_refs[slot][1, _key_cols(c), :] = sc1
        return _fold_keys(sc0, jnp.maximum), _fold_keys(sc1, jnp.maximum)

    def max_merge(m, mc):
        return mc if m is None else (jnp.maximum(m[0], mc[0]), jnp.maximum(m[1], mc[1]))

    def max_store(slot, m):
        m_refs[slot][0] = jnp.max(m[0], axis=0, keepdims=True)
        m_refs[slot][1] = jnp.max(m[1], axis=0, keepdims=True)

    def max_load(slot):
        return m_refs[slot][0], m_refs[slot][1]

    def value_chunk(slot, c, m, acc):
        acc0, acc1 = (None, None) if acc is None else acc
        return (_prob_values(s_refs[slot].at[0], vx_ref, c, m[0], acc0),
                _prob_values(s_refs[slot].at[1], vx_ref, c, m[1], acc1))

    def write_out(t, acc):
        ox0, ox1 = acc
        ot = (ox0[:width] * (1.0 / ox0[width:width + 1])
              - ox1[:width] * (lam / ox1[width:width + 1]))
        o = ot.T
        o_ref[_q_rows(t), :] = (_rms(o, gsub_ref[...]) * (1.0 - LAM_INIT)).astype(o_ref.dtype)

    _pipelined_tiles(q_ref.shape[0] // Q_TILE, k_ref.shape[0] // KEY_CHUNK,
                     score_chunk, max_merge, max_store, max_load, value_chunk, write_out)


def _dattn(qa, ka, vat, lamv, gsub, batch, seq):
    head = lambda b, h: (b, h)
    return pl.pallas_call(
        _dattn_kernel,
        grid=(batch, DA_HEADS),
        in_specs=[
            pl.BlockSpec((seq, LANES), head),
            pl.BlockSpec((seq, LANES), head),
            pl.BlockSpec((LANES, seq), lambda b, h: (h, b)),
            pl.BlockSpec(lamv.shape, lambda b, h: (0, 0)),
            pl.BlockSpec(gsub.shape, lambda b, h: (0, 0)),
        ],
        out_specs=pl.BlockSpec((seq, LANES), head),
        out_shape=jax.ShapeDtypeStruct(qa.shape, jnp.bfloat16),
        scratch_shapes=[pltpu.VMEM((2, seq, Q_TILE), jnp.float32)] * SCORE_SLOTS
        + [pltpu.VMEM((2, 1, Q_TILE), jnp.float32)] * SCORE_SLOTS
        + [pltpu.VMEM((PV_ROWS, seq), jnp.bfloat16), pltpu.VMEM((2, seq, LANES), jnp.bfloat16)],
        compiler_params=pltpu.CompilerParams(
            dimension_semantics=("parallel", "parallel"), vmem_limit_bytes=VMEM_LIMIT),
        name="dattn",
    )(qa, ka, vat, lamv, gsub)


def _mattn_kernel(q_ref, k_ref, vt_ref, o_ref, s0_ref, s1_ref, s2_ref, m0_ref, m1_ref, m2_ref,
                  vx_ref):
    s_refs, m_refs = (s0_ref, s1_ref, s2_ref), (m0_ref, m1_ref, m2_ref)
    _fill_values_ext(vx_ref, vt_ref)
    width = vt_ref.shape[0]

    def score_chunk(t, slot, c):
        sc = lax.dot_general(k_ref[_key_cols(c), :], q_ref[_q_rows(t), :], _NT,
                             preferred_element_type=jnp.float32)
        s_refs[slot][_key_cols(c), :] = sc
        return _fold_keys(sc, jnp.maximum)

    def max_merge(m, mc):
        return mc if m is None else jnp.maximum(m, mc)

    def max_store(slot, m):
        m_refs[slot][...] = jnp.max(m, axis=0, keepdims=True)

    def max_load(slot):
        return m_refs[slot][...]

    def value_chunk(slot, c, m, acc):
        return _prob_values(s_refs[slot], vx_ref, c, m, acc)

    def write_out(t, ox):
        ot = ox[:width] * (1.0 / ox[width:width + 1])
        o_ref[_q_rows(t), :] = ot.T.astype(o_ref.dtype)

    _pipelined_tiles(q_ref.shape[0] // Q_TILE, k_ref.shape[0] // KEY_CHUNK,
                     score_chunk, max_merge, max_store, max_load, value_chunk, write_out)


def _mattn(qm, km, vmt, batch, seq):
    head = lambda b, h: (b, h)
    return pl.pallas_call(
        _mattn_kernel,
        grid=(batch, MLA_HEADS),
        in_specs=[
            pl.BlockSpec((seq, MLA_QK_PAD), head),
            pl.BlockSpec((seq, MLA_QK_PAD), head),
            pl.BlockSpec((MLA_V, seq), lambda b, h: (h, b)),
        ],
        out_specs=pl.BlockSpec((seq, MLA_V), head),
        out_shape=jax.ShapeDtypeStruct((batch * seq, MLA_HEADS * MLA_V), jnp.bfloat16),
        scratch_shapes=[pltpu.VMEM((seq, Q_TILE), jnp.float32)] * SCORE_SLOTS
        + [pltpu.VMEM((1, Q_TILE), jnp.float32)] * SCORE_SLOTS
        + [pltpu.VMEM((PV_ROWS, seq), jnp.bfloat16)],
        compiler_params=pltpu.CompilerParams(
            dimension_semantics=("parallel", "parallel"), vmem_limit_bytes=VMEM_LIMIT),
        name="mattn",
    )(qm, km, vmt)


def _rows(parts, dtype):
    sub = lax.broadcasted_iota(jnp.int32, (len(parts), parts[0].shape[1]), 0)
    out = jnp.zeros(sub.shape, dtype)
    for k, p in enumerate(parts):
        out = jnp.where(sub == k, p.astype(dtype), out)
    return out


def _merge_kernel(x_ref, oa_ref, ob_ref, gates_ref, woa_ref, wob_ref, wout_ref, gffn_ref,
                  wrt_ref, brt_ref,
                  x1_ref, h2_ref, idx_ref, w_ref, rank_ref, counts_ref, carry_ref, hprev_ref):
    i = pl.program_id(0)

    @pl.when(i == 0)
    def _():
        carry_ref[...] = jnp.zeros_like(carry_ref)
        hprev_ref[...] = jnp.zeros_like(hprev_ref)

    gates = gates_ref[...].astype(jnp.float32)
    half = D_MODEL // 2

    logits = lax.dot_general(wrt_ref[...], hprev_ref[...], _NT, precision=lax.Precision.HIGHEST,
                             preferred_element_type=jnp.float32) + brt_ref[...]
    tm = logits.shape[1]
    sub = lax.broadcasted_iota(jnp.int32, logits.shape, 0)
    vals, idxs, hots = [], [], []
    l = logits

    def pick(l):
        m = jnp.max(l, axis=0, keepdims=True)
        idx = jnp.min(jnp.where(l == m, sub, N_EXPERTS), axis=0, keepdims=True)
        hot = sub == idx
        vals.append(m)
        idxs.append(idx)
        hots.append(hot)
        return jnp.where(hot, -jnp.inf, l)

    ya_lo = gates[:, :half] * _dot(oa_ref[...], woa_ref[:, :half])
    l = pick(l)
    ya_hi = gates[:, half:D_MODEL] * _dot(oa_ref[...], woa_ref[:, half:])
    l = pick(l)
    yb_lo = gates[:, D_MODEL:D_MODEL + half] * _dot(ob_ref[...], wob_ref[:, :half])
    l = pick(l)
    yb_hi = gates[:, D_MODEL + half:] * _dot(ob_ref[...], wob_ref[:, half:])
    l = pick(l)
    merged = jnp.concatenate([ya_lo + yb_lo, ya_hi + yb_hi], axis=1).astype(jnp.bfloat16)

    es = [jnp.exp(v - vals[0]) for v in vals]
    den = es[0] + es[1] + es[2] + es[3]
    w_ref[...] = _rows([e / den for e in es], jnp.float32)
    idx_ref[...] = _rows(idxs, jnp.int32)
    x1_lo = x_ref[:, :half] + _dot(merged, wout_ref[:, :half])

    chosen = (hots[0] | hots[1] | hots[2] | hots[3]).astype(jnp.float32)
    r_i = lax.broadcasted_iota(jnp.int32, (tm, tm), 0)
    c_i = lax.broadcasted_iota(jnp.int32, (tm, tm), 1)
    earlier = (r_i < c_i).astype(jnp.bfloat16)
    prefix = _dot(chosen.astype(jnp.bfloat16), earlier) + carry_ref[...]
    x1_hi = x_ref[:, half:] + _dot(merged, wout_ref[:, half:])
    ranks = [jnp.sum(jnp.where(h, prefix, 0.0), axis=0, keepdims=True) for h in hots]
    rank_ref[...] = _rows(ranks, jnp.int32)
    live = (i > 0).astype(jnp.float32)
    carry = carry_ref[...] + live * jnp.sum(chosen, axis=1, keepdims=True)
    carry_ref[...] = carry
    counts_ref[...] = carry.astype(jnp.int32)

    x1 = jnp.concatenate([x1_lo, x1_hi], axis=1)
    x1_ref[...] = x1
    h2 = _rms(x1, gffn_ref[...])
    h2_ref[...] = h2
    hprev_ref[...] = h2


def _merge(x2, oa, ob, gates, woa, wob, wout, gffn, wrt, brt):
    t = x2.shape[0]
    tm = TOKEN_TILE
    n = t // tm
    row = lambda w: pl.BlockSpec((tm, w), lambda i: (jnp.minimum(i, n - 1), 0))
    col = lambda: pl.BlockSpec((TOP_K, tm), lambda i: (0, jnp.maximum(i - 1, 0)))
    consts = (woa, wob, wout, gffn, wrt, brt)
    return pl.pallas_call(
        _merge_kernel,
        grid=(n + 1,),
        in_specs=[row(D_MODEL), row(D_MODEL), row(D_MODEL), row(2 * D_MODEL)]
        + [_const_spec(c.shape) for c in consts],
        out_specs=[row(D_MODEL), row(D_MODEL), col(), col(), col(),
                   _const_spec((N_EXPERTS, 1))],
        out_shape=[
            jax.ShapeDtypeStruct((t, D_MODEL), jnp.float32),
            jax.ShapeDtypeStruct((t, D_MODEL), jnp.float32),
            jax.ShapeDtypeStruct((TOP_K, t), jnp.int32),
            jax.ShapeDtypeStruct((TOP_K, t), jnp.float32),
            jax.ShapeDtypeStruct((TOP_K, t), jnp.int32),
            jax.ShapeDtypeStruct((N_EXPERTS, 1), jnp.int32),
        ],
        scratch_shapes=[pltpu.VMEM((N_EXPERTS, 1), jnp.float32),
                        pltpu.VMEM((tm, D_MODEL), jnp.float32)],
        compiler_params=pltpu.CompilerParams(
            dimension_semantics=("arbitrary",), vmem_limit_bytes=VMEM_LIMIT),
        name="merge",
    )(x2, oa, ob, gates, *consts)


def _row_copy_wait(src_rows_ref, dst_rows_ref, sem, n):
    for _ in range(n):
        pltpu.make_async_copy(src_rows_ref, dst_rows_ref, sem).wait()


def _dispatch_kernel(dest_ref, pad_lo_ref, pad_hi_ref, nu_ref, h2_ref, xs_ref, zero_ref, sem, zsem):
    i = pl.program_id(0)
    tm = h2_ref.shape[0]
    bm = zero_ref.shape[0]
    n_blocks = xs_ref.shape[0] // bm

    @pl.when(i == 0)
    def _():
        zero_ref[...] = jnp.zeros_like(zero_ref)

        def pad_rows(fn):
            def per_expert(e, c):
                return lax.fori_loop(pad_lo_ref[e], pad_hi_ref[e], fn, c)
            lax.fori_loop(0, N_EXPERTS, per_expert, 0)

        def row_copy(j):
            return pltpu.make_async_copy(zero_ref.at[pl.ds(0, 1)], xs_ref.at[pl.ds(j, 1)], zsem)

        def blk_copy(b):
            return pltpu.make_async_copy(zero_ref, xs_ref.at[pl.ds(pl.multiple_of(b * bm, bm), bm)], zsem)

        def start_row(j, c):
            row_copy(j).start()
            return c

        def wait_row(j, c):
            row_copy(j).wait()
            return c

        def start_blk(b, c):
            blk_copy(b).start()
            return c

        def wait_blk(b, c):
            blk_copy(b).wait()
            return c

        pad_rows(start_row)
        lax.fori_loop(nu_ref[0], n_blocks, start_blk, 0)
        pad_rows(wait_row)
        lax.fori_loop(nu_ref[0], n_blocks, wait_blk, 0)

    def body(r, c):
        base = (i * tm + r) * TOP_K
        for k in range(TOP_K):
            d = dest_ref[base + k]
            pltpu.make_async_copy(h2_ref.at[pl.ds(r, 1)], xs_ref.at[pl.ds(d, 1)], sem).start()
        return c

    lax.fori_loop(0, tm, body, 0, unroll=ROW_DMA_UNROLL)
    _row_copy_wait(h2_ref, xs_ref.at[pl.ds(0, tm)], sem, TOP_K)


def _dispatch(dest, pad_lo, pad_hi, n_used, h2, n_blocks):
    t = h2.shape[0]
    tm = DISPATCH_TILE
    bm = ROW_BLOCK
    return pl.pallas_call(
        _dispatch_kernel,
        grid_spec=pltpu.PrefetchScalarGridSpec(
            num_scalar_prefetch=4,
            grid=(t // tm,),
            in_specs=[pl.BlockSpec((tm, D_MODEL), lambda i, *_: (i, 0))],
            out_specs=pl.BlockSpec(memory_space=pl.ANY),
            scratch_shapes=[pltpu.VMEM((bm, D_MODEL), jnp.float32),
                            pltpu.SemaphoreType.DMA(()), pltpu.SemaphoreType.DMA(())],
        ),
        out_shape=jax.ShapeDtypeStruct((n_blocks * bm, D_MODEL), jnp.float32),
        compiler_params=pltpu.CompilerParams(
            dimension_semantics=("arbitrary",), vmem_limit_bytes=VMEM_LIMIT),
        name="dispatch",
    )(dest, pad_lo, pad_hi, n_used, h2)


def _experts_kernel(be_ref, nu_ref, nxt_ref, grp_ref, xs_ref, wg_ref, bg_ref, wu_ref, bu_ref, wd_ref,
                    bd_ref, ys_ref, stage_ref, wbf_ref, sem):
    w_hbm = (wg_ref, wu_ref, wd_ref)
    bm = ROW_BLOCK

    def weight_copies(expert, slot):
        return [pltpu.make_async_copy(w.at[expert], stage_ref.at[slot, j], sem.at[slot, j])
                for j, w in enumerate(w_hbm)]

    def row_block(b, rows):
        e = be_ref[b]
        slot = lax.rem(grp_ref[b], 2)

        @pl.when(b == 0)
        def _():
            for cp in weight_copies(e, slot):
                cp.start()

        @pl.when(jnp.logical_or(b == 0, e != be_ref[jnp.maximum(b - 1, 0)]))
        def _():
            for j, cp in enumerate(weight_copies(e, slot)):
                cp.wait()
                wbf_ref[j] = stage_ref[slot, j].astype(jnp.bfloat16)

            @pl.when(nxt_ref[b] >= 0)
            def _():
                for cp in weight_copies(nxt_ref[b], 1 - slot):
                    cp.start()

        @pl.when(b >= nu_ref[0])
        def _():
            ys_ref[rows, :] = jnp.zeros((bm, ys_ref.shape[1]), ys_ref.dtype)

        @pl.when(b < nu_ref[0])
        def _():
            xb = xs_ref[rows, :].astype(jnp.bfloat16)
            gate = jnp.minimum(_dot(xb, wbf_ref[0]) + bg_ref[e], SWIGLU_LIMIT)
            up = jnp.clip(_dot(xb, wbf_ref[1]) + bu_ref[e], -SWIGLU_LIMIT, SWIGLU_LIMIT)
            act = (up + 1.0) * (gate * jax.nn.sigmoid(SWIGLU_ALPHA * gate))
            ys_ref[rows, :] = _dot(act.astype(jnp.bfloat16), wbf_ref[2]) + bd_ref[e]

    for j in range(BLOCKS_PER_STEP):
        row_block(pl.program_id(0) * BLOCKS_PER_STEP + j, slice(j * bm, (j + 1) * bm))


def _experts(block_e, n_used, next_e, group, xs, wg, bg, wu, bu, wd, bd, n_blocks):
    rows_per_step = ROW_BLOCK * BLOCKS_PER_STEP
    rows = lambda s, be, nu, *_: (jnp.maximum(jnp.minimum(s, (nu[0] - 1) // BLOCKS_PER_STEP), 0), 0)
    hbm = lambda: pl.BlockSpec(memory_space=pl.ANY)
    bias = lambda: pl.BlockSpec((N_EXPERTS, 1, D_FF), lambda s, *_: (0, 0, 0))
    return pl.pallas_call(
        _experts_kernel,
        grid_spec=pltpu.PrefetchScalarGridSpec(
            num_scalar_prefetch=4,
            grid=(n_blocks // BLOCKS_PER_STEP,),
            in_specs=[pl.BlockSpec((rows_per_step, D_MODEL), rows), hbm(), bias(), hbm(), bias(),
                      hbm(), bias()],
            out_specs=pl.BlockSpec((rows_per_step, D_MODEL), lambda s, *_: (s, 0)),
            scratch_shapes=[pltpu.VMEM((2, 3, D_MODEL, D_FF), jnp.float32),
                            pltpu.VMEM((3, D_MODEL, D_FF), jnp.bfloat16),
                            pltpu.SemaphoreType.DMA((2, 3))],
        ),
        out_shape=jax.ShapeDtypeStruct((n_blocks * ROW_BLOCK, D_MODEL), jnp.float32),
        compiler_params=pltpu.CompilerParams(
            dimension_semantics=("arbitrary",), vmem_limit_bytes=VMEM_LIMIT),
        name="experts",
    )(block_e, n_used, next_e, group, xs, wg, bg, wu, bu, wd, bd)


def _combine_kernel(dest_ref, ys_ref, x1_ref, w_ref, gfin_ref, o_ref, buf_ref, sem):
    i = pl.program_id(0)
    tm = x1_ref.shape[0]

    def gather(tile):
        slot = lax.rem(tile, 2)

        def body(r, c):
            base = (tile * tm + r) * TOP_K
            for k in range(TOP_K):
                d = dest_ref[base + k]
                pltpu.make_async_copy(ys_ref.at[pl.ds(d, 1)], buf_ref.at[slot, k, pl.ds(r, 1)],
                                      sem.at[slot]).start()
            return c

        lax.fori_loop(0, tm, body, 0, unroll=ROW_DMA_UNROLL)

    @pl.when(i == 0)
    def _():
        gather(i)

    @pl.when(i + 1 < pl.num_programs(0))
    def _():
        gather(i + 1)

    slot = lax.rem(i, 2)
    _row_copy_wait(ys_ref.at[pl.ds(0, tm)], buf_ref.at[slot, 0], sem.at[slot], TOP_K)
    w = w_ref[...]
    y = x1_ref[...]
    for k in range(TOP_K):
        y = y + buf_ref[slot, k] * w[:, k:k + 1]
    o_ref[...] = _rms(y, gfin_ref[...])


def _combine(dest, ys, x1, top_w, gfin):
    t = x1.shape[0]
    tm = TOKEN_TILE
    return pl.pallas_call(
        _combine_kernel,
        grid_spec=pltpu.PrefetchScalarGridSpec(
            num_scalar_prefetch=1,
            grid=(t // tm,),
            in_specs=[pl.BlockSpec(memory_space=pl.ANY),
                      pl.BlockSpec((tm, D_MODEL), lambda i, d: (i, 0)),
                      pl.BlockSpec((tm, TOP_K), lambda i, d: (i, 0)),
                      pl.BlockSpec((1, D_MODEL), lambda i, d: (0, 0))],
            out_specs=pl.BlockSpec((tm, D_MODEL), lambda i, d: (i, 0)),
            scratch_shapes=[pltpu.VMEM((2, TOP_K, tm, D_MODEL), jnp.float32),
                            pltpu.SemaphoreType.DMA((2,))],
        ),
        out_shape=jax.ShapeDtypeStruct((t, D_MODEL), jnp.float32),
        compiler_params=pltpu.CompilerParams(
            dimension_semantics=("arbitrary",), vmem_limit_bytes=VMEM_LIMIT),
        name="combine",
    )(dest, ys, x1, top_w, gfin)


def _rope_lane_table(rot, group):
    half = rot // 2
    inv = ROPE_THETA ** (-jnp.arange(0, rot, 2, dtype=jnp.float32) / rot)
    d = jnp.arange(LANES) % group
    first = jnp.arange(LANES) < (LANES if group < LANES else rot)
    in_lo = (d < half) & first
    in_hi = (d >= half) & (d < rot) & first
    freq = jnp.where(in_lo | in_hi, inv[d % half], 0.0)
    return jnp.stack([freq, in_lo.astype(jnp.float32), in_hi.astype(jnp.float32)]).astype(jnp.float32)


def kernel(x, positions, g_mix, w_in, lam_q1, lam_k1, lam_q2, lam_k2, g_subln, g_q, g_kv, w_uq, w_ukv, w_o_diff, w_o_mla, b_gates, w_out, g_ffn, w_router, b_router, w_gate, b_gate, w_up, b_up, w_down, b_down, g_final):
    batch, seq, d = x.shape
    t = batch * seq
    bf = jnp.bfloat16
    l = 0
    x2 = x.reshape(t, d)
    posf = positions.astype(jnp.float32).reshape(t, 1)

    w = w_in[l]
    win = jnp.concatenate([w[:, :IN_KR_END], jnp.zeros((d, LANES - MLA_ROPE), w.dtype),
                           w[:, IN_KR_END:]], axis=1).astype(bf)
    wuq = jnp.pad(w_uq[l].reshape(MLA_Q_LORA, MLA_HEADS, MLA_NOPE + MLA_ROPE),
                  ((0, 0), (0, 0), (0, MLA_QK_PAD - MLA_NOPE - MLA_ROPE))
                  ).reshape(MLA_Q_LORA, MLA_HEADS * MLA_QK_PAD).astype(bf)
    wukv = w_ukv[l].reshape(MLA_KV_LORA, MLA_HEADS, MLA_NOPE + MLA_V)
    wkn = wukv[:, :, :MLA_NOPE].reshape(MLA_KV_LORA, MLA_HEADS * MLA_NOPE).astype(bf)
    wvt = wukv[:, :, MLA_NOPE:].reshape(MLA_KV_LORA, MLA_HEADS * MLA_V).T.astype(bf)
    wvat = w[:, IN_OFF["va"]:IN_OFF["va"] + IN_W["va"]].T.astype(bf)
    fd = _rope_lane_table(DA_ROT, DA_HEAD_DIM)
    fm = _rope_lane_table(MLA_ROPE, LANES)

    qa, ka, vat, gates, qm, km, vmt = _proj(
        x2, posf, g_mix[l][None], win, b_gates[l][None],
        g_q[l][None], g_kv[l][None], wuq, wkn, wvat, wvt, fd, fm)

    lamv = jnp.stack([lam_q1[l], lam_k1[l], lam_q2[l], lam_k2[l]]).astype(jnp.float32)
    oa = _dattn(qa, ka, vat, lamv, g_subln[l][None], batch, seq)
    ob = _mattn(qm, km, vmt, batch, seq)

    x1, h2, e_idx, top_w, rank, counts = _merge(
        x2, oa, ob, gates, w_o_diff[l].astype(bf), w_o_mla[l].astype(bf), w_out[l].astype(bf),
        g_ffn[l][None], w_router[l].T, b_router[l][:, None])

    bm = ROW_BLOCK
    n_blocks = (t * TOP_K) // bm + N_EXPERTS
    counts = counts[:, 0]
    padded = (counts + bm - 1) // bm * bm
    padded_end = jnp.cumsum(padded)
    padded_start = padded_end - padded
    n_used = (padded_end[-1] // bm).astype(jnp.int32)
    blk = jnp.minimum(jnp.arange(n_blocks, dtype=jnp.int32), n_used - 1)
    block_e = jnp.minimum(jnp.sum(padded_end[None, :] <= (blk * bm)[:, None], axis=1),
                          N_EXPERTS - 1).astype(jnp.int32)
    hot = e_idx[:, :, None] == jnp.arange(N_EXPERTS, dtype=jnp.int32)
    dest = (jnp.sum(jnp.where(hot, padded_start, 0), axis=-1) + rank).T.reshape(-1).astype(jnp.int32)

    n_used = n_used.reshape(1)
    xs = _dispatch(dest, (padded_start + counts).astype(jnp.int32), padded_end.astype(jnp.int32),
                   n_used, h2, n_blocks)
    later = block_e[None, :] > block_e[:, None]
    next_e = jnp.min(jnp.where(later, block_e[None, :], N_EXPERTS), axis=1)
    next_e = jnp.where(next_e == N_EXPERTS, -1, next_e).astype(jnp.int32)
    group = jnp.cumsum(jnp.concatenate([jnp.zeros((1,), jnp.int32),
                                        (block_e[1:] != block_e[:-1]).astype(jnp.int32)]))
    ys = _experts(block_e, n_used, next_e, group.astype(jnp.int32), xs, w_gate[l],
                  b_gate[l][:, None, :], w_up[l], b_up[l][:, None, :], w_down[l],
                  b_down[l][:, None, :], n_blocks)
    out = _combine(dest, ys, x1, top_w.T, g_final[None])
    return out.reshape(batch, seq, d)
```

```python
import math

import jax
import jax.numpy as jnp
from jax import lax
from jax.experimental import pallas as pl
from jax.experimental.pallas import tpu as pltpu

D_MODEL = 1024
ROPE_THETA = 500000.0
NORM_EPS = 1e-6
DA_HEADS = 8
DA_HEAD_DIM = 64
DA_ROT = DA_HEAD_DIM // 4
MLA_HEADS = 8
MLA_Q_LORA = 768
MLA_KV_LORA = 512
MLA_NOPE = 128
MLA_ROPE = 64
MLA_V = 128
N_EXPERTS = 32
TOP_K = 4
D_FF = 1024
SWIGLU_ALPHA = 1.702
SWIGLU_LIMIT = 7.0
LAM_INIT = 0.8 - 0.6 * math.exp(-0.3 * 0)

LANES = 128
SUBLANES = 8
PV_ROWS = LANES + 16
MLA_QK_PAD = 256
TOKEN_TILE = 256
DISPATCH_TILE = 512
Q_TILE = 256
SCORE_SLOTS = 3
KEY_CHUNK = 256
ROW_BLOCK = 256
BLOCKS_PER_STEP = 4
ROW_DMA_UNROLL = 8
VMEM_LIMIT = 56 * 1024 * 1024

LOG2_E = math.log2(math.e)

IN_W = {"qa": DA_HEADS * 2 * DA_HEAD_DIM, "ka": DA_HEADS * 2 * DA_HEAD_DIM,
        "va": DA_HEADS * 2 * DA_HEAD_DIM, "cq": MLA_Q_LORA, "ckv": MLA_KV_LORA}
IN_OFF = dict(zip(IN_W, [sum(list(IN_W.values())[:n]) for n in range(len(IN_W))]))
IN_KR = sum(IN_W.values())
IN_GATES = IN_KR + MLA_ROPE

_NT = (((1,), (1,)), ((), ()))


def _rms(x, g):
    return x * lax.rsqrt(jnp.mean(x * x, axis=-1, keepdims=True) + NORM_EPS) * g


def _dot(a, b):
    return jnp.dot(a, b, preferred_element_type=jnp.float32)


def _rope_tables(pos, freq, m_lo, m_hi):
    ang = pos * freq
    c = jnp.cos(ang)
    s = jnp.sin(ang)
    return c, -s * m_lo, s * m_hi


def _rope_block(xb, tables, half):
    c, s_lo, s_hi = tables
    return (xb * c + pltpu.roll(xb, LANES - half, 1) * s_lo
            + pltpu.roll(xb, half, 1) * s_hi)


def _proj_kernel(x_ref, pos_ref, gmix_ref, win_ref, wkr_ref, wgate_ref, bg_ref, gq_ref, gkv_ref,
                 wuq_ref, wkn_ref,
                 wvat_ref, wvt_ref, fd_ref, fm_ref,
                 qa_ref, ka_ref, vat_ref, gates_ref, qm_ref, km_ref, vmt_ref):
    hb = _rms(x_ref[...], gmix_ref[...]).astype(jnp.bfloat16)
    tm = hb.shape[0]

    da_scale = DA_HEAD_DIM ** -0.5 * LOG2_E
    w_cols = lambda name: win_ref[:, IN_OFF[name]:IN_OFF[name] + IN_W[name]]
    mla_scale = (MLA_NOPE + MLA_ROPE) ** -0.5 * LOG2_E

    def in_cols(name, lo, width):
        return win_ref[:, IN_OFF[name] + lo:IN_OFF[name] + lo + width]

    n_groups = 4
    heads = DA_HEADS // n_groups

    cq = _rms(_dot(hb, w_cols("cq")), gq_ref[...]).astype(jnp.bfloat16)
    ck = _dot(hb, w_cols("ckv"))
    kr_in = _dot(hb, wkr_ref[...])
    td_parts, tmla_parts = [], []
    for g in range(n_groups):
        rows = slice(g * tm // n_groups, (g + 1) * tm // n_groups)
        pos = pos_ref[rows, :]
        td_parts.append(_rope_tables(pos, fd_ref[0:1, :], fd_ref[1:2, :], fd_ref[2:3, :]))
        tmla_parts.append(_rope_tables(pos, fm_ref[0:1, :], fm_ref[1:2, :], fm_ref[2:3, :]))
        cols = slice(g * heads * LANES, (g + 1) * heads * LANES)
        gw = 2 * D_MODEL // n_groups
        gcols = slice(g * gw, (g + 1) * gw)
        gates_ref[:, gcols] = jax.nn.sigmoid(
            _dot(hb, wgate_ref[:, gcols]) + bg_ref[:, gcols]).astype(jnp.bfloat16)
        vat_ref[cols, :] = lax.dot_general(wvat_ref[cols, :], hb, _NT,
                                           preferred_element_type=jnp.float32).astype(jnp.bfloat16)
    td = tuple(jnp.concatenate([p[k] for p in td_parts], axis=0) for k in range(3))
    tmla = tuple(jnp.concatenate([p[k] for p in tmla_parts], axis=0) for k in range(3))
    kr = _rope_block(kr_in, tmla, MLA_ROPE // 2).astype(jnp.bfloat16)
    ckv = _rms(ck, gkv_ref[...]).astype(jnp.bfloat16)

    for g in range(n_groups):
        zq = _dot(hb, in_cols("qa", g * heads * LANES, heads * LANES))
        for j in range(heads):
            dst = slice((g * heads + j) * LANES, (g * heads + j + 1) * LANES)
            blk = zq[:, j * LANES:(j + 1) * LANES]
            qa_ref[:, dst] = (_rope_block(blk, td, DA_ROT // 2) * da_scale).astype(jnp.bfloat16)

        qm = _dot(cq, wuq_ref[:, g * heads * MLA_QK_PAD:(g + 1) * heads * MLA_QK_PAD])
        for j in range(heads):
            h = g * heads + j
            lo = slice(j * MLA_QK_PAD, j * MLA_QK_PAD + LANES)
            hi = slice(j * MLA_QK_PAD + LANES, (j + 1) * MLA_QK_PAD)
            qm_ref[:, h * MLA_QK_PAD:h * MLA_QK_PAD + LANES] = (qm[:, lo] * mla_scale).astype(jnp.bfloat16)
            qm_ref[:, h * MLA_QK_PAD + LANES:(h + 1) * MLA_QK_PAD] = (
                _rope_block(qm[:, hi], tmla, MLA_ROPE // 2) * mla_scale).astype(jnp.bfloat16)

        zk = _dot(hb, in_cols("ka", g * heads * LANES, heads * LANES))
        for j in range(heads):
            dst = slice((g * heads + j) * LANES, (g * heads + j + 1) * LANES)
            blk = zk[:, j * LANES:(j + 1) * LANES]
            ka_ref[:, dst] = _rope_block(blk, td, DA_ROT // 2).astype(jnp.bfloat16)

        kn = _dot(ckv, wkn_ref[:, g * heads * LANES:(g + 1) * heads * LANES])
        for j in range(heads):
            h = g * heads + j
            km_ref[:, h * MLA_QK_PAD:h * MLA_QK_PAD + LANES] = kn[:, j * LANES:(j + 1) * LANES].astype(jnp.bfloat16)
            km_ref[:, h * MLA_QK_PAD + LANES:(h + 1) * MLA_QK_PAD] = kr

        cols = slice(g * heads * LANES, (g + 1) * heads * LANES)
        vmt_ref[cols, :] = lax.dot_general(wvt_ref[cols, :], ckv, _NT,
                                           preferred_element_type=jnp.float32).astype(jnp.bfloat16)


def _const_spec(shape):
    return pl.BlockSpec(shape, lambda i: (0,) * len(shape))


def _proj(x2, posf, gmix, win, wkr, wgate, bg, gq, gkv, wuq, wkn, wvat, wvt, fd, fm):
    t = x2.shape[0]
    tm = TOKEN_TILE
    bf = jnp.bfloat16
    row = lambda w: pl.BlockSpec((tm, w), lambda i: (i, 0))
    col = lambda w: pl.BlockSpec((w, tm), lambda i: (0, i))
    consts = (gmix, win, wkr, wgate, bg, gq, gkv, wuq, wkn, wvat, wvt, fd, fm)
    outs = ((row, D_MODEL), (row, D_MODEL), (col, D_MODEL), (row, 2 * D_MODEL),
            (row, MLA_HEADS * MLA_QK_PAD), (row, MLA_HEADS * MLA_QK_PAD), (col, MLA_HEADS * MLA_V))
    return pl.pallas_call(
        _proj_kernel,
        grid=(t // tm,),
        in_specs=[row(D_MODEL), row(1)] + [_const_spec(c.shape) for c in consts],
        out_specs=[kind(w) for kind, w in outs],
        out_shape=[jax.ShapeDtypeStruct((t, w) if kind is row else (w, t), bf) for kind, w in outs],
        compiler_params=pltpu.CompilerParams(
            dimension_semantics=("parallel",), vmem_limit_bytes=VMEM_LIMIT),
        name="proj",
    )(x2, posf, *consts)


def _pipelined_tiles(n_tiles, n_chunks, score_chunk, max_merge, max_store, max_load,
                     value_chunk, write_out):
    def fused(t_next, slot_next, t_cur, slot_cur):
        m_cur = None if t_cur is None else max_load(slot_cur)
        m_next, acc = None, None
        for c in range(n_chunks):
            if t_next is not None:
                m_next = max_merge(m_next, score_chunk(t_next, slot_next, c))
            if t_cur is not None:
                acc = value_chunk(slot_cur, c, m_cur, acc)
        if t_next is not None:
            max_store(slot_next, m_next)
        if t_cur is not None:
            write_out(t_cur, acc)

    fused(0, 0, None, None)
    fused(1, 1, None, None)
    n_triples = (n_tiles - 2) // SCORE_SLOTS

    def triple(j, carry):
        t = SCORE_SLOTS * j
        for i in range(SCORE_SLOTS):
            fused(t + i + 2, (i + 2) % SCORE_SLOTS, t + i, i)
        return carry

    lax.fori_loop(0, n_triples, triple, 0)
    for t in range(SCORE_SLOTS * n_triples, n_tiles):
        ahead = t + 2 if t + 2 < n_tiles else None
        fused(ahead, (t + 2) % SCORE_SLOTS, t, t % SCORE_SLOTS)


def _q_rows(t):
    if isinstance(t, int):
        return pl.ds(t * Q_TILE, Q_TILE)
    return pl.ds(pl.multiple_of(t * Q_TILE, Q_TILE), Q_TILE)


def _key_cols(c):
    return slice(c * KEY_CHUNK, (c + 1) * KEY_CHUNK)


def _fold_keys(x, op):
    out = x[:SUBLANES]
    for j in range(1, x.shape[0] // SUBLANES):
        out = op(out, x[j * SUBLANES:(j + 1) * SUBLANES])
    return out


def _fill_values_ext(vx_ref, vt_ref):
    width = vt_ref.shape[0]
    vx_ref[:width, :] = vt_ref[...]
    sub = lax.broadcasted_iota(jnp.int32, (vx_ref.shape[0] - width, vt_ref.shape[1]), 0)
    vx_ref[width:, :] = jnp.where(sub == 0, 1.0, 0.0).astype(vx_ref.dtype)


def _prob_values(s_ref, vx_ref, c, m, acc):
    p = jnp.exp2(s_ref[_key_cols(c), :] - m).astype(jnp.bfloat16)
    part = _dot(vx_ref[:, _key_cols(c)], p)
    return part if acc is None else acc + part


def _dattn_kernel(q_ref, k_ref, vt_ref, lam_ref, gsub_ref, o_ref,
                  s0_ref, s1_ref, s2_ref, m0_ref, m1_ref, m2_ref, vx_ref, qm_ref):
    s_refs, m_refs = (s0_ref, s1_ref, s2_ref), (m0_ref, m1_ref, m2_ref)
    _fill_values_ext(vx_ref, vt_ref)
    width = vt_ref.shape[0]
    lamv = lam_ref[...]
    lam = (jnp.exp(jnp.sum(lamv[0:1] * lamv[1:2], axis=-1, keepdims=True))
           - jnp.exp(jnp.sum(lamv[2:3] * lamv[3:4], axis=-1, keepdims=True)) + LAM_INIT)
    q_all = q_ref[...]
    lane = lax.broadcasted_iota(jnp.int32, q_all.shape, 1)
    qm_ref[0] = jnp.where(lane < DA_HEAD_DIM, q_all, jnp.zeros_like(q_all))
    qm_ref[1] = jnp.where(lane >= DA_HEAD_DIM, q_all, jnp.zeros_like(q_all))

    def score_chunk(t, slot, c):
        k = k_ref[_key_cols(c), :]
        sc0 = lax.dot_general(k, qm_ref[0, _q_rows(t), :], _NT, preferred_element_type=jnp.float32)
        sc1 = lax.dot_general(k, qm_ref[1, _q_rows(t), :], _NT, preferred_element_type=jnp.float32)
        s_refs[slot][0, _key_cols(c), :] = sc0
        s_refs[slot][1, _key_cols(c), :] = sc1
        return _fold_keys(sc0, jnp.maximum), _fold_keys(sc1, jnp.maximum)

    def max_merge(m, mc):
        return mc if m is None else (jnp.maximum(m[0], mc[0]), jnp.maximum(m[1], mc[1]))

    def max_store(slot, m):
        m_refs[slot][0] = jnp.max(m[0], axis=0, keepdims=True)
        m_refs[slot][1] = jnp.max(m[1], axis=0, keepdims=True)

    def max_load(slot):
        return m_refs[slot][0], m_refs[slot][1]

    def value_chunk(slot, c, m, acc):
        acc0, acc1 = (None, None) if acc is None else acc
        return (_prob_values(s_refs[slot].at[0], vx_ref, c, m[0], acc0),
                _prob_values(s_refs[slot].at[1], vx_ref, c, m[1], acc1))

    def write_out(t, acc):
        ox0, ox1 = acc
        ot = (ox0[:width] * (1.0 / ox0[width:width + 1])
              - ox1[:width] * (lam / ox1[width:width + 1]))
        o = ot.T
        o_ref[_q_rows(t), :] = (_rms(o, gsub_ref[...]) * (1.0 - LAM_INIT)).astype(o_ref.dtype)

    _pipelined_tiles(q_ref.shape[0] // Q_TILE, k_ref.shape[0] // KEY_CHUNK,
                     score_chunk, max_merge, max_store, max_load, value_chunk, write_out)


def _dattn(qa, ka, vat, lamv, gsub, batch, seq):
    head = lambda b, h: (b, h)
    return pl.pallas_call(
        _dattn_kernel,
        grid=(batch, DA_HEADS),
        in_specs=[
            pl.BlockSpec((seq, LANES), head),
            pl.BlockSpec((seq, LANES), head),
            pl.BlockSpec((LANES, seq), lambda b, h: (h, b)),
            pl.BlockSpec(lamv.shape, lambda b, h: (0, 0)),
            pl.BlockSpec(gsub.shape, lambda b, h: (0, 0)),
        ],
        out_specs=pl.BlockSpec((seq, LANES), head),
        out_shape=jax.ShapeDtypeStruct(qa.shape, jnp.bfloat16),
        scratch_shapes=[pltpu.VMEM((2, seq, Q_TILE), jnp.float32)] * SCORE_SLOTS
        + [pltpu.VMEM((2, 1, Q_TILE), jnp.float32)] * SCORE_SLOTS
        + [pltpu.VMEM((PV_ROWS, seq), jnp.bfloat16), pltpu.VMEM((2, seq, LANES), jnp.bfloat16)],
        compiler_params=pltpu.CompilerParams(
            dimension_semantics=("parallel", "parallel"), vmem_limit_bytes=VMEM_LIMIT),
        name="dattn",
    )(qa, ka, vat, lamv, gsub)


def _mattn_kernel(q_ref, k_ref, vt_ref, o_ref, s0_ref, s1_ref, s2_ref, m0_ref, m1_ref, m2_ref,
                  vx_ref):
    s_refs, m_refs = (s0_ref, s1_ref, s2_ref), (m0_ref, m1_ref, m2_ref)
    _fill_values_ext(vx_ref, vt_ref)
    width = vt_ref.shape[0]

    def score_chunk(t, slot, c):
        sc = lax.dot_general(k_ref[_key_cols(c), :], q_ref[_q_rows(t), :], _NT,
                             preferred_element_type=jnp.float32)
        s_refs[slot][_key_cols(c), :] = sc
        return _fold_keys(sc, jnp.maximum)

    def max_merge(m, mc):
        return mc if m is None else jnp.maximum(m, mc)

    def max_store(slot, m):
        m_refs[slot][...] = jnp.max(m, axis=0, keepdims=True)

    def max_load(slot):
        return m_refs[slot][...]

    def value_chunk(slot, c, m, acc):
        return _prob_values(s_refs[slot], vx_ref, c, m, acc)

    def write_out(t, ox):
        ot = ox[:width] * (1.0 / ox[width:width + 1])
        o_ref[_q_rows(t), :] = ot.T.astype(o_ref.dtype)

    _pipelined_tiles(q_ref.shape[0] // Q_TILE, k_ref.shape[0] // KEY_CHUNK,
                     score_chunk, max_merge, max_store, max_load, value_chunk, write_out)


def _mattn(qm, km, vmt, batch, seq):
    head = lambda b, h: (b, h)
    return pl.pallas_call(
        _mattn_kernel,
        grid=(batch, MLA_HEADS),
        in_specs=[
            pl.BlockSpec((seq, MLA_QK_PAD), head),
            pl.BlockSpec((seq, MLA_QK_PAD), head),
            pl.BlockSpec((MLA_V, seq), lambda b, h: (h, b)),
        ],
        out_specs=pl.BlockSpec((seq, MLA_V), head),
        out_shape=jax.ShapeDtypeStruct((batch * seq, MLA_HEADS * MLA_V), jnp.bfloat16),
        scratch_shapes=[pltpu.VMEM((seq, Q_TILE), jnp.float32)] * SCORE_SLOTS
        + [pltpu.VMEM((1, Q_TILE), jnp.float32)] * SCORE_SLOTS
        + [pltpu.VMEM((PV_ROWS, seq), jnp.bfloat16)],
        compiler_params=pltpu.CompilerParams(
            dimension_semantics=("parallel", "parallel"), vmem_limit_bytes=VMEM_LIMIT),
        name="mattn",
    )(qm, km, vmt)


def _rows(parts, dtype):
    sub = lax.broadcasted_iota(jnp.int32, (len(parts), parts[0].shape[1]), 0)
    out = jnp.zeros(sub.shape, dtype)
    for k, p in enumerate(parts):
        out = jnp.where(sub == k, p.astype(dtype), out)
    return out


def _merge_kernel(x_ref, oa_ref, ob_ref, gates_ref, woa_ref, wob_ref, wout_ref, gffn_ref,
                  wrt_ref, brt_ref,
                  x1_ref, h2_ref, idx_ref, w_ref, rank_ref, counts_ref, carry_ref, hprev_ref):
    i = pl.program_id(0)

    @pl.when(i == 0)
    def _():
        carry_ref[...] = jnp.zeros_like(carry_ref)
        hprev_ref[...] = jnp.zeros_like(hprev_ref)

    gates = gates_ref[...].astype(jnp.float32)
    half = D_MODEL // 2

    logits = lax.dot_general(wrt_ref[...], hprev_ref[...], _NT, precision=lax.Precision.HIGHEST,
                             preferred_element_type=jnp.float32) + brt_ref[...]
    tm = logits.shape[1]
    sub = lax.broadcasted_iota(jnp.int32, logits.shape, 0)
    vals, idxs, hots = [], [], []
    l = logits

    def pick(l):
        m = jnp.max(l, axis=0, keepdims=True)
        idx = jnp.min(jnp.where(l == m, sub, N_EXPERTS), axis=0, keepdims=True)
        hot = sub == idx
        vals.append(m)
        idxs.append(idx)
        hots.append(hot)
        return jnp.where(hot, -jnp.inf, l)

    ya_lo = gates[:, :half] * _dot(oa_ref[...], woa_ref[:, :half])
    l = pick(l)
    ya_hi = gates[:, half:D_MODEL] * _dot(oa_ref[...], woa_ref[:, half:])
    l = pick(l)
    yb_lo = gates[:, D_MODEL:D_MODEL + half] * _dot(ob_ref[...], wob_ref[:, :half])
    l = pick(l)
    yb_hi = gates[:, D_MODEL + half:] * _dot(ob_ref[...], wob_ref[:, half:])
    l = pick(l)
    merged = jnp.concatenate([ya_lo + yb_lo, ya_hi + yb_hi], axis=1).astype(jnp.bfloat16)

    es = [jnp.exp(v - vals[0]) for v in vals]
    den = es[0] + es[1] + es[2] + es[3]
    w_ref[...] = _rows([e / den for e in es], jnp.float32)
    idx_ref[...] = _rows(idxs, jnp.int32)
    x1_lo = x_ref[:, :half] + _dot(merged, wout_ref[:, :half])

    chosen = (hots[0] | hots[1] | hots[2] | hots[3]).astype(jnp.float32)
    r_i = lax.broadcasted_iota(jnp.int32, (tm, tm), 0)
    c_i = lax.broadcasted_iota(jnp.int32, (tm, tm), 1)
    earlier = (r_i < c_i).astype(jnp.bfloat16)
    prefix = _dot(chosen.astype(jnp.bfloat16), earlier) + carry_ref[...]
    x1_hi = x_ref[:, half:] + _dot(merged, wout_ref[:, half:])
    ranks = [jnp.sum(jnp.where(h, prefix, 0.0), axis=0, keepdims=True) for h in hots]
    rank_ref[...] = _rows(ranks, jnp.int32)
    live = (i > 0).astype(jnp.float32)
    carry = carry_ref[...] + live * jnp.sum(chosen, axis=1, keepdims=True)
    carry_ref[...] = carry
    counts_ref[...] = carry.astype(jnp.int32)

    x1 = jnp.concatenate([x1_lo, x1_hi], axis=1)
    x1_ref[...] = x1
    h2 = _rms(x1, gffn_ref[...])
    h2_ref[...] = h2
    hprev_ref[...] = h2


def _merge(x2, oa, ob, gates, woa, wob, wout, gffn, wrt, brt):
    t = x2.shape[0]
    tm = TOKEN_TILE
    n = t // tm
    row = lambda w: pl.BlockSpec((tm, w), lambda i: (jnp.minimum(i, n - 1), 0))
    col = lambda: pl.BlockSpec((TOP_K, tm), lambda i: (0, jnp.maximum(i - 1, 0)))
    consts = (woa, wob, wout, gffn, wrt, brt)
    return pl.pallas_call(
        _merge_kernel,
        grid=(n + 1,),
        in_specs=[row(D_MODEL), row(D_MODEL), row(D_MODEL), row(2 * D_MODEL)]
        + [_const_spec(c.shape) for c in consts],
        out_specs=[row(D_MODEL), row(D_MODEL), col(), col(), col(),
                   _const_spec((N_EXPERTS, 1))],
        out_shape=[
            jax.ShapeDtypeStruct((t, D_MODEL), jnp.float32),
            jax.ShapeDtypeStruct((t, D_MODEL), jnp.float32),
            jax.ShapeDtypeStruct((TOP_K, t), jnp.int32),
            jax.ShapeDtypeStruct((TOP_K, t), jnp.float32),
            jax.ShapeDtypeStruct((TOP_K, t), jnp.int32),
            jax.ShapeDtypeStruct((N_EXPERTS, 1), jnp.int32),
        ],
        scratch_shapes=[pltpu.VMEM((N_EXPERTS, 1), jnp.float32),
                        pltpu.VMEM((tm, D_MODEL), jnp.float32)],
        compiler_params=pltpu.CompilerParams(
            dimension_semantics=("arbitrary",), vmem_limit_bytes=VMEM_LIMIT),
        name="merge",
    )(x2, oa, ob, gates, *consts)


def _row_copy_wait(src_rows_ref, dst_rows_ref, sem, n):
    for _ in range(n):
        pltpu.make_async_copy(src_rows_ref, dst_rows_ref, sem).wait()


def _dispatch_kernel(dest_ref, pad_lo_ref, pad_hi_ref, nu_ref, h2_ref, xs_ref, zero_ref, sem, zsem):
    i = pl.program_id(0)
    tm = h2_ref.shape[0]
    bm = zero_ref.shape[0]
    n_blocks = xs_ref.shape[0] // bm

    @pl.when(i == 0)
    def _():
        zero_ref[...] = jnp.zeros_like(zero_ref)

        def pad_rows(fn):
            def per_expert(e, c):
                return lax.fori_loop(pad_lo_ref[e], pad_hi_ref[e], fn, c)
            lax.fori_loop(0, N_EXPERTS, per_expert, 0)

        def row_copy(j):
            return pltpu.make_async_copy(zero_ref.at[pl.ds(0, 1)], xs_ref.at[pl.ds(j, 1)], zsem)

        def blk_copy(b):
            return pltpu.make_async_copy(zero_ref, xs_ref.at[pl.ds(pl.multiple_of(b * bm, bm), bm)], zsem)

        def start_row(j, c):
            row_copy(j).start()
            return c

        def wait_row(j, c):
            row_copy(j).wait()
            return c

        def start_blk(b, c):
            blk_copy(b).start()
            return c

        def wait_blk(b, c):
            blk_copy(b).wait()
            return c

        pad_rows(start_row)
        lax.fori_loop(nu_ref[0], n_blocks, start_blk, 0)
        pad_rows(wait_row)
        lax.fori_loop(nu_ref[0], n_blocks, wait_blk, 0)

    def body(r, c):
        base = (i * tm + r) * TOP_K
        for k in range(TOP_K):
            d = dest_ref[base + k]
            pltpu.make_async_copy(h2_ref.at[pl.ds(r, 1)], xs_ref.at[pl.ds(d, 1)], sem).start()
        return c

    lax.fori_loop(0, tm, body, 0, unroll=ROW_DMA_UNROLL)
    _row_copy_wait(h2_ref, xs_ref.at[pl.ds(0, tm)], sem, TOP_K)


def _dispatch(dest, pad_lo, pad_hi, n_used, h2, n_blocks):
    t = h2.shape[0]
    tm = DISPATCH_TILE
    bm = ROW_BLOCK
    return pl.pallas_call(
        _dispatch_kernel,
        grid_spec=pltpu.PrefetchScalarGridSpec(
            num_scalar_prefetch=4,
            grid=(t // tm,),
            in_specs=[pl.BlockSpec((tm, D_MODEL), lambda i, *_: (i, 0))],
            out_specs=pl.BlockSpec(memory_space=pl.ANY),
            scratch_shapes=[pltpu.VMEM((bm, D_MODEL), jnp.float32),
                            pltpu.SemaphoreType.DMA(()), pltpu.SemaphoreType.DMA(())],
        ),
        out_shape=jax.ShapeDtypeStruct((n_blocks * bm, D_MODEL), jnp.float32),
        compiler_params=pltpu.CompilerParams(
            dimension_semantics=("arbitrary",), vmem_limit_bytes=VMEM_LIMIT),
        name="dispatch",
    )(dest, pad_lo, pad_hi, n_used, h2)


def _experts_kernel(be_ref, nu_ref, nxt_ref, grp_ref, xs_ref, wg_ref, bg_ref, wu_ref, bu_ref, wd_ref,
                    bd_ref, ys_ref, stage_ref, wbf_ref, sem):
    w_hbm = (wg_ref, wu_ref, wd_ref)
    bm = ROW_BLOCK

    def weight_copies(expert, slot):
        return [pltpu.make_async_copy(w.at[expert], stage_ref.at[slot, j], sem.at[slot, j])
                for j, w in enumerate(w_hbm)]

    def row_block(b, rows):
        e = be_ref[b]
        slot = lax.rem(grp_ref[b], 2)

        @pl.when(b == 0)
        def _():
            for cp in weight_copies(e, slot):
                cp.start()

        @pl.when(jnp.logical_or(b == 0, e != be_ref[jnp.maximum(b - 1, 0)]))
        def _():
            for j, cp in enumerate(weight_copies(e, slot)):
                cp.wait()
                wbf_ref[j] = stage_ref[slot, j].astype(jnp.bfloat16)

            @pl.when(nxt_ref[b] >= 0)
            def _():
                for cp in weight_copies(nxt_ref[b], 1 - slot):
                    cp.start()

        @pl.when(b >= nu_ref[0])
        def _():
            ys_ref[rows, :] = jnp.zeros((bm, ys_ref.shape[1]), ys_ref.dtype)

        @pl.when(b < nu_ref[0])
        def _():
            xb = xs_ref[rows, :].astype(jnp.bfloat16)
            gate = jnp.minimum(_dot(xb, wbf_ref[0]) + bg_ref[e], SWIGLU_LIMIT)
            up = jnp.clip(_dot(xb, wbf_ref[1]) + bu_ref[e], -SWIGLU_LIMIT, SWIGLU_LIMIT)
            act = (up + 1.0) * (gate * jax.nn.sigmoid(SWIGLU_ALPHA * gate))
            ys_ref[rows, :] = _dot(act.astype(jnp.bfloat16), wbf_ref[2]) + bd_ref[e]

    for j in range(BLOCKS_PER_STEP):
        row_block(pl.program_id(0) * BLOCKS_PER_STEP + j, slice(j * bm, (j + 1) * bm))


def _experts(block_e, n_used, next_e, group, xs, wg, bg, wu, bu, wd, bd, n_blocks):
    rows_per_step = ROW_BLOCK * BLOCKS_PER_STEP
    rows = lambda s, be, nu, *_: (jnp.maximum(jnp.minimum(s, (nu[0] - 1) // BLOCKS_PER_STEP), 0), 0)
    hbm = lambda: pl.BlockSpec(memory_space=pl.ANY)
    bias = lambda: pl.BlockSpec((N_EXPERTS, 1, D_FF), lambda s, *_: (0, 0, 0))
    return pl.pallas_call(
        _experts_kernel,
        grid_spec=pltpu.PrefetchScalarGridSpec(
            num_scalar_prefetch=4,
            grid=(n_blocks // BLOCKS_PER_STEP,),
            in_specs=[pl.BlockSpec((rows_per_step, D_MODEL), rows), hbm(), bias(), hbm(), bias(),
                      hbm(), bias()],
            out_specs=pl.BlockSpec((rows_per_step, D_MODEL), lambda s, *_: (s, 0)),
            scratch_shapes=[pltpu.VMEM((2, 3, D_MODEL, D_FF), jnp.float32),
                            pltpu.VMEM((3, D_MODEL, D_FF), jnp.bfloat16),
                            pltpu.SemaphoreType.DMA((2, 3))],
        ),
        out_shape=jax.ShapeDtypeStruct((n_blocks * ROW_BLOCK, D_MODEL), jnp.float32),
        compiler_params=pltpu.CompilerParams(
            dimension_semantics=("arbitrary",), vmem_limit_bytes=VMEM_LIMIT),
        name="experts",
    )(block_e, n_used, next_e, group, xs, wg, bg, wu, bu, wd, bd)


def _combine_kernel(dest_ref, ys_ref, x1_ref, w_ref, gfin_ref, o_ref, buf_ref, sem):
    i = pl.program_id(0)
    tm = x1_ref.shape[0]

    def gather(tile):
        slot = lax.rem(tile, 2)

        def body(r, c):
            base = (tile * tm + r) * TOP_K
            for k in range(TOP_K):
                d = dest_ref[base + k]
                pltpu.make_async_copy(ys_ref.at[pl.ds(d, 1)], buf_ref.at[slot, k, pl.ds(r, 1)],
                                      sem.at[slot]).start()
            return c

        lax.fori_loop(0, tm, body, 0, unroll=ROW_DMA_UNROLL)

    @pl.when(i == 0)
    def _():
        gather(i)

    @pl.when(i + 1 < pl.num_programs(0))
    def _():
        gather(i + 1)

    slot = lax.rem(i, 2)
    _row_copy_wait(ys_ref.at[pl.ds(0, tm)], buf_ref.at[slot, 0], sem.at[slot], TOP_K)
    w = w_ref[...]
    y = x1_ref[...]
    for k in range(TOP_K):
        y = y + buf_ref[slot, k] * w[:, k:k + 1]
    o_ref[...] = _rms(y, gfin_ref[...])


def _combine(dest, ys, x1, top_w, gfin):
    t = x1.shape[0]
    tm = TOKEN_TILE
    return pl.pallas_call(
        _combine_kernel,
        grid_spec=pltpu.PrefetchScalarGridSpec(
            num_scalar_prefetch=1,
            grid=(t // tm,),
            in_specs=[pl.BlockSpec(memory_space=pl.ANY),
                      pl.BlockSpec((tm, D_MODEL), lambda i, d: (i, 0)),
                      pl.BlockSpec((tm, TOP_K), lambda i, d: (i, 0)),
                      pl.BlockSpec((1, D_MODEL), lambda i, d: (0, 0))],
            out_specs=pl.BlockSpec((tm, D_MODEL), lambda i, d: (i, 0)),
            scratch_shapes=[pltpu.VMEM((2, TOP_K, tm, D_MODEL), jnp.float32),
                            pltpu.SemaphoreType.DMA((2,))],
        ),
        out_shape=jax.ShapeDtypeStruct((t, D_MODEL), jnp.float32),
        compiler_params=pltpu.CompilerParams(
            dimension_semantics=("arbitrary",), vmem_limit_bytes=VMEM_LIMIT),
        name="combine",
    )(dest, ys, x1, top_w, gfin)


def _rope_lane_table(rot, group):
    half = rot // 2
    inv = ROPE_THETA ** (-jnp.arange(0, rot, 2, dtype=jnp.float32) / rot)
    d = jnp.arange(LANES) % group
    first = jnp.arange(LANES) < (LANES if group < LANES else rot)
    in_lo = (d < half) & first
    in_hi = (d >= half) & (d < rot) & first
    freq = jnp.where(in_lo | in_hi, inv[d % half], 0.0)
    return jnp.stack([freq, in_lo.astype(jnp.float32), in_hi.astype(jnp.float32)]).astype(jnp.float32)


def kernel(x, positions, g_mix, w_in, lam_q1, lam_k1, lam_q2, lam_k2, g_subln, g_q, g_kv, w_uq, w_ukv, w_o_diff, w_o_mla, b_gates, w_out, g_ffn, w_router, b_router, w_gate, b_gate, w_up, b_up, w_down, b_down, g_final):
    batch, seq, d = x.shape
    t = batch * seq
    bf = jnp.bfloat16
    l = 0
    x2 = x.reshape(t, d)
    posf = positions.astype(jnp.float32).reshape(t, 1)

    w = w_in[l]
    win = w[:, :IN_KR].astype(bf)
    wkr = jnp.pad(w[:, IN_KR:IN_GATES], ((0, 0), (0, LANES - MLA_ROPE))).astype(bf)
    wgate = w[:, IN_GATES:].astype(bf)
    wuq = jnp.pad(w_uq[l].reshape(MLA_Q_LORA, MLA_HEADS, MLA_NOPE + MLA_ROPE),
                  ((0, 0), (0, 0), (0, MLA_QK_PAD - MLA_NOPE - MLA_ROPE))
                  ).reshape(MLA_Q_LORA, MLA_HEADS * MLA_QK_PAD).astype(bf)
    wukv = w_ukv[l].reshape(MLA_KV_LORA, MLA_HEADS, MLA_NOPE + MLA_V)
    wkn = wukv[:, :, :MLA_NOPE].reshape(MLA_KV_LORA, MLA_HEADS * MLA_NOPE).astype(bf)
    wvt = wukv[:, :, MLA_NOPE:].reshape(MLA_KV_LORA, MLA_HEADS * MLA_V).T.astype(bf)
    wvat = w[:, IN_OFF["va"]:IN_OFF["va"] + IN_W["va"]].T.astype(bf)
    fd = _rope_lane_table(DA_ROT, DA_HEAD_DIM)
    fm = _rope_lane_table(MLA_ROPE, LANES)

    qa, ka, vat, gates, qm, km, vmt = _proj(
        x2, posf, g_mix[l][None], win, wkr, wgate, b_gates[l][None],
        g_q[l][None], g_kv[l][None], wuq, wkn, wvat, wvt, fd, fm)

    lamv = jnp.stack([lam_q1[l], lam_k1[l], lam_q2[l], lam_k2[l]]).astype(jnp.float32)
    oa = _dattn(qa, ka, vat, lamv, g_subln[l][None], batch, seq)
    ob = _mattn(qm, km, vmt, batch, seq)

    x1, h2, e_idx, top_w, rank, counts = _merge(
        x2, oa, ob, gates, w_o_diff[l].astype(bf), w_o_mla[l].astype(bf), w_out[l].astype(bf),
        g_ffn[l][None], w_router[l].T, b_router[l][:, None])

    bm = ROW_BLOCK
    n_blocks = (t * TOP_K) // bm + N_EXPERTS
    counts = counts[:, 0]
    padded = (counts + bm - 1) // bm * bm
    padded_end = jnp.cumsum(padded)
    padded_start = padded_end - padded
    n_used = (padded_end[-1] // bm).astype(jnp.int32)
    blk = jnp.minimum(jnp.arange(n_blocks, dtype=jnp.int32), n_used - 1)
    block_e = jnp.minimum(jnp.sum(padded_end[None, :] <= (blk * bm)[:, None], axis=1),
                          N_EXPERTS - 1).astype(jnp.int32)
    hot = e_idx[:, :, None] == jnp.arange(N_EXPERTS, dtype=jnp.int32)
    dest = (jnp.sum(jnp.where(hot, padded_start, 0), axis=-1) + rank).T.reshape(-1).astype(jnp.int32)

    n_used = n_used.reshape(1)
    xs = _dispatch(dest, (padded_start + counts).astype(jnp.int32), padded_end.astype(jnp.int32),
                   n_used, h2, n_blocks)
    later = block_e[None, :] > block_e[:, None]
    next_e = jnp.min(jnp.where(later, block_e[None, :], N_EXPERTS), axis=1)
    next_e = jnp.where(next_e == N_EXPERTS, -1, next_e).astype(jnp.int32)
    group = jnp.cumsum(jnp.concatenate([jnp.zeros((1,), jnp.int32),
                                        (block_e[1:] != block_e[:-1]).astype(jnp.int32)]))
    ys = _experts(block_e, n_used, next_e, group.astype(jnp.int32), xs, w_gate[l],
                  b_gate[l][:, None, :], w_up[l], b_up[l][:, None, :], w_down[l],
                  b_down[l][:, None, :], n_blocks)
    out = _combine(dest, ys, x1, top_w.T, g_final[None])
    return out.reshape(batch, seq, d)
```

```python
import math

import jax
import jax.numpy as jnp
from jax import lax
from jax.experimental import pallas as pl
from jax.experimental.pallas import tpu as pltpu

D_MODEL = 1024
ROPE_THETA = 500000.0
NORM_EPS = 1e-6
DA_HEADS = 8
DA_HEAD_DIM = 64
DA_ROT = DA_HEAD_DIM // 4
MLA_HEADS = 8
MLA_Q_LORA = 768
MLA_KV_LORA = 512
MLA_NOPE = 128
MLA_ROPE = 64
MLA_V = 128
N_EXPERTS = 32
TOP_K = 4
D_FF = 1024
SWIGLU_ALPHA = 1.702
SWIGLU_LIMIT = 7.0
LAM_INIT = 0.8 - 0.6 * math.exp(-0.3 * 0)

LANES = 128
SUBLANES = 8
PV_ROWS = LANES + 16
MLA_QK_PAD = 256
TOKEN_TILE = 256
DISPATCH_TILE = 512
Q_TILE = 256
SCORE_SLOTS = 3
KEY_CHUNK = 256
ROW_BLOCK = 256
BLOCKS_PER_STEP = 4
ROW_DMA_UNROLL = 8
VMEM_LIMIT = 56 * 1024 * 1024

LOG2_E = math.log2(math.e)

IN_W = {"qa": DA_HEADS * 2 * DA_HEAD_DIM, "ka": DA_HEADS * 2 * DA_HEAD_DIM,
        "va": DA_HEADS * 2 * DA_HEAD_DIM, "cq": MLA_Q_LORA, "ckv": MLA_KV_LORA}
IN_OFF = dict(zip(IN_W, [sum(list(IN_W.values())[:n]) for n in range(len(IN_W))]))
IN_KR = sum(IN_W.values())
IN_GATES = IN_KR + MLA_ROPE

_NT = (((1,), (1,)), ((), ()))


def _rms(x, g):
    return x * lax.rsqrt(jnp.mean(x * x, axis=-1, keepdims=True) + NORM_EPS) * g


def _dot(a, b):
    return jnp.dot(a, b, preferred_element_type=jnp.float32)


def _rope_tables(pos, freq):
    ang = pos * freq
    c = jnp.cos(ang)
    s = jnp.sin(ang)
    lane = lax.broadcasted_iota(jnp.int32, c.shape, 1)
    m_half, m_rot = MLA_ROPE // 2, MLA_ROPE
    t_mla = (jnp.where(lane < m_rot, c, 1.0),
             jnp.where(lane < m_half, -s, 0.0),
             jnp.where((lane >= m_half) & (lane < m_rot), s, 0.0))
    c_d = jnp.where(lane < m_rot, pltpu.roll(c, LANES // 2, 1), c)
    s_d = jnp.where(lane < m_rot, pltpu.roll(s, LANES // 2, 1), s)
    g = lane % DA_HEAD_DIM
    d_half, d_rot = DA_ROT // 2, DA_ROT
    t_diff = (jnp.where(g < d_rot, c_d, 1.0),
              jnp.where(g < d_half, -s_d, 0.0),
              jnp.where((g >= d_half) & (g < d_rot), s_d, 0.0))
    return t_diff, t_mla


def _rope_block(xb, tables, half):
    c, s_lo, s_hi = tables
    return (xb * c + pltpu.roll(xb, LANES - half, 1) * s_lo
            + pltpu.roll(xb, half, 1) * s_hi)


def _proj_kernel(x_ref, pos_ref, gmix_ref, win_ref, wkr_ref, wgate_ref, bg_ref, gq_ref, gkv_ref,
                 wuq_ref, wkn_ref,
                 wvat_ref, wvt_ref, freq_ref,
                 qa_ref, ka_ref, vat_ref, gates_ref, qm_ref, km_ref, vmt_ref):
    hb = _rms(x_ref[...], gmix_ref[...]).astype(jnp.bfloat16)
    tm = hb.shape[0]

    da_scale = DA_HEAD_DIM ** -0.5 * LOG2_E
    w_cols = lambda name: win_ref[:, IN_OFF[name]:IN_OFF[name] + IN_W[name]]
    mla_scale = (MLA_NOPE + MLA_ROPE) ** -0.5 * LOG2_E

    def in_cols(name, lo, width):
        return win_ref[:, IN_OFF[name] + lo:IN_OFF[name] + lo + width]

    n_groups = 4
    heads = DA_HEADS // n_groups

    cq = _rms(_dot(hb, w_cols("cq")), gq_ref[...]).astype(jnp.bfloat16)
    ck = _dot(hb, w_cols("ckv"))
    kr_in = _dot(hb, wkr_ref[...])
    td_parts, tmla_parts = [], []
    for g in range(n_groups):
        rows = slice(g * tm // n_groups, (g + 1) * tm // n_groups)
        pos = pos_ref[rows, :]
        t_diff, t_mla = _rope_tables(pos, freq_ref[...])
        td_parts.append(t_diff)
        tmla_parts.append(t_mla)
        cols = slice(g * heads * LANES, (g + 1) * heads * LANES)
        gw = 2 * D_MODEL // n_groups
        gcols = slice(g * gw, (g + 1) * gw)
        gates_ref[:, gcols] = jax.nn.sigmoid(
            _dot(hb, wgate_ref[:, gcols]) + bg_ref[:, gcols]).astype(jnp.bfloat16)
        vat_ref[cols, :] = lax.dot_general(wvat_ref[cols, :], hb, _NT,
                                           preferred_element_type=jnp.float32).astype(jnp.bfloat16)
    td = tuple(jnp.concatenate([p[k] for p in td_parts], axis=0) for k in range(3))
    tmla = tuple(jnp.concatenate([p[k] for p in tmla_parts], axis=0) for k in range(3))
    kr = _rope_block(kr_in, tmla, MLA_ROPE // 2).astype(jnp.bfloat16)
    ckv = _rms(ck, gkv_ref[...]).astype(jnp.bfloat16)

    for g in range(n_groups):
        zq = _dot(hb, in_cols("qa", g * heads * LANES, heads * LANES))
        for j in range(heads):
            dst = slice((g * heads + j) * LANES, (g * heads + j + 1) * LANES)
            blk = zq[:, j * LANES:(j + 1) * LANES]
            qa_ref[:, dst] = (_rope_block(blk, td, DA_ROT // 2) * da_scale).astype(jnp.bfloat16)

        qm = _dot(cq, wuq_ref[:, g * heads * MLA_QK_PAD:(g + 1) * heads * MLA_QK_PAD])
        for j in range(heads):
            h = g * heads + j
            lo = slice(j * MLA_QK_PAD, j * MLA_QK_PAD + LANES)
            hi = slice(j * MLA_QK_PAD + LANES, (j + 1) * MLA_QK_PAD)
            qm_ref[:, h * MLA_QK_PAD:h * MLA_QK_PAD + LANES] = (qm[:, lo] * mla_scale).astype(jnp.bfloat16)
            qm_ref[:, h * MLA_QK_PAD + LANES:(h + 1) * MLA_QK_PAD] = (
                _rope_block(qm[:, hi], tmla, MLA_ROPE // 2) * mla_scale).astype(jnp.bfloat16)

        zk = _dot(hb, in_cols("ka", g * heads * LANES, heads * LANES))
        for j in range(heads):
            dst = slice((g * heads + j) * LANES, (g * heads + j + 1) * LANES)
            blk = zk[:, j * LANES:(j + 1) * LANES]
            ka_ref[:, dst] = _rope_block(blk, td, DA_ROT // 2).astype(jnp.bfloat16)

        kn = _dot(ckv, wkn_ref[:, g * heads * LANES:(g + 1) * heads * LANES])
        for j in range(heads):
            h = g * heads + j
            km_ref[:, h * MLA_QK_PAD:h * MLA_QK_PAD + LANES] = kn[:, j * LANES:(j + 1) * LANES].astype(jnp.bfloat16)
            km_ref[:, h * MLA_QK_PAD + LANES:(h + 1) * MLA_QK_PAD] = kr

        cols = slice(g * heads * LANES, (g + 1) * heads * LANES)
        vmt_ref[cols, :] = lax.dot_general(wvt_ref[cols, :], ckv, _NT,
                                           preferred_element_type=jnp.float32).astype(jnp.bfloat16)


def _const_spec(shape):
    return pl.BlockSpec(shape, lambda i: (0,) * len(shape))


def _proj(x2, posf, gmix, win, wkr, wgate, bg, gq, gkv, wuq, wkn, wvat, wvt, freq):
    t = x2.shape[0]
    tm = TOKEN_TILE
    bf = jnp.bfloat16
    row = lambda w: pl.BlockSpec((tm, w), lambda i: (i, 0))
    col = lambda w: pl.BlockSpec((w, tm), lambda i: (0, i))
    consts = (gmix, win, wkr, wgate, bg, gq, gkv, wuq, wkn, wvat, wvt, freq)
    outs = ((row, D_MODEL), (row, D_MODEL), (col, D_MODEL), (row, 2 * D_MODEL),
            (row, MLA_HEADS * MLA_QK_PAD), (row, MLA_HEADS * MLA_QK_PAD), (col, MLA_HEADS * MLA_V))
    return pl.pallas_call(
        _proj_kernel,
        grid=(t // tm,),
        in_specs=[row(D_MODEL), row(1)] + [_const_spec(c.shape) for c in consts],
        out_specs=[kind(w) for kind, w in outs],
        out_shape=[jax.ShapeDtypeStruct((t, w) if kind is row else (w, t), bf) for kind, w in outs],
        compiler_params=pltpu.CompilerParams(
            dimension_semantics=("parallel",), vmem_limit_bytes=VMEM_LIMIT),
        name="proj",
    )(x2, posf, *consts)


def _pipelined_tiles(n_tiles, n_chunks, score_chunk, max_merge, max_store, max_load,
                     value_chunk, write_out):
    def fused(t_next, slot_next, t_cur, slot_cur):
        m_cur = None if t_cur is None else max_load(slot_cur)
        m_next, acc = None, None
        for c in range(n_chunks):
            if t_next is not None:
                m_next = max_merge(m_next, score_chunk(t_next, slot_next, c))
            if t_cur is not None:
                acc = value_chunk(slot_cur, c, m_cur, acc)
        if t_next is not None:
            max_store(slot_next, m_next)
        if t_cur is not None:
            write_out(t_cur, acc)

    fused(0, 0, None, None)
    fused(1, 1, None, None)
    n_triples = (n_tiles - 2) // SCORE_SLOTS

    def triple(j, carry):
        t = SCORE_SLOTS * j
        for i in range(SCORE_SLOTS):
            fused(t + i + 2, (i + 2) % SCORE_SLOTS, t + i, i)
        return carry

    lax.fori_loop(0, n_triples, triple, 0)
    for t in range(SCORE_SLOTS * n_triples, n_tiles):
        ahead = t + 2 if t + 2 < n_tiles else None
        fused(ahead, (t + 2) % SCORE_SLOTS, t, t % SCORE_SLOTS)


def _q_rows(t):
    if isinstance(t, int):
        return pl.ds(t * Q_TILE, Q_TILE)
    return pl.ds(pl.multiple_of(t * Q_TILE, Q_TILE), Q_TILE)


def _key_cols(c):
    return slice(c * KEY_CHUNK, (c + 1) * KEY_CHUNK)


def _fold_keys(x, op):
    out = x[:SUBLANES]
    for j in range(1, x.shape[0] // SUBLANES):
        out = op(out, x[j * SUBLANES:(j + 1) * SUBLANES])
    return out


def _fill_values_ext(vx_ref, vt_ref):
    width = vt_ref.shape[0]
    vx_ref[:width, :] = vt_ref[...]
    sub = lax.broadcasted_iota(jnp.int32, (vx_ref.shape[0] - width, vt_ref.shape[1]), 0)
    vx_ref[width:, :] = jnp.where(sub == 0, 1.0, 0.0).astype(vx_ref.dtype)


def _prob_values(s_ref, vx_ref, c, m, acc):
    p = jnp.exp2(s_ref[_key_cols(c), :] - m).astype(jnp.bfloat16)
    part = _dot(vx_ref[:, _key_cols(c)], p)
    return part if acc is None else acc + part


def _dattn_kernel(q_ref, k_ref, vt_ref, lam_ref, gsub_ref, o_ref,
                  s0_ref, s1_ref, s2_ref, m0_ref, m1_ref, m2_ref, vx_ref, qm_ref):
    s_refs, m_refs = (s0_ref, s1_ref, s2_ref), (m0_ref, m1_ref, m2_ref)
    _fill_values_ext(vx_ref, vt_ref)
    width = vt_ref.shape[0]
    lamv = lam_ref[...]
    lam = (jnp.exp(jnp.sum(lamv[0:1] * lamv[1:2], axis=-1, keepdims=True))
           - jnp.exp(jnp.sum(lamv[2:3] * lamv[3:4], axis=-1, keepdims=True)) + LAM_INIT)
    q_all = q_ref[...]
    lane = lax.broadcasted_iota(jnp.int32, q_all.shape, 1)
    qm_ref[0] = jnp.where(lane < DA_HEAD_DIM, q_all, jnp.zeros_like(q_all))
    qm_ref[1] = jnp.where(lane >= DA_HEAD_DIM, q_all, jnp.zeros_like(q_all))

    def score_chunk(t, slot, c):
        k = k_ref[_key_cols(c), :]
        sc0 = lax.dot_general(k, qm_ref[0, _q_rows(t), :], _NT, preferred_element_type=jnp.float32)
        sc1 = lax.dot_general(k, qm_ref[1, _q_rows(t), :], _NT, preferred_element_type=jnp.float32)
        s_refs[slot][0, _key_cols(c), :] = sc0
        s_refs[slot][1, _key_cols(c), :] = sc1
        return _fold_keys(sc0, jnp.maximum), _fold_keys(sc1, jnp.maximum)

    def max_merge(m, mc):
        return mc if m is None else (jnp.maximum(m[0], mc[0]), jnp.maximum(m[1], mc[1]))

    def max_store(slot, m):
        m_refs[slot][0] = jnp.max(m[0], axis=0, keepdims=True)
        m_refs[slot][1] = jnp.max(m[1], axis=0, keepdims=True)

    def max_load(slot):
        return m_refs[slot][0], m_refs[slot][1]

    def value_chunk(slot, c, m, acc):
        acc0, acc1 = (None, None) if acc is None else acc
        return (_prob_values(s_refs[slot].at[0], vx_ref, c, m[0], acc0),
                _prob_values(s_refs[slot].at[1], vx_ref, c, m[1], acc1))

    def write_out(t, acc):
        ox0, ox1 = acc
        ot = (ox0[:width] * (1.0 / ox0[width:width + 1])
              - ox1[:width] * (lam / ox1[width:width + 1]))
        o = ot.T
        o_ref[_q_rows(t), :] = (_rms(o, gsub_ref[...]) * (1.0 - LAM_INIT)).astype(o_ref.dtype)

    _pipelined_tiles(q_ref.shape[0] // Q_TILE, k_ref.shape[0] // KEY_CHUNK,
                     score_chunk, max_merge, max_store, max_load, value_chunk, write_out)


def _dattn(qa, ka, vat, lamv, gsub, batch, seq):
    head = lambda b, h: (b, h)
    return pl.pallas_call(
        _dattn_kernel,
        grid=(batch, DA_HEADS),
        in_specs=[
            pl.BlockSpec((seq, LANES), head),
            pl.BlockSpec((seq, LANES), head),
            pl.BlockSpec((LANES, seq), lambda b, h: (h, b)),
            pl.BlockSpec(lamv.shape, lambda b, h: (0, 0)),
            pl.BlockSpec(gsub.shape, lambda b, h: (0, 0)),
        ],
        out_specs=pl.BlockSpec((seq, LANES), head),
        out_shape=jax.ShapeDtypeStruct(qa.shape, jnp.bfloat16),
        scratch_shapes=[pltpu.VMEM((2, seq, Q_TILE), jnp.float32)] * SCORE_SLOTS
        + [pltpu.VMEM((2, 1, Q_TILE), jnp.float32)] * SCORE_SLOTS
        + [pltpu.VMEM((PV_ROWS, seq), jnp.bfloat16), pltpu.VMEM((2, seq, LANES), jnp.bfloat16)],
        compiler_params=pltpu.CompilerParams(
            dimension_semantics=("parallel", "parallel"), vmem_limit_bytes=VMEM_LIMIT),
        name="dattn",
    )(qa, ka, vat, lamv, gsub)


def _mattn_kernel(q_ref, k_ref, vt_ref, o_ref, s0_ref, s1_ref, s2_ref, m0_ref, m1_ref, m2_ref,
                  vx_ref):
    s_refs, m_refs = (s0_ref, s1_ref, s2_ref), (m0_ref, m1_ref, m2_ref)
    _fill_values_ext(vx_ref, vt_ref)
    width = vt_ref.shape[0]

    def score_chunk(t, slot, c):
        sc = lax.dot_general(k_ref[_key_cols(c), :], q_ref[_q_rows(t), :], _NT,
                             preferred_element_type=jnp.float32)
        s_refs[slot][_key_cols(c), :] = sc
        return _fold_keys(sc, jnp.maximum)

    def max_merge(m, mc):
        return mc if m is None else jnp.maximum(m, mc)

    def max_store(slot, m):
        m_refs[slot][...] = jnp.max(m, axis=0, keepdims=True)

    def max_load(slot):
        return m_refs[slot][...]

    def value_chunk(slot, c, m, acc):
        return _prob_values(s_refs[slot], vx_ref, c, m, acc)

    def write_out(t, ox):
        ot = ox[:width] * (1.0 / ox[width:width + 1])
        o_ref[_q_rows(t), :] = ot.T.astype(o_ref.dtype)

    _pipelined_tiles(q_ref.shape[0] // Q_TILE, k_ref.shape[0] // KEY_CHUNK,
                     score_chunk, max_merge, max_store, max_load, value_chunk, write_out)


def _mattn(qm, km, vmt, batch, seq):
    head = lambda b, h: (b, h)
    return pl.pallas_call(
        _mattn_kernel,
        grid=(batch, MLA_HEADS),
        in_specs=[
            pl.BlockSpec((seq, MLA_QK_PAD), head),
            pl.BlockSpec((seq, MLA_QK_PAD), head),
            pl.BlockSpec((MLA_V, seq), lambda b, h: (h, b)),
        ],
        out_specs=pl.BlockSpec((seq, MLA_V), head),
        out_shape=jax.ShapeDtypeStruct((batch * seq, MLA_HEADS * MLA_V), jnp.bfloat16),
        scratch_shapes=[pltpu.VMEM((seq, Q_TILE), jnp.float32)] * SCORE_SLOTS
        + [pltpu.VMEM((1, Q_TILE), jnp.float32)] * SCORE_SLOTS
        + [pltpu.VMEM((PV_ROWS, seq), jnp.bfloat16)],
        compiler_params=pltpu.CompilerParams(
            dimension_semantics=("parallel", "parallel"), vmem_limit_bytes=VMEM_LIMIT),
        name="mattn",
    )(qm, km, vmt)


def _rows(parts, dtype):
    sub = lax.broadcasted_iota(jnp.int32, (len(parts), parts[0].shape[1]), 0)
    out = jnp.zeros(sub.shape, dtype)
    for k, p in enumerate(parts):
        out = jnp.where(sub == k, p.astype(dtype), out)
    return out


def _merge_kernel(x_ref, oa_ref, ob_ref, gates_ref, woa_ref, wob_ref, wout_ref, gffn_ref,
                  wrt_ref, brt_ref,
                  x1_ref, h2_ref, idx_ref, w_ref, rank_ref, counts_ref, carry_ref, hprev_ref):
    i = pl.program_id(0)

    @pl.when(i == 0)
    def _():
        carry_ref[...] = jnp.zeros_like(carry_ref)
        hprev_ref[...] = jnp.zeros_like(hprev_ref)

    gates = gates_ref[...].astype(jnp.float32)
    half = D_MODEL // 2

    logits = lax.dot_general(wrt_ref[...], hprev_ref[...], _NT, precision=lax.Precision.HIGHEST,
                             preferred_element_type=jnp.float32) + brt_ref[...]
    tm = logits.shape[1]
    sub = lax.broadcasted_iota(jnp.int32, logits.shape, 0)
    vals, idxs, hots = [], [], []
    l = logits

    def pick(l):
        m = jnp.max(l, axis=0, keepdims=True)
        idx = jnp.min(jnp.where(l == m, sub, N_EXPERTS), axis=0, keepdims=True)
        hot = sub == idx
        vals.append(m)
        idxs.append(idx)
        hots.append(hot)
        return jnp.where(hot, -jnp.inf, l)

    ya_lo = gates[:, :half] * _dot(oa_ref[...], woa_ref[:, :half])
    l = pick(l)
    ya_hi = gates[:, half:D_MODEL] * _dot(oa_ref[...], woa_ref[:, half:])
    l = pick(l)
    yb_lo = gates[:, D_MODEL:D_MODEL + half] * _dot(ob_ref[...], wob_ref[:, :half])
    l = pick(l)
    yb_hi = gates[:, D_MODEL + half:] * _dot(ob_ref[...], wob_ref[:, half:])
    l = pick(l)
    merged = jnp.concatenate([ya_lo + yb_lo, ya_hi + yb_hi], axis=1).astype(jnp.bfloat16)

    es = [jnp.exp(v - vals[0]) for v in vals]
    den = es[0] + es[1] + es[2] + es[3]
    w_ref[...] = _rows([e / den for e in es], jnp.float32)
    idx_ref[...] = _rows(idxs, jnp.int32)
    x1_lo = x_ref[:, :half] + _dot(merged, wout_ref[:, :half])

    chosen = (hots[0] | hots[1] | hots[2] | hots[3]).astype(jnp.float32)
    r_i = lax.broadcasted_iota(jnp.int32, (tm, tm), 0)
    c_i = lax.broadcasted_iota(jnp.int32, (tm, tm), 1)
    earlier = (r_i < c_i).astype(jnp.bfloat16)
    prefix = _dot(chosen.astype(jnp.bfloat16), earlier) + carry_ref[...]
    x1_hi = x_ref[:, half:] + _dot(merged, wout_ref[:, half:])
    ranks = [jnp.sum(jnp.where(h, prefix, 0.0), axis=0, keepdims=True) for h in hots]
    rank_ref[...] = _rows(ranks, jnp.int32)
    live = (i > 0).astype(jnp.float32)
    carry = carry_ref[...] + live * jnp.sum(chosen, axis=1, keepdims=True)
    carry_ref[...] = carry
    counts_ref[...] = carry.astype(jnp.int32)

    x1 = jnp.concatenate([x1_lo, x1_hi], axis=1)
    x1_ref[...] = x1
    h2 = _rms(x1, gffn_ref[...])
    h2_ref[...] = h2
    hprev_ref[...] = h2


def _merge(x2, oa, ob, gates, woa, wob, wout, gffn, wrt, brt):
    t = x2.shape[0]
    tm = TOKEN_TILE
    n = t // tm
    row = lambda w: pl.BlockSpec((tm, w), lambda i: (jnp.minimum(i, n - 1), 0))
    col = lambda: pl.BlockSpec((TOP_K, tm), lambda i: (0, jnp.maximum(i - 1, 0)))
    consts = (woa, wob, wout, gffn, wrt, brt)
    return pl.pallas_call(
        _merge_kernel,
        grid=(n + 1,),
        in_specs=[row(D_MODEL), row(D_MODEL), row(D_MODEL), row(2 * D_MODEL)]
        + [_const_spec(c.shape) for c in consts],
        out_specs=[row(D_MODEL), row(D_MODEL), col(), col(), col(),
                   _const_spec((N_EXPERTS, 1))],
        out_shape=[
            jax.ShapeDtypeStruct((t, D_MODEL), jnp.float32),
            jax.ShapeDtypeStruct((t, D_MODEL), jnp.float32),
            jax.ShapeDtypeStruct((TOP_K, t), jnp.int32),
            jax.ShapeDtypeStruct((TOP_K, t), jnp.float32),
            jax.ShapeDtypeStruct((TOP_K, t), jnp.int32),
            jax.ShapeDtypeStruct((N_EXPERTS, 1), jnp.int32),
        ],
        scratch_shapes=[pltpu.VMEM((N_EXPERTS, 1), jnp.float32),
                        pltpu.VMEM((tm, D_MODEL), jnp.float32)],
        compiler_params=pltpu.CompilerParams(
            dimension_semantics=("arbitrary",), vmem_limit_bytes=VMEM_LIMIT),
        name="merge",
    )(x2, oa, ob, gates, *consts)


def _row_copy_wait(src_rows_ref, dst_rows_ref, sem, n):
    for _ in range(n):
        pltpu.make_async_copy(src_rows_ref, dst_rows_ref, sem).wait()


def _dispatch_kernel(dest_ref, pad_lo_ref, pad_hi_ref, nu_ref, h2_ref, xs_ref, zero_ref, sem, zsem):
    i = pl.program_id(0)
    tm = h2_ref.shape[0]
    bm = zero_ref.shape[0]
    n_blocks = xs_ref.shape[0] // bm

    @pl.when(i == 0)
    def _():
        zero_ref[...] = jnp.zeros_like(zero_ref)

        def pad_rows(fn):
            def per_expert(e, c):
                return lax.fori_loop(pad_lo_ref[e], pad_hi_ref[e], fn, c)
            lax.fori_loop(0, N_EXPERTS, per_expert, 0)

        def row_copy(j):
            return pltpu.make_async_copy(zero_ref.at[pl.ds(0, 1)], xs_ref.at[pl.ds(j, 1)], zsem)

        def blk_copy(b):
            return pltpu.make_async_copy(zero_ref, xs_ref.at[pl.ds(pl.multiple_of(b * bm, bm), bm)], zsem)

        def start_row(j, c):
            row_copy(j).start()
            return c

        def wait_row(j, c):
            row_copy(j).wait()
            return c

        def start_blk(b, c):
            blk_copy(b).start()
            return c

        def wait_blk(b, c):
            blk_copy(b).wait()
            return c

        pad_rows(start_row)
        lax.fori_loop(nu_ref[0], n_blocks, start_blk, 0)
        pad_rows(wait_row)
        lax.fori_loop(nu_ref[0], n_blocks, wait_blk, 0)

    def body(r, c):
        base = (i * tm + r) * TOP_K
        for k in range(TOP_K):
            d = dest_ref[base + k]
            pltpu.make_async_copy(h2_ref.at[pl.ds(r, 1)], xs_ref.at[pl.ds(d, 1)], sem).start()
        return c

    lax.fori_loop(0, tm, body, 0, unroll=ROW_DMA_UNROLL)
    _row_copy_wait(h2_ref, xs_ref.at[pl.ds(0, tm)], sem, TOP_K)


def _dispatch(dest, pad_lo, pad_hi, n_used, h2, n_blocks):
    t = h2.shape[0]
    tm = DISPATCH_TILE
    bm = ROW_BLOCK
    return pl.pallas_call(
        _dispatch_kernel,
        grid_spec=pltpu.PrefetchScalarGridSpec(
            num_scalar_prefetch=4,
            grid=(t // tm,),
            in_specs=[pl.BlockSpec((tm, D_MODEL), lambda i, *_: (i, 0))],
            out_specs=pl.BlockSpec(memory_space=pl.ANY),
            scratch_shapes=[pltpu.VMEM((bm, D_MODEL), jnp.float32),
                            pltpu.SemaphoreType.DMA(()), pltpu.SemaphoreType.DMA(())],
        ),
        out_shape=jax.ShapeDtypeStruct((n_blocks * bm, D_MODEL), jnp.float32),
        compiler_params=pltpu.CompilerParams(
            dimension_semantics=("arbitrary",), vmem_limit_bytes=VMEM_LIMIT),
        name="dispatch",
    )(dest, pad_lo, pad_hi, n_used, h2)


def _experts_kernel(be_ref, nu_ref, nxt_ref, grp_ref, xs_ref, wg_ref, bg_ref, wu_ref, bu_ref, wd_ref,
                    bd_ref, ys_ref, stage_ref, wbf_ref, sem):
    w_hbm = (wg_ref, wu_ref, wd_ref)
    bm = ROW_BLOCK

    def weight_copies(expert, slot):
        return [pltpu.make_async_copy(w.at[expert], stage_ref.at[slot, j], sem.at[slot, j])
                for j, w in enumerate(w_hbm)]

    def row_block(b, rows):
        e = be_ref[b]
        slot = lax.rem(grp_ref[b], 2)

        @pl.when(b == 0)
        def _():
            for cp in weight_copies(e, slot):
                cp.start()

        @pl.when(jnp.logical_or(b == 0, e != be_ref[jnp.maximum(b - 1, 0)]))
        def _():
            for j, cp in enumerate(weight_copies(e, slot)):
                cp.wait()
                wbf_ref[j] = stage_ref[slot, j].astype(jnp.bfloat16)

            @pl.when(nxt_ref[b] >= 0)
            def _():
                for cp in weight_copies(nxt_ref[b], 1 - slot):
                    cp.start()

        @pl.when(b >= nu_ref[0])
        def _():
            ys_ref[rows, :] = jnp.zeros((bm, ys_ref.shape[1]), ys_ref.dtype)

        @pl.when(b < nu_ref[0])
        def _():
            xb = xs_ref[rows, :].astype(jnp.bfloat16)
            gate = jnp.minimum(_dot(xb, wbf_ref[0]) + bg_ref[e], SWIGLU_LIMIT)
            up = jnp.clip(_dot(xb, wbf_ref[1]) + bu_ref[e], -SWIGLU_LIMIT, SWIGLU_LIMIT)
            act = (up + 1.0) * (gate * jax.nn.sigmoid(SWIGLU_ALPHA * gate))
            ys_ref[rows, :] = _dot(act.astype(jnp.bfloat16), wbf_ref[2]) + bd_ref[e]

    for j in range(BLOCKS_PER_STEP):
        row_block(pl.program_id(0) * BLOCKS_PER_STEP + j, slice(j * bm, (j + 1) * bm))


def _experts(block_e, n_used, next_e, group, xs, wg, bg, wu, bu, wd, bd, n_blocks):
    rows_per_step = ROW_BLOCK * BLOCKS_PER_STEP
    rows = lambda s, be, nu, *_: (jnp.maximum(jnp.minimum(s, (nu[0] - 1) // BLOCKS_PER_STEP), 0), 0)
    hbm = lambda: pl.BlockSpec(memory_space=pl.ANY)
    bias = lambda: pl.BlockSpec((N_EXPERTS, 1, D_FF), lambda s, *_: (0, 0, 0))
    return pl.pallas_call(
        _experts_kernel,
        grid_spec=pltpu.PrefetchScalarGridSpec(
            num_scalar_prefetch=4,
            grid=(n_blocks // BLOCKS_PER_STEP,),
            in_specs=[pl.BlockSpec((rows_per_step, D_MODEL), rows), hbm(), bias(), hbm(), bias(),
                      hbm(), bias()],
            out_specs=pl.BlockSpec((rows_per_step, D_MODEL), lambda s, *_: (s, 0)),
            scratch_shapes=[pltpu.VMEM((2, 3, D_MODEL, D_FF), jnp.float32),
                            pltpu.VMEM((3, D_MODEL, D_FF), jnp.bfloat16),
                            pltpu.SemaphoreType.DMA((2, 3))],
        ),
        out_shape=jax.ShapeDtypeStruct((n_blocks * ROW_BLOCK, D_MODEL), jnp.float32),
        compiler_params=pltpu.CompilerParams(
            dimension_semantics=("arbitrary",), vmem_limit_bytes=VMEM_LIMIT),
        name="experts",
    )(block_e, n_used, next_e, group, xs, wg, bg, wu, bu, wd, bd)


def _combine_kernel(dest_ref, ys_ref, x1_ref, w_ref, gfin_ref, o_ref, buf_ref, sem):
    i = pl.program_id(0)
    tm = x1_ref.shape[0]

    def gather(tile):
        slot = lax.rem(tile, 2)

        def body(r, c):
            base = (tile * tm + r) * TOP_K
            for k in range(TOP_K):
                d = dest_ref[base + k]
                pltpu.make_async_copy(ys_ref.at[pl.ds(d, 1)], buf_ref.at[slot, k, pl.ds(r, 1)],
                                      sem.at[slot]).start()
            return c

        lax.fori_loop(0, tm, body, 0, unroll=ROW_DMA_UNROLL)

    @pl.when(i == 0)
    def _():
        gather(i)

    @pl.when(i + 1 < pl.num_programs(0))
    def _():
        gather(i + 1)

    slot = lax.rem(i, 2)
    _row_copy_wait(ys_ref.at[pl.ds(0, tm)], buf_ref.at[slot, 0], sem.at[slot], TOP_K)
    w = w_ref[...]
    y = x1_ref[...]
    for k in range(TOP_K):
        y = y + buf_ref[slot, k] * w[:, k:k + 1]
    o_ref[...] = _rms(y, gfin_ref[...])


def _combine(dest, ys, x1, top_w, gfin):
    t = x1.shape[0]
    tm = DISPATCH_TILE
    return pl.pallas_call(
        _combine_kernel,
        grid_spec=pltpu.PrefetchScalarGridSpec(
            num_scalar_prefetch=1,
            grid=(t // tm,),
            in_specs=[pl.BlockSpec(memory_space=pl.ANY),
                      pl.BlockSpec((tm, D_MODEL), lambda i, d: (i, 0)),
                      pl.BlockSpec((tm, TOP_K), lambda i, d: (i, 0)),
                      pl.BlockSpec((1, D_MODEL), lambda i, d: (0, 0))],
            out_specs=pl.BlockSpec((tm, D_MODEL), lambda i, d: (i, 0)),
            scratch_shapes=[pltpu.VMEM((2, TOP_K, tm, D_MODEL), jnp.float32),
                            pltpu.SemaphoreType.DMA((2,))],
        ),
        out_shape=jax.ShapeDtypeStruct((t, D_MODEL), jnp.float32),
        compiler_params=pltpu.CompilerParams(
            dimension_semantics=("arbitrary",), vmem_limit_bytes=VMEM_LIMIT),
        name="combine",
    )(dest, ys, x1, top_w, gfin)


def _rope_freq_lanes():
    def inv(rot):
        return ROPE_THETA ** (-jnp.arange(0, rot, 2, dtype=jnp.float32) / rot)
    lane = jnp.arange(LANES)
    f_mla = inv(MLA_ROPE)[lane % (MLA_ROPE // 2)]
    f_diff = inv(DA_ROT)[lane % (DA_ROT // 2)]
    freq = jnp.where(lane < MLA_ROPE, f_mla, jnp.where(lane < MLA_ROPE + DA_ROT, f_diff, 0.0))
    return freq.astype(jnp.float32)[None]


def kernel(x, positions, g_mix, w_in, lam_q1, lam_k1, lam_q2, lam_k2, g_subln, g_q, g_kv, w_uq, w_ukv, w_o_diff, w_o_mla, b_gates, w_out, g_ffn, w_router, b_router, w_gate, b_gate, w_up, b_up, w_down, b_down, g_final):
    batch, seq, d = x.shape
    t = batch * seq
    bf = jnp.bfloat16
    l = 0
    x2 = x.reshape(t, d)
    posf = positions.astype(jnp.float32).reshape(t, 1)

    w = w_in[l]
    win = w[:, :IN_KR].astype(bf)
    wkr = jnp.pad(w[:, IN_KR:IN_GATES], ((0, 0), (0, LANES - MLA_ROPE))).astype(bf)
    wgate = w[:, IN_GATES:].astype(bf)
    wuq = jnp.pad(w_uq[l].reshape(MLA_Q_LORA, MLA_HEADS, MLA_NOPE + MLA_ROPE),
                  ((0, 0), (0, 0), (0, MLA_QK_PAD - MLA_NOPE - MLA_ROPE))
                  ).reshape(MLA_Q_LORA, MLA_HEADS * MLA_QK_PAD).astype(bf)
    wukv = w_ukv[l].reshape(MLA_KV_LORA, MLA_HEADS, MLA_NOPE + MLA_V)
    wkn = wukv[:, :, :MLA_NOPE].reshape(MLA_KV_LORA, MLA_HEADS * MLA_NOPE).astype(bf)
    wvt = wukv[:, :, MLA_NOPE:].reshape(MLA_KV_LORA, MLA_HEADS * MLA_V).T.astype(bf)
    wvat = w[:, IN_OFF["va"]:IN_OFF["va"] + IN_W["va"]].T.astype(bf)
    freq = _rope_freq_lanes()

    qa, ka, vat, gates, qm, km, vmt = _proj(
        x2, posf, g_mix[l][None], win, wkr, wgate, b_gates[l][None],
        g_q[l][None], g_kv[l][None], wuq, wkn, wvat, wvt, freq)

    lamv = jnp.stack([lam_q1[l], lam_k1[l], lam_q2[l], lam_k2[l]]).astype(jnp.float32)
    oa = _dattn(qa, ka, vat, lamv, g_subln[l][None], batch, seq)
    ob = _mattn(qm, km, vmt, batch, seq)

    x1, h2, e_idx, top_w, rank, counts = _merge(
        x2, oa, ob, gates, w_o_diff[l].astype(bf), w_o_mla[l].astype(bf), w_out[l].astype(bf),
        g_ffn[l][None], w_router[l].T, b_router[l][:, None])

    bm = ROW_BLOCK
    n_blocks = (t * TOP_K) // bm + N_EXPERTS
    counts = counts[:, 0]
    padded = (counts + bm - 1) // bm * bm
    padded_end = jnp.cumsum(padded)
    padded_start = padded_end - padded
    n_used = (padded_end[-1] // bm).astype(jnp.int32)
    blk = jnp.minimum(jnp.arange(n_blocks, dtype=jnp.int32), n_used - 1)
    block_e = jnp.minimum(jnp.sum(padded_end[None, :] <= (blk * bm)[:, None], axis=1),
                          N_EXPERTS - 1).astype(jnp.int32)
    hot = e_idx[:, :, None] == jnp.arange(N_EXPERTS, dtype=jnp.int32)
    dest = (jnp.sum(jnp.where(hot, padded_start, 0), axis=-1) + rank).T.reshape(-1).astype(jnp.int32)

    n_used = n_used.reshape(1)
    xs = _dispatch(dest, (padded_start + counts).astype(jnp.int32), padded_end.astype(jnp.int32),
                   n_used, h2, n_blocks)
    later = block_e[None, :] > block_e[:, None]
    next_e = jnp.min(jnp.where(later, block_e[None, :], N_EXPERTS), axis=1)
    next_e = jnp.where(next_e == N_EXPERTS, -1, next_e).astype(jnp.int32)
    group = jnp.cumsum(jnp.concatenate([jnp.zeros((1,), jnp.int32),
                                        (block_e[1:] != block_e[:-1]).astype(jnp.int32)]))
    ys = _experts(block_e, n_used, next_e, group.astype(jnp.int32), xs, w_gate[l],
                  b_gate[l][:, None, :], w_up[l], b_up[l][:, None, :], w_down[l],
                  b_down[l][:, None, :], n_blocks)
    out = _combine(dest, ys, x1, top_w.T, g_final[None])
    return out.reshape(batch, seq, d)
```

```python
import math

import jax
import jax.numpy as jnp
from jax import lax
from jax.experimental import pallas as pl
from jax.experimental.pallas import tpu as pltpu

D_MODEL = 1024
ROPE_THETA = 500000.0
NORM_EPS = 1e-6
DA_HEADS = 8
DA_HEAD_DIM = 64
DA_ROT = DA_HEAD_DIM // 4
MLA_HEADS = 8
MLA_Q_LORA = 768
MLA_KV_LORA = 512
MLA_NOPE = 128
MLA_ROPE = 64
MLA_V = 128
N_EXPERTS = 32
TOP_K = 4
D_FF = 1024
SWIGLU_ALPHA = 1.702
SWIGLU_LIMIT = 7.0
LAM_INIT = 0.8 - 0.6 * math.exp(-0.3 * 0)

LANES = 128
SUBLANES = 8
PV_ROWS = LANES + 16
MLA_QK_PAD = 256
TOKEN_TILE = 256
DISPATCH_TILE = 512
Q_TILE = 256
SCORE_SLOTS = 3
KEY_CHUNK = 256
ROW_BLOCK = 256
BLOCKS_PER_STEP = 4
ROW_DMA_UNROLL = 8
VMEM_LIMIT = 56 * 1024 * 1024

LOG2_E = math.log2(math.e)

IN_W = {"qa": DA_HEADS * 2 * DA_HEAD_DIM, "ka": DA_HEADS * 2 * DA_HEAD_DIM,
        "va": DA_HEADS * 2 * DA_HEAD_DIM, "cq": MLA_Q_LORA, "ckv": MLA_KV_LORA}
IN_OFF = dict(zip(IN_W, [sum(list(IN_W.values())[:n]) for n in range(len(IN_W))]))
IN_KR = sum(IN_W.values())
IN_GATES = IN_KR + MLA_ROPE

_NT = (((1,), (1,)), ((), ()))


def _rms(x, g):
    return x * lax.rsqrt(jnp.mean(x * x, axis=-1, keepdims=True) + NORM_EPS) * g


def _dot(a, b):
    return jnp.dot(a, b, preferred_element_type=jnp.float32)


def _rope_tables(pos, freq):
    ang = pos * freq
    c = jnp.cos(ang)
    s = jnp.sin(ang)
    lane = lax.broadcasted_iota(jnp.int32, c.shape, 1)
    m_half, m_rot = MLA_ROPE // 2, MLA_ROPE
    t_mla = (jnp.where(lane < m_rot, c, 1.0),
             jnp.where(lane < m_half, -s, 0.0),
             jnp.where((lane >= m_half) & (lane < m_rot), s, 0.0))
    c_d = jnp.where(lane < m_rot, pltpu.roll(c, LANES // 2, 1), c)
    s_d = jnp.where(lane < m_rot, pltpu.roll(s, LANES // 2, 1), s)
    g = lane % DA_HEAD_DIM
    d_half, d_rot = DA_ROT // 2, DA_ROT
    t_diff = (jnp.where(g < d_rot, c_d, 1.0),
              jnp.where(g < d_half, -s_d, 0.0),
              jnp.where((g >= d_half) & (g < d_rot), s_d, 0.0))
    return t_diff, t_mla


def _rope_block(xb, tables, half):
    c, s_lo, s_hi = tables
    return (xb * c + pltpu.roll(xb, LANES - half, 1) * s_lo
            + pltpu.roll(xb, half, 1) * s_hi)


def _proj_kernel(x_ref, pos_ref, gmix_ref, win_ref, wkr_ref, wgate_ref, bg_ref, gq_ref, gkv_ref,
                 wuq_ref, wkn_ref,
                 wvat_ref, wvt_ref, freq_ref,
                 qa_ref, ka_ref, vat_ref, gates_ref, qm_ref, km_ref, vmt_ref):
    hb = _rms(x_ref[...], gmix_ref[...]).astype(jnp.bfloat16)
    tm = hb.shape[0]

    da_scale = DA_HEAD_DIM ** -0.5 * LOG2_E
    w_cols = lambda name: win_ref[:, IN_OFF[name]:IN_OFF[name] + IN_W[name]]
    mla_scale = (MLA_NOPE + MLA_ROPE) ** -0.5 * LOG2_E

    def in_cols(name, lo, width):
        return win_ref[:, IN_OFF[name] + lo:IN_OFF[name] + lo + width]

    n_groups = 4
    heads = DA_HEADS // n_groups

    cq = _rms(_dot(hb, w_cols("cq")), gq_ref[...]).astype(jnp.bfloat16)
    ck = _dot(hb, w_cols("ckv"))
    kr_in = _dot(hb, wkr_ref[...])
    td_parts, tmla_parts = [], []
    for g in range(n_groups):
        rows = slice(g * tm // n_groups, (g + 1) * tm // n_groups)
        pos = pos_ref[rows, :]
        t_diff, t_mla = _rope_tables(pos, freq_ref[...])
        td_parts.append(t_diff)
        tmla_parts.append(t_mla)
        cols = slice(g * heads * LANES, (g + 1) * heads * LANES)
        gw = 2 * D_MODEL // n_groups
        gcols = slice(g * gw, (g + 1) * gw)
        gates_ref[:, gcols] = jax.nn.sigmoid(
            _dot(hb, wgate_ref[:, gcols]) + bg_ref[:, gcols]).astype(jnp.bfloat16)
        vat_ref[cols, :] = lax.dot_general(wvat_ref[cols, :], hb, _NT,
                                           preferred_element_type=jnp.float32).astype(jnp.bfloat16)
    td = tuple(jnp.concatenate([p[k] for p in td_parts], axis=0) for k in range(3))
    tmla = tuple(jnp.concatenate([p[k] for p in tmla_parts], axis=0) for k in range(3))
    kr = _rope_block(kr_in, tmla, MLA_ROPE // 2).astype(jnp.bfloat16)
    ckv = _rms(ck, gkv_ref[...]).astype(jnp.bfloat16)

    for g in range(n_groups):
        zq = _dot(hb, in_cols("qa", g * heads * LANES, heads * LANES))
        for j in range(heads):
            dst = slice((g * heads + j) * LANES, (g * heads + j + 1) * LANES)
            blk = zq[:, j * LANES:(j + 1) * LANES]
            qa_ref[:, dst] = (_rope_block(blk, td, DA_ROT // 2) * da_scale).astype(jnp.bfloat16)

        qm = _dot(cq, wuq_ref[:, g * heads * MLA_QK_PAD:(g + 1) * heads * MLA_QK_PAD])
        for j in range(heads):
            h = g * heads + j
            lo = slice(j * MLA_QK_PAD, j * MLA_QK_PAD + LANES)
            hi = slice(j * MLA_QK_PAD + LANES, (j + 1) * MLA_QK_PAD)
            qm_ref[:, h * MLA_QK_PAD:h * MLA_QK_PAD + LANES] = (qm[:, lo] * mla_scale).astype(jnp.bfloat16)
            qm_ref[:, h * MLA_QK_PAD + LANES:(h + 1) * MLA_QK_PAD] = (
                _rope_block(qm[:, hi], tmla, MLA_ROPE // 2) * mla_scale).astype(jnp.bfloat16)

        zk = _dot(hb, in_cols("ka", g * heads * LANES, heads * LANES))
        for j in range(heads):
            dst = slice((g * heads + j) * LANES, (g * heads + j + 1) * LANES)
            blk = zk[:, j * LANES:(j + 1) * LANES]
            ka_ref[:, dst] = _rope_block(blk, td, DA_ROT // 2).astype(jnp.bfloat16)

        kn = _dot(ckv, wkn_ref[:, g * heads * LANES:(g + 1) * heads * LANES])
        for j in range(heads):
            h = g * heads + j
            km_ref[:, h * MLA_QK_PAD:h * MLA_QK_PAD + LANES] = kn[:, j * LANES:(j + 1) * LANES].astype(jnp.bfloat16)
            km_ref[:, h * MLA_QK_PAD + LANES:(h + 1) * MLA_QK_PAD] = kr

        cols = slice(g * heads * LANES, (g + 1) * heads * LANES)
        vmt_ref[cols, :] = lax.dot_general(wvt_ref[cols, :], ckv, _NT,
                                           preferred_element_type=jnp.float32).astype(jnp.bfloat16)


def _const_spec(shape):
    return pl.BlockSpec(shape, lambda i: (0,) * len(shape))


def _proj(x2, posf, gmix, win, wkr, wgate, bg, gq, gkv, wuq, wkn, wvat, wvt, freq):
    t = x2.shape[0]
    tm = TOKEN_TILE
    bf = jnp.bfloat16
    row = lambda w: pl.BlockSpec((tm, w), lambda i: (i, 0))
    col = lambda w: pl.BlockSpec((w, tm), lambda i: (0, i))
    consts = (gmix, win, wkr, wgate, bg, gq, gkv, wuq, wkn, wvat, wvt, freq)
    outs = ((row, D_MODEL), (row, D_MODEL), (col, D_MODEL), (row, 2 * D_MODEL),
            (row, MLA_HEADS * MLA_QK_PAD), (row, MLA_HEADS * MLA_QK_PAD), (col, MLA_HEADS * MLA_V))
    return pl.pallas_call(
        _proj_kernel,
        grid=(t // tm,),
        in_specs=[row(D_MODEL), row(1)] + [_const_spec(c.shape) for c in consts],
        out_specs=[kind(w) for kind, w in outs],
        out_shape=[jax.ShapeDtypeStruct((t, w) if kind is row else (w, t), bf) for kind, w in outs],
        compiler_params=pltpu.CompilerParams(
            dimension_semantics=("parallel",), vmem_limit_bytes=VMEM_LIMIT),
        name="proj",
    )(x2, posf, *consts)


def _pipelined_tiles(n_tiles, n_chunks, score_chunk, max_merge, max_store, max_load,
                     value_chunk, write_out):
    def fused(t_next, slot_next, t_cur, slot_cur):
        m_cur = None if t_cur is None else max_load(slot_cur)
        m_next, acc = None, None
        for c in range(n_chunks):
            if t_next is not None:
                m_next = max_merge(m_next, score_chunk(t_next, slot_next, c))
            if t_cur is not None:
                acc = value_chunk(slot_cur, c, m_cur, acc)
        if t_next is not None:
            max_store(slot_next, m_next)
        if t_cur is not None:
            write_out(t_cur, acc)

    fused(0, 0, None, None)
    fused(1, 1, None, None)
    n_triples = (n_tiles - 2) // SCORE_SLOTS

    def triple(j, carry):
        t = SCORE_SLOTS * j
        for i in range(SCORE_SLOTS):
            fused(t + i + 2, (i + 2) % SCORE_SLOTS, t + i, i)
        return carry

    lax.fori_loop(0, n_triples, triple, 0)
    for t in range(SCORE_SLOTS * n_triples, n_tiles):
        ahead = t + 2 if t + 2 < n_tiles else None
        fused(ahead, (t + 2) % SCORE_SLOTS, t, t % SCORE_SLOTS)


def _q_rows(t):
    if isinstance(t, int):
        return pl.ds(t * Q_TILE, Q_TILE)
    return pl.ds(pl.multiple_of(t * Q_TILE, Q_TILE), Q_TILE)


def _key_cols(c):
    return slice(c * KEY_CHUNK, (c + 1) * KEY_CHUNK)


def _fold_keys(x, op):
    out = x[:SUBLANES]
    for j in range(1, x.shape[0] // SUBLANES):
        out = op(out, x[j * SUBLANES:(j + 1) * SUBLANES])
    return out


def _fill_values_ext(vx_ref, vt_ref):
    width = vt_ref.shape[0]
    vx_ref[:width, :] = vt_ref[...]
    sub = lax.broadcasted_iota(jnp.int32, (vx_ref.shape[0] - width, vt_ref.shape[1]), 0)
    vx_ref[width:, :] = jnp.where(sub == 0, 1.0, 0.0).astype(vx_ref.dtype)


def _prob_values(s_ref, vx_ref, c, m, acc):
    p = jnp.exp2(s_ref[_key_cols(c), :] - m).astype(jnp.bfloat16)
    part = _dot(vx_ref[:, _key_cols(c)], p)
    return part if acc is None else acc + part


def _dattn_kernel(q_ref, k_ref, vt_ref, lam_ref, gsub_ref, o_ref,
                  s0_ref, s1_ref, s2_ref, m0_ref, m1_ref, m2_ref, vx_ref, qm_ref):
    s_refs, m_refs = (s0_ref, s1_ref, s2_ref), (m0_ref, m1_ref, m2_ref)
    _fill_values_ext(vx_ref, vt_ref)
    width = vt_ref.shape[0]
    lamv = lam_ref[...]
    lam = (jnp.exp(jnp.sum(lamv[0:1] * lamv[1:2], axis=-1, keepdims=True))
           - jnp.exp(jnp.sum(lamv[2:3] * lamv[3:4], axis=-1, keepdims=True)) + LAM_INIT)
    q_all = q_ref[...]
    lane = lax.broadcasted_iota(jnp.int32, q_all.shape, 1)
    qm_ref[0] = jnp.where(lane < DA_HEAD_DIM, q_all, jnp.zeros_like(q_all))
    qm_ref[1] = jnp.where(lane >= DA_HEAD_DIM, q_all, jnp.zeros_like(q_all))

    def score_chunk(t, slot, c):
        k = k_ref[_key_cols(c), :]
        sc0 = lax.dot_general(k, qm_ref[0, _q_rows(t), :], _NT, preferred_element_type=jnp.float32)
        sc1 = lax.dot_general(k, qm_ref[1, _q_rows(t), :], _NT, preferred_element_type=jnp.float32)
        s_refs[slot][0, _key_cols(c), :] = sc0
        s_refs[slot][1, _key_cols(c), :] = sc1
        return _fold_keys(sc0, jnp.maximum), _fold_keys(sc1, jnp.maximum)

    def max_merge(m, mc):
        return mc if m is None else (jnp.maximum(m[0], mc[0]), jnp.maximum(m[1], mc[1]))

    def max_store(slot, m):
        m_refs[slot][0] = jnp.max(m[0], axis=0, keepdims=True)
        m_refs[slot][1] = jnp.max(m[1], axis=0, keepdims=True)

    def max_load(slot):
        return m_refs[slot][0], m_refs[slot][1]

    def value_chunk(slot, c, m, acc):
        acc0, acc1 = (None, None) if acc is None else acc
        return (_prob_values(s_refs[slot].at[0], vx_ref, c, m[0], acc0),
                _prob_values(s_refs[slot].at[1], vx_ref, c, m[1], acc1))

    def write_out(t, acc):
        ox0, ox1 = acc
        ot = (ox0[:width] * (1.0 / ox0[width:width + 1])
              - ox1[:width] * (lam / ox1[width:width + 1]))
        o = ot.T
        o_ref[_q_rows(t), :] = (_rms(o, gsub_ref[...]) * (1.0 - LAM_INIT)).astype(o_ref.dtype)

    _pipelined_tiles(q_ref.shape[0] // Q_TILE, k_ref.shape[0] // KEY_CHUNK,
                     score_chunk, max_merge, max_store, max_load, value_chunk, write_out)


def _dattn(qa, ka, vat, lamv, gsub, batch, seq):
    head = lambda b, h: (b, h)
    return pl.pallas_call(
        _dattn_kernel,
        grid=(batch, DA_HEADS),
        in_specs=[
            pl.BlockSpec((seq, LANES), head),
            pl.BlockSpec((seq, LANES), head),
            pl.BlockSpec((LANES, seq), lambda b, h: (h, b)),
            pl.BlockSpec(lamv.shape, lambda b, h: (0, 0)),
            pl.BlockSpec(gsub.shape, lambda b, h: (0, 0)),
        ],
        out_specs=pl.BlockSpec((seq, LANES), head),
        out_shape=jax.ShapeDtypeStruct(qa.shape, jnp.bfloat16),
        scratch_shapes=[pltpu.VMEM((2, seq, Q_TILE), jnp.float32)] * SCORE_SLOTS
        + [pltpu.VMEM((2, 1, Q_TILE), jnp.float32)] * SCORE_SLOTS
        + [pltpu.VMEM((PV_ROWS, seq), jnp.bfloat16), pltpu.VMEM((2, seq, LANES), jnp.bfloat16)],
        compiler_params=pltpu.CompilerParams(
            dimension_semantics=("parallel", "parallel"), vmem_limit_bytes=VMEM_LIMIT),
        name="dattn",
    )(qa, ka, vat, lamv, gsub)


def _mattn_kernel(q_ref, k_ref, vt_ref, o_ref, s0_ref, s1_ref, s2_ref, m0_ref, m1_ref, m2_ref,
                  vx_ref):
    s_refs, m_refs = (s0_ref, s1_ref, s2_ref), (m0_ref, m1_ref, m2_ref)
    _fill_values_ext(vx_ref, vt_ref)
    width = vt_ref.shape[0]

    def score_chunk(t, slot, c):
        sc = lax.dot_general(k_ref[_key_cols(c), :], q_ref[_q_rows(t), :], _NT,
                             preferred_element_type=jnp.float32)
        s_refs[slot][_key_cols(c), :] = sc
        return _fold_keys(sc, jnp.maximum)

    def max_merge(m, mc):
        return mc if m is None else jnp.maximum(m, mc)

    def max_store(slot, m):
        m_refs[slot][...] = jnp.max(m, axis=0, keepdims=True)

    def max_load(slot):
        return m_refs[slot][...]

    def value_chunk(slot, c, m, acc):
        return _prob_values(s_refs[slot], vx_ref, c, m, acc)

    def write_out(t, ox):
        ot = ox[:width] * (1.0 / ox[width:width + 1])
        o_ref[_q_rows(t), :] = ot.T.astype(o_ref.dtype)

    _pipelined_tiles(q_ref.shape[0] // Q_TILE, k_ref.shape[0] // KEY_CHUNK,
                     score_chunk, max_merge, max_store, max_load, value_chunk, write_out)


def _mattn(qm, km, vmt, batch, seq):
    head = lambda b, h: (b, h)
    return pl.pallas_call(
        _mattn_kernel,
        grid=(batch, MLA_HEADS),
        in_specs=[
            pl.BlockSpec((seq, MLA_QK_PAD), head),
            pl.BlockSpec((seq, MLA_QK_PAD), head),
            pl.BlockSpec((MLA_V, seq), lambda b, h: (h, b)),
        ],
        out_specs=pl.BlockSpec((seq, MLA_V), head),
        out_shape=jax.ShapeDtypeStruct((batch * seq, MLA_HEADS * MLA_V), jnp.bfloat16),
        scratch_shapes=[pltpu.VMEM((seq, Q_TILE), jnp.float32)] * SCORE_SLOTS
        + [pltpu.VMEM((1, Q_TILE), jnp.float32)] * SCORE_SLOTS
        + [pltpu.VMEM((PV_ROWS, seq), jnp.bfloat16)],
        compiler_params=pltpu.CompilerParams(
            dimension_semantics=("parallel", "parallel"), vmem_limit_bytes=VMEM_LIMIT),
        name="mattn",
    )(qm, km, vmt)


def _rows(parts, dtype):
    sub = lax.broadcasted_iota(jnp.int32, (len(parts), parts[0].shape[1]), 0)
    out = jnp.zeros(sub.shape, dtype)
    for k, p in enumerate(parts):
        out = jnp.where(sub == k, p.astype(dtype), out)
    return out


def _merge_kernel(x_ref, oa_ref, ob_ref, gates_ref, woa_ref, wob_ref, wout_ref, gffn_ref,
                  wrt_ref, brt_ref,
                  x1_ref, h2_ref, idx_ref, w_ref, rank_ref, counts_ref, carry_ref, hprev_ref):
    i = pl.program_id(0)

    @pl.when(i == 0)
    def _():
        carry_ref[...] = jnp.zeros_like(carry_ref)
        hprev_ref[...] = jnp.zeros_like(hprev_ref)

    gates = gates_ref[...].astype(jnp.float32)
    half = D_MODEL // 2

    logits = lax.dot_general(wrt_ref[...], hprev_ref[...], _NT, precision=lax.Precision.HIGHEST,
                             preferred_element_type=jnp.float32) + brt_ref[...]
    tm = logits.shape[1]
    sub = lax.broadcasted_iota(jnp.int32, logits.shape, 0)
    vals, idxs, hots = [], [], []
    l = logits

    def pick(l):
        m = jnp.max(l, axis=0, keepdims=True)
        idx = jnp.min(jnp.where(l == m, sub, N_EXPERTS), axis=0, keepdims=True)
        hot = sub == idx
        vals.append(m)
        idxs.append(idx)
        hots.append(hot)
        return jnp.where(hot, -jnp.inf, l)

    ya_lo = gates[:, :half] * _dot(oa_ref[...], woa_ref[:, :half])
    l = pick(l)
    ya_hi = gates[:, half:D_MODEL] * _dot(oa_ref[...], woa_ref[:, half:])
    l = pick(l)
    yb_lo = gates[:, D_MODEL:D_MODEL + half] * _dot(ob_ref[...], wob_ref[:, :half])
    l = pick(l)
    yb_hi = gates[:, D_MODEL + half:] * _dot(ob_ref[...], wob_ref[:, half:])
    l = pick(l)
    merged = jnp.concatenate([ya_lo + yb_lo, ya_hi + yb_hi], axis=1).astype(jnp.bfloat16)

    es = [jnp.exp(v - vals[0]) for v in vals]
    den = es[0] + es[1] + es[2] + es[3]
    w_ref[...] = _rows([e / den for e in es], jnp.float32)
    idx_ref[...] = _rows(idxs, jnp.int32)
    x1_lo = x_ref[:, :half] + _dot(merged, wout_ref[:, :half])

    chosen = (hots[0] | hots[1] | hots[2] | hots[3]).astype(jnp.float32)
    r_i = lax.broadcasted_iota(jnp.int32, (tm, tm), 0)
    c_i = lax.broadcasted_iota(jnp.int32, (tm, tm), 1)
    earlier = (r_i < c_i).astype(jnp.bfloat16)
    prefix = _dot(chosen.astype(jnp.bfloat16), earlier) + carry_ref[...]
    x1_hi = x_ref[:, half:] + _dot(merged, wout_ref[:, half:])
    ranks = [jnp.sum(jnp.where(h, prefix, 0.0), axis=0, keepdims=True) for h in hots]
    rank_ref[...] = _rows(ranks, jnp.int32)
    live = (i > 0).astype(jnp.float32)
    carry = carry_ref[...] + live * jnp.sum(chosen, axis=1, keepdims=True)
    carry_ref[...] = carry
    counts_ref[...] = carry.astype(jnp.int32)

    x1 = jnp.concatenate([x1_lo, x1_hi], axis=1)
    x1_ref[...] = x1
    h2 = _rms(x1, gffn_ref[...])
    h2_ref[...] = h2
    hprev_ref[...] = h2


def _merge(x2, oa, ob, gates, woa, wob, wout, gffn, wrt, brt):
    t = x2.shape[0]
    tm = TOKEN_TILE
    n = t // tm
    row = lambda w: pl.BlockSpec((tm, w), lambda i: (jnp.minimum(i, n - 1), 0))
    col = lambda: pl.BlockSpec((TOP_K, tm), lambda i: (0, jnp.maximum(i - 1, 0)))
    consts = (woa, wob, wout, gffn, wrt, brt)
    return pl.pallas_call(
        _merge_kernel,
        grid=(n + 1,),
        in_specs=[row(D_MODEL), row(D_MODEL), row(D_MODEL), row(2 * D_MODEL)]
        + [_const_spec(c.shape) for c in consts],
        out_specs=[row(D_MODEL), row(D_MODEL), col(), col(), col(),
                   _const_spec((N_EXPERTS, 1))],
        out_shape=[
            jax.ShapeDtypeStruct((t, D_MODEL), jnp.float32),
            jax.ShapeDtypeStruct((t, D_MODEL), jnp.float32),
            jax.ShapeDtypeStruct((TOP_K, t), jnp.int32),
            jax.ShapeDtypeStruct((TOP_K, t), jnp.float32),
            jax.ShapeDtypeStruct((TOP_K, t), jnp.int32),
            jax.ShapeDtypeStruct((N_EXPERTS, 1), jnp.int32),
        ],
        scratch_shapes=[pltpu.VMEM((N_EXPERTS, 1), jnp.float32),
                        pltpu.VMEM((tm, D_MODEL), jnp.float32)],
        compiler_params=pltpu.CompilerParams(
            dimension_semantics=("arbitrary",), vmem_limit_bytes=VMEM_LIMIT),
        name="merge",
    )(x2, oa, ob, gates, *consts)


def _row_copy_wait(src_rows_ref, dst_rows_ref, sem, n):
    for _ in range(n):
        pltpu.make_async_copy(src_rows_ref, dst_rows_ref, sem).wait()


def _dispatch_kernel(dest_ref, pad_lo_ref, pad_hi_ref, nu_ref, h2_ref, xs_ref, zero_ref, sem, zsem):
    i = pl.program_id(0)
    tm = h2_ref.shape[0]
    bm = zero_ref.shape[0]
    n_blocks = xs_ref.shape[0] // bm

    @pl.when(i == 0)
    def _():
        zero_ref[...] = jnp.zeros_like(zero_ref)

        def pad_rows(fn):
            def per_expert(e, c):
                return lax.fori_loop(pad_lo_ref[e], pad_hi_ref[e], fn, c)
            lax.fori_loop(0, N_EXPERTS, per_expert, 0)

        def row_copy(j):
            return pltpu.make_async_copy(zero_ref.at[pl.ds(0, 1)], xs_ref.at[pl.ds(j, 1)], zsem)

        def blk_copy(b):
            return pltpu.make_async_copy(zero_ref, xs_ref.at[pl.ds(pl.multiple_of(b * bm, bm), bm)], zsem)

        def start_row(j, c):
            row_copy(j).start()
            return c

        def wait_row(j, c):
            row_copy(j).wait()
            return c

        def start_blk(b, c):
            blk_copy(b).start()
            return c

        def wait_blk(b, c):
            blk_copy(b).wait()
            return c

        pad_rows(start_row)
        lax.fori_loop(nu_ref[0], n_blocks, start_blk, 0)
        pad_rows(wait_row)
        lax.fori_loop(nu_ref[0], n_blocks, wait_blk, 0)

    def body(r, c):
        base = (i * tm + r) * TOP_K
        for k in range(TOP_K):
            d = dest_ref[base + k]
            pltpu.make_async_copy(h2_ref.at[pl.ds(r, 1)], xs_ref.at[pl.ds(d, 1)], sem).start(
                priority=k % 2)
        return c

    lax.fori_loop(0, tm, body, 0, unroll=ROW_DMA_UNROLL)
    _row_copy_wait(h2_ref, xs_ref.at[pl.ds(0, tm)], sem, TOP_K)


def _dispatch(dest, pad_lo, pad_hi, n_used, h2, n_blocks):
    t = h2.shape[0]
    tm = DISPATCH_TILE
    bm = ROW_BLOCK
    return pl.pallas_call(
        _dispatch_kernel,
        grid_spec=pltpu.PrefetchScalarGridSpec(
            num_scalar_prefetch=4,
            grid=(t // tm,),
            in_specs=[pl.BlockSpec((tm, D_MODEL), lambda i, *_: (i, 0))],
            out_specs=pl.BlockSpec(memory_space=pl.ANY),
            scratch_shapes=[pltpu.VMEM((bm, D_MODEL), jnp.float32),
                            pltpu.SemaphoreType.DMA(()), pltpu.SemaphoreType.DMA(())],
        ),
        out_shape=jax.ShapeDtypeStruct((n_blocks * bm, D_MODEL), jnp.float32),
        compiler_params=pltpu.CompilerParams(
            dimension_semantics=("arbitrary",), vmem_limit_bytes=VMEM_LIMIT),
        name="dispatch",
    )(dest, pad_lo, pad_hi, n_used, h2)


def _experts_kernel(be_ref, nu_ref, nxt_ref, grp_ref, xs_ref, wg_ref, bg_ref, wu_ref, bu_ref, wd_ref,
                    bd_ref, ys_ref, stage_ref, wbf_ref, sem):
    w_hbm = (wg_ref, wu_ref, wd_ref)
    bm = ROW_BLOCK

    def weight_copies(expert, slot):
        return [pltpu.make_async_copy(w.at[expert], stage_ref.at[slot, j], sem.at[slot, j])
                for j, w in enumerate(w_hbm)]

    def row_block(b, rows):
        e = be_ref[b]
        slot = lax.rem(grp_ref[b], 2)

        @pl.when(b == 0)
        def _():
            for cp in weight_copies(e, slot):
                cp.start()

        @pl.when(jnp.logical_or(b == 0, e != be_ref[jnp.maximum(b - 1, 0)]))
        def _():
            for j, cp in enumerate(weight_copies(e, slot)):
                cp.wait()
                wbf_ref[j] = stage_ref[slot, j].astype(jnp.bfloat16)

            @pl.when(nxt_ref[b] >= 0)
            def _():
                for cp in weight_copies(nxt_ref[b], 1 - slot):
                    cp.start()

        @pl.when(b >= nu_ref[0])
        def _():
            ys_ref[rows, :] = jnp.zeros((bm, ys_ref.shape[1]), ys_ref.dtype)

        @pl.when(b < nu_ref[0])
        def _():
            xb = xs_ref[rows, :].astype(jnp.bfloat16)
            gate = jnp.minimum(_dot(xb, wbf_ref[0]) + bg_ref[e], SWIGLU_LIMIT)
            up = jnp.clip(_dot(xb, wbf_ref[1]) + bu_ref[e], -SWIGLU_LIMIT, SWIGLU_LIMIT)
            act = (up + 1.0) * (gate * jax.nn.sigmoid(SWIGLU_ALPHA * gate))
            ys_ref[rows, :] = _dot(act.astype(jnp.bfloat16), wbf_ref[2]) + bd_ref[e]

    for j in range(BLOCKS_PER_STEP):
        row_block(pl.program_id(0) * BLOCKS_PER_STEP + j, slice(j * bm, (j + 1) * bm))


def _experts(block_e, n_used, next_e, group, xs, wg, bg, wu, bu, wd, bd, n_blocks):
    rows_per_step = ROW_BLOCK * BLOCKS_PER_STEP
    rows = lambda s, be, nu, *_: (jnp.maximum(jnp.minimum(s, (nu[0] - 1) // BLOCKS_PER_STEP), 0), 0)
    hbm = lambda: pl.BlockSpec(memory_space=pl.ANY)
    bias = lambda: pl.BlockSpec((N_EXPERTS, 1, D_FF), lambda s, *_: (0, 0, 0))
    return pl.pallas_call(
        _experts_kernel,
        grid_spec=pltpu.PrefetchScalarGridSpec(
            num_scalar_prefetch=4,
            grid=(n_blocks // BLOCKS_PER_STEP,),
            in_specs=[pl.BlockSpec((rows_per_step, D_MODEL), rows), hbm(), bias(), hbm(), bias(),
                      hbm(), bias()],
            out_specs=pl.BlockSpec((rows_per_step, D_MODEL), lambda s, *_: (s, 0)),
            scratch_shapes=[pltpu.VMEM((2, 3, D_MODEL, D_FF), jnp.float32),
                            pltpu.VMEM((3, D_MODEL, D_FF), jnp.bfloat16),
                            pltpu.SemaphoreType.DMA((2, 3))],
        ),
        out_shape=jax.ShapeDtypeStruct((n_blocks * ROW_BLOCK, D_MODEL), jnp.float32),
        compiler_params=pltpu.CompilerParams(
            dimension_semantics=("arbitrary",), vmem_limit_bytes=VMEM_LIMIT),
        name="experts",
    )(block_e, n_used, next_e, group, xs, wg, bg, wu, bu, wd, bd)


def _combine_kernel(dest_ref, ys_ref, x1_ref, w_ref, gfin_ref, o_ref, buf_ref, sem):
    i = pl.program_id(0)
    tm = x1_ref.shape[0]

    def gather(tile):
        slot = lax.rem(tile, 2)

        def body(r, c):
            base = (tile * tm + r) * TOP_K
            for k in range(TOP_K):
                d = dest_ref[base + k]
                pltpu.make_async_copy(ys_ref.at[pl.ds(d, 1)], buf_ref.at[slot, k, pl.ds(r, 1)],
                                      sem.at[slot]).start(priority=k % 2)
            return c

        lax.fori_loop(0, tm, body, 0, unroll=ROW_DMA_UNROLL)

    @pl.when(i == 0)
    def _():
        gather(i)

    @pl.when(i + 1 < pl.num_programs(0))
    def _():
        gather(i + 1)

    slot = lax.rem(i, 2)
    _row_copy_wait(ys_ref.at[pl.ds(0, tm)], buf_ref.at[slot, 0], sem.at[slot], TOP_K)
    w = w_ref[...]
    y = x1_ref[...]
    for k in range(TOP_K):
        y = y + buf_ref[slot, k] * w[:, k:k + 1]
    o_ref[...] = _rms(y, gfin_ref[...])


def _combine(dest, ys, x1, top_w, gfin):
    t = x1.shape[0]
    tm = DISPATCH_TILE
    return pl.pallas_call(
        _combine_kernel,
        grid_spec=pltpu.PrefetchScalarGridSpec(
            num_scalar_prefetch=1,
            grid=(t // tm,),
            in_specs=[pl.BlockSpec(memory_space=pl.ANY),
                      pl.BlockSpec((tm, D_MODEL), lambda i, d: (i, 0)),
                      pl.BlockSpec((tm, TOP_K), lambda i, d: (i, 0)),
                      pl.BlockSpec((1, D_MODEL), lambda i, d: (0, 0))],
            out_specs=pl.BlockSpec((tm, D_MODEL), lambda i, d: (i, 0)),
            scratch_shapes=[pltpu.VMEM((2, TOP_K, tm, D_MODEL), jnp.float32),
                            pltpu.SemaphoreType.DMA((2,))],
        ),
        out_shape=jax.ShapeDtypeStruct((t, D_MODEL), jnp.float32),
        compiler_params=pltpu.CompilerParams(
            dimension_semantics=("arbitrary",), vmem_limit_bytes=VMEM_LIMIT),
        name="combine",
    )(dest, ys, x1, top_w, gfin)


def _rope_freq_lanes():
    def inv(rot):
        return ROPE_THETA ** (-jnp.arange(0, rot, 2, dtype=jnp.float32) / rot)
    lane = jnp.arange(LANES)
    f_mla = inv(MLA_ROPE)[lane % (MLA_ROPE // 2)]
    f_diff = inv(DA_ROT)[lane % (DA_ROT // 2)]
    freq = jnp.where(lane < MLA_ROPE, f_mla, jnp.where(lane < MLA_ROPE + DA_ROT, f_diff, 0.0))
    return freq.astype(jnp.float32)[None]


def kernel(x, positions, g_mix, w_in, lam_q1, lam_k1, lam_q2, lam_k2, g_subln, g_q, g_kv, w_uq, w_ukv, w_o_diff, w_o_mla, b_gates, w_out, g_ffn, w_router, b_router, w_gate, b_gate, w_up, b_up, w_down, b_down, g_final):
    batch, seq, d = x.shape
    t = batch * seq
    bf = jnp.bfloat16
    l = 0
    x2 = x.reshape(t, d)
    posf = positions.astype(jnp.float32).reshape(t, 1)

    w = w_in[l]
    win = w[:, :IN_KR].astype(bf)
    wkr = jnp.pad(w[:, IN_KR:IN_GATES], ((0, 0), (0, LANES - MLA_ROPE))).astype(bf)
    wgate = w[:, IN_GATES:].astype(bf)
    wuq = jnp.pad(w_uq[l].reshape(MLA_Q_LORA, MLA_HEADS, MLA_NOPE + MLA_ROPE),
                  ((0, 0), (0, 0), (0, MLA_QK_PAD - MLA_NOPE - MLA_ROPE))
                  ).reshape(MLA_Q_LORA, MLA_HEADS * MLA_QK_PAD).astype(bf)
    wukv = w_ukv[l].reshape(MLA_KV_LORA, MLA_HEADS, MLA_NOPE + MLA_V)
    wkn = wukv[:, :, :MLA_NOPE].reshape(MLA_KV_LORA, MLA_HEADS * MLA_NOPE).astype(bf)
    wvt = wukv[:, :, MLA_NOPE:].reshape(MLA_KV_LORA, MLA_HEADS * MLA_V).T.astype(bf)
    wvat = w[:, IN_OFF["va"]:IN_OFF["va"] + IN_W["va"]].T.astype(bf)
    freq = _rope_freq_lanes()

    qa, ka, vat, gates, qm, km, vmt = _proj(
        x2, posf, g_mix[l][None], win, wkr, wgate, b_gates[l][None],
        g_q[l][None], g_kv[l][None], wuq, wkn, wvat, wvt, freq)

    lamv = jnp.stack([lam_q1[l], lam_k1[l], lam_q2[l], lam_k2[l]]).astype(jnp.float32)
    oa = _dattn(qa, ka, vat, lamv, g_subln[l][None], batch, seq)
    ob = _mattn(qm, km, vmt, batch, seq)

    x1, h2, e_idx, top_w, rank, counts = _merge(
        x2, oa, ob, gates, w_o_diff[l].astype(bf), w_o_mla[l].astype(bf), w_out[l].astype(bf),
        g_ffn[l][None], w_router[l].T, b_router[l][:, None])

    bm = ROW_BLOCK
    n_blocks = (t * TOP_K) // bm + N_EXPERTS
    counts = counts[:, 0]
    padded = (counts + bm - 1) // bm * bm
    padded_end = jnp.cumsum(padded)
    padded_start = padded_end - padded
    n_used = (padded_end[-1] // bm).astype(jnp.int32)
    blk = jnp.minimum(jnp.arange(n_blocks, dtype=jnp.int32), n_used - 1)
    block_e = jnp.minimum(jnp.sum(padded_end[None, :] <= (blk * bm)[:, None], axis=1),
                          N_EXPERTS - 1).astype(jnp.int32)
    hot = e_idx[:, :, None] == jnp.arange(N_EXPERTS, dtype=jnp.int32)
    dest = (jnp.sum(jnp.where(hot, padded_start, 0), axis=-1) + rank).T.reshape(-1).astype(jnp.int32)

    n_used = n_used.reshape(1)
    xs = _dispatch(dest, (padded_start + counts).astype(jnp.int32), padded_end.astype(jnp.int32),
                   n_used, h2, n_blocks)
    later = block_e[None, :] > block_e[:, None]
    next_e = jnp.min(jnp.where(later, block_e[None, :], N_EXPERTS), axis=1)
    next_e = jnp.where(next_e == N_EXPERTS, -1, next_e).astype(jnp.int32)
    group = jnp.cumsum(jnp.concatenate([jnp.zeros((1,), jnp.int32),
                                        (block_e[1:] != block_e[:-1]).astype(jnp.int32)]))
    ys = _experts(block_e, n_used, next_e, group.astype(jnp.int32), xs, w_gate[l],
                  b_gate[l][:, None, :], w_up[l], b_up[l][:, None, :], w_down[l],
                  b_down[l][:, None, :], n_blocks)
    out = _combine(dest, ys, x1, top_w.T, g_final[None])
    return out.reshape(batch, seq, d)
```

```python
import math

import jax
import jax.numpy as jnp
from jax import lax
from jax.experimental import pallas as pl
from jax.experimental.pallas import tpu as pltpu

D_MODEL = 1024
ROPE_THETA = 500000.0
NORM_EPS = 1e-6
DA_HEADS = 8
DA_HEAD_DIM = 64
DA_ROT = DA_HEAD_DIM // 4
MLA_HEADS = 8
MLA_Q_LORA = 768
MLA_KV_LORA = 512
MLA_NOPE = 128
MLA_ROPE = 64
MLA_V = 128
N_EXPERTS = 32
TOP_K = 4
D_FF = 1024
SWIGLU_ALPHA = 1.702
SWIGLU_LIMIT = 7.0
LAM_INIT = 0.8 - 0.6 * math.exp(-0.3 * 0)

LANES = 128
SUBLANES = 8
PV_ROWS = LANES + 16
MLA_QK_PAD = 256
TOKEN_TILE = 256
DISPATCH_TILE = 512
Q_TILE = 256
ATTN_HEADS = 2
SCORE_SLOTS = 3
KEY_CHUNK = 256
ROW_BLOCK = 256
BLOCKS_PER_STEP = 4
ROW_DMA_UNROLL = 8
VMEM_LIMIT = 56 * 1024 * 1024

LOG2_E = math.log2(math.e)

IN_W = {"qa": DA_HEADS * 2 * DA_HEAD_DIM, "ka": DA_HEADS * 2 * DA_HEAD_DIM,
        "va": DA_HEADS * 2 * DA_HEAD_DIM, "cq": MLA_Q_LORA, "ckv": MLA_KV_LORA}
IN_OFF = dict(zip(IN_W, [sum(list(IN_W.values())[:n]) for n in range(len(IN_W))]))
IN_KR = sum(IN_W.values())
IN_GATES = IN_KR + MLA_ROPE

_NT = (((1,), (1,)), ((), ()))


def _rms(x, g):
    return x * lax.rsqrt(jnp.mean(x * x, axis=-1, keepdims=True) + NORM_EPS) * g


def _dot(a, b):
    return jnp.dot(a, b, preferred_element_type=jnp.float32)


def _rope_tables(pos, freq):
    ang = pos * freq
    c = jnp.cos(ang)
    s = jnp.sin(ang)
    lane = lax.broadcasted_iota(jnp.int32, c.shape, 1)
    m_half, m_rot = MLA_ROPE // 2, MLA_ROPE
    t_mla = (jnp.where(lane < m_rot, c, 1.0),
             jnp.where(lane < m_half, -s, 0.0),
             jnp.where((lane >= m_half) & (lane < m_rot), s, 0.0))
    c_d = jnp.where(lane < m_rot, pltpu.roll(c, LANES // 2, 1), c)
    s_d = jnp.where(lane < m_rot, pltpu.roll(s, LANES // 2, 1), s)
    g = lane % DA_HEAD_DIM
    d_half, d_rot = DA_ROT // 2, DA_ROT
    t_diff = (jnp.where(g < d_rot, c_d, 1.0),
              jnp.where(g < d_half, -s_d, 0.0),
              jnp.where((g >= d_half) & (g < d_rot), s_d, 0.0))
    return t_diff, t_mla


def _rope_block(xb, tables, half):
    c, s_lo, s_hi = tables
    return (xb * c + pltpu.roll(xb, LANES - half, 1) * s_lo
            + pltpu.roll(xb, half, 1) * s_hi)


def _proj_kernel(x_ref, pos_ref, gmix_ref, win_ref, wkr_ref, wgate_ref, bg_ref, gq_ref, gkv_ref,
                 wuq_ref, wkn_ref,
                 wvat_ref, wvt_ref, freq_ref,
                 qa_ref, ka_ref, vat_ref, gates_ref, qm_ref, km_ref, vmt_ref):
    hb = _rms(x_ref[...], gmix_ref[...]).astype(jnp.bfloat16)
    tm = hb.shape[0]

    da_scale = DA_HEAD_DIM ** -0.5 * LOG2_E
    w_cols = lambda name: win_ref[:, IN_OFF[name]:IN_OFF[name] + IN_W[name]]
    mla_scale = (MLA_NOPE + MLA_ROPE) ** -0.5 * LOG2_E

    def in_cols(name, lo, width):
        return win_ref[:, IN_OFF[name] + lo:IN_OFF[name] + lo + width]

    n_groups = 4
    heads = DA_HEADS // n_groups

    cq = _rms(_dot(hb, w_cols("cq")), gq_ref[...]).astype(jnp.bfloat16)
    ck = _dot(hb, w_cols("ckv"))
    kr_in = _dot(hb, wkr_ref[...])
    td_parts, tmla_parts = [], []
    for g in range(n_groups):
        rows = slice(g * tm // n_groups, (g + 1) * tm // n_groups)
        pos = pos_ref[rows, :]
        t_diff, t_mla = _rope_tables(pos, freq_ref[...])
        td_parts.append(t_diff)
        tmla_parts.append(t_mla)
        cols = slice(g * heads * LANES, (g + 1) * heads * LANES)
        gw = 2 * D_MODEL // n_groups
        gcols = slice(g * gw, (g + 1) * gw)
        gates_ref[:, gcols] = jax.nn.sigmoid(
            _dot(hb, wgate_ref[:, gcols]) + bg_ref[:, gcols]).astype(jnp.bfloat16)
        vat_ref[cols, :] = lax.dot_general(wvat_ref[cols, :], hb, _NT,
                                           preferred_element_type=jnp.float32).astype(jnp.bfloat16)
    td = tuple(jnp.concatenate([p[k] for p in td_parts], axis=0) for k in range(3))
    tmla = tuple(jnp.concatenate([p[k] for p in tmla_parts], axis=0) for k in range(3))
    kr = _rope_block(kr_in, tmla, MLA_ROPE // 2).astype(jnp.bfloat16)
    ckv = _rms(ck, gkv_ref[...]).astype(jnp.bfloat16)

    for g in range(n_groups):
        zq = _dot(hb, in_cols("qa", g * heads * LANES, heads * LANES))
        for j in range(heads):
            dst = slice((g * heads + j) * LANES, (g * heads + j + 1) * LANES)
            blk = zq[:, j * LANES:(j + 1) * LANES]
            qa_ref[:, dst] = (_rope_block(blk, td, DA_ROT // 2) * da_scale).astype(jnp.bfloat16)

        qm = _dot(cq, wuq_ref[:, g * heads * MLA_QK_PAD:(g + 1) * heads * MLA_QK_PAD])
        for j in range(heads):
            h = g * heads + j
            lo = slice(j * MLA_QK_PAD, j * MLA_QK_PAD + LANES)
            hi = slice(j * MLA_QK_PAD + LANES, (j + 1) * MLA_QK_PAD)
            qm_ref[:, h * MLA_QK_PAD:h * MLA_QK_PAD + LANES] = (qm[:, lo] * mla_scale).astype(jnp.bfloat16)
            qm_ref[:, h * MLA_QK_PAD + LANES:(h + 1) * MLA_QK_PAD] = (
                _rope_block(qm[:, hi], tmla, MLA_ROPE // 2) * mla_scale).astype(jnp.bfloat16)

        zk = _dot(hb, in_cols("ka", g * heads * LANES, heads * LANES))
        for j in range(heads):
            dst = slice((g * heads + j) * LANES, (g * heads + j + 1) * LANES)
            blk = zk[:, j * LANES:(j + 1) * LANES]
            ka_ref[:, dst] = _rope_block(blk, td, DA_ROT // 2).astype(jnp.bfloat16)

        kn = _dot(ckv, wkn_ref[:, g * heads * LANES:(g + 1) * heads * LANES])
        for j in range(heads):
            h = g * heads + j
            km_ref[:, h * MLA_QK_PAD:h * MLA_QK_PAD + LANES] = kn[:, j * LANES:(j + 1) * LANES].astype(jnp.bfloat16)
            km_ref[:, h * MLA_QK_PAD + LANES:(h + 1) * MLA_QK_PAD] = kr

        cols = slice(g * heads * LANES, (g + 1) * heads * LANES)
        vmt_ref[cols, :] = lax.dot_general(wvt_ref[cols, :], ckv, _NT,
                                           preferred_element_type=jnp.float32).astype(jnp.bfloat16)


def _const_spec(shape):
    return pl.BlockSpec(shape, lambda i: (0,) * len(shape))


def _proj(x2, posf, gmix, win, wkr, wgate, bg, gq, gkv, wuq, wkn, wvat, wvt, freq):
    t = x2.shape[0]
    tm = TOKEN_TILE
    bf = jnp.bfloat16
    row = lambda w: pl.BlockSpec((tm, w), lambda i: (i, 0))
    col = lambda w: pl.BlockSpec((w, tm), lambda i: (0, i))
    consts = (gmix, win, wkr, wgate, bg, gq, gkv, wuq, wkn, wvat, wvt, freq)
    outs = ((row, D_MODEL), (row, D_MODEL), (col, D_MODEL), (row, 2 * D_MODEL),
            (row, MLA_HEADS * MLA_QK_PAD), (row, MLA_HEADS * MLA_QK_PAD), (col, MLA_HEADS * MLA_V))
    return pl.pallas_call(
        _proj_kernel,
        grid=(t // tm,),
        in_specs=[row(D_MODEL), row(1)] + [_const_spec(c.shape) for c in consts],
        out_specs=[kind(w) for kind, w in outs],
        out_shape=[jax.ShapeDtypeStruct((t, w) if kind is row else (w, t), bf) for kind, w in outs],
        compiler_params=pltpu.CompilerParams(
            dimension_semantics=("parallel",), vmem_limit_bytes=VMEM_LIMIT),
        name="proj",
    )(x2, posf, *consts)


def _pipelined_tiles(n_heads, n_tiles, n_chunks, score_chunk, max_merge, max_store, max_load,
                     value_chunk, write_out):
    def fused(nxt, slot_next, cur, slot_cur):
        m_cur = None if cur is None else max_load(slot_cur)
        m_next, acc = None, None
        for c in range(n_chunks):
            if nxt is not None:
                m_next = max_merge(m_next, score_chunk(nxt[0], nxt[1], slot_next, c))
            if cur is not None:
                acc = value_chunk(cur[0], slot_cur, c, m_cur, acc)
        if nxt is not None:
            max_store(slot_next, m_next)
        if cur is not None:
            write_out(cur[0], cur[1], acc)

    def unit(h, t):
        return h * n_tiles + t

    def ahead(h, t):
        u = unit(h, t) + 2
        return None if u >= n_heads * n_tiles else (u // n_tiles, u % n_tiles)

    fused((0, 0), 0, None, None)
    fused((0, 1), 1, None, None)
    n_triples = (n_tiles - 2) // SCORE_SLOTS
    for h in range(n_heads):
        def triple(j, carry, h=h):
            t = SCORE_SLOTS * j
            for i in range(SCORE_SLOTS):
                fused((h, t + i + 2), (unit(h, i) + 2) % SCORE_SLOTS, (h, t + i),
                      unit(h, i) % SCORE_SLOTS)
            return carry

        lax.fori_loop(0, n_triples, triple, 0)
        for t in range(SCORE_SLOTS * n_triples, n_tiles):
            fused(ahead(h, t), (unit(h, t) + 2) % SCORE_SLOTS, (h, t), unit(h, t) % SCORE_SLOTS)


def _q_rows(t):
    if isinstance(t, int):
        return pl.ds(t * Q_TILE, Q_TILE)
    return pl.ds(pl.multiple_of(t * Q_TILE, Q_TILE), Q_TILE)


def _key_cols(c):
    return slice(c * KEY_CHUNK, (c + 1) * KEY_CHUNK)


def _fold_keys(x, op):
    out = x[:SUBLANES]
    for j in range(1, x.shape[0] // SUBLANES):
        out = op(out, x[j * SUBLANES:(j + 1) * SUBLANES])
    return out


def _fill_values_ext(vx_ref, vt_ref):
    width = vt_ref.shape[0]
    vx_ref[:width, :] = vt_ref[...]
    sub = lax.broadcasted_iota(jnp.int32, (vx_ref.shape[0] - width, vt_ref.shape[1]), 0)
    vx_ref[width:, :] = jnp.where(sub == 0, 1.0, 0.0).astype(vx_ref.dtype)


def _prob_values(s_ref, vx_ref, c, m, acc):
    p = jnp.exp2(s_ref[_key_cols(c), :] - m).astype(jnp.bfloat16)
    part = _dot(vx_ref[:, _key_cols(c)], p)
    return part if acc is None else acc + part


def _dattn_kernel(q_ref, k_ref, vt_ref, lam_ref, gsub_ref, o_ref,
                  s0_ref, s1_ref, s2_ref, m0_ref, m1_ref, m2_ref, vx_ref, qm_ref):
    s_refs, m_refs = (s0_ref, s1_ref, s2_ref), (m0_ref, m1_ref, m2_ref)
    width = LANES
    lamv = lam_ref[...]
    lam = (jnp.exp(jnp.sum(lamv[0:1] * lamv[1:2], axis=-1, keepdims=True))
           - jnp.exp(jnp.sum(lamv[2:3] * lamv[3:4], axis=-1, keepdims=True)) + LAM_INIT)
    for h in range(ATTN_HEADS):
        hcols = slice(h * LANES, (h + 1) * LANES)
        _fill_values_ext(vx_ref.at[h], vt_ref.at[hcols, :])
        q_all = q_ref[:, hcols]
        lane = lax.broadcasted_iota(jnp.int32, q_all.shape, 1)
        qm_ref[h, 0] = jnp.where(lane < DA_HEAD_DIM, q_all, jnp.zeros_like(q_all))
        qm_ref[h, 1] = jnp.where(lane >= DA_HEAD_DIM, q_all, jnp.zeros_like(q_all))

    def score_chunk(h, t, slot, c):
        k = k_ref[_key_cols(c), h * LANES:(h + 1) * LANES]
        sc0 = lax.dot_general(k, qm_ref[h, 0, _q_rows(t), :], _NT, preferred_element_type=jnp.float32)
        sc1 = lax.dot_general(k, qm_ref[h, 1, _q_rows(t), :], _NT, preferred_element_type=jnp.float32)
        s_refs[slot][0, _key_cols(c), :] = sc0
        s_refs[slot][1, _key_cols(c), :] = sc1
        return _fold_keys(sc0, jnp.maximum), _fold_keys(sc1, jnp.maximum)

    def max_merge(m, mc):
        return mc if m is None else (jnp.maximum(m[0], mc[0]), jnp.maximum(m[1], mc[1]))

    def max_store(slot, m):
        m_refs[slot][0] = jnp.max(m[0], axis=0, keepdims=True)
        m_refs[slot][1] = jnp.max(m[1], axis=0, keepdims=True)

    def max_load(slot):
        return m_refs[slot][0], m_refs[slot][1]

    def value_chunk(h, slot, c, m, acc):
        acc0, acc1 = (None, None) if acc is None else acc
        return (_prob_values(s_refs[slot].at[0], vx_ref.at[h], c, m[0], acc0),
                _prob_values(s_refs[slot].at[1], vx_ref.at[h], c, m[1], acc1))

    def write_out(h, t, acc):
        ox0, ox1 = acc
        ot = (ox0[:width] * (1.0 / ox0[width:width + 1])
              - ox1[:width] * (lam / ox1[width:width + 1]))
        o = ot.T
        o_ref[_q_rows(t), h * LANES:(h + 1) * LANES] = (
            _rms(o, gsub_ref[...]) * (1.0 - LAM_INIT)).astype(o_ref.dtype)

    _pipelined_tiles(ATTN_HEADS, q_ref.shape[0] // Q_TILE, k_ref.shape[0] // KEY_CHUNK,
                     score_chunk, max_merge, max_store, max_load, value_chunk, write_out)


def _dattn(qa, ka, vat, lamv, gsub, batch, seq):
    head = lambda b, h: (b, h)
    hw = ATTN_HEADS * LANES
    return pl.pallas_call(
        _dattn_kernel,
        grid=(batch, DA_HEADS // ATTN_HEADS),
        in_specs=[
            pl.BlockSpec((seq, hw), head),
            pl.BlockSpec((seq, hw), head),
            pl.BlockSpec((hw, seq), lambda b, h: (h, b)),
            pl.BlockSpec(lamv.shape, lambda b, h: (0, 0)),
            pl.BlockSpec(gsub.shape, lambda b, h: (0, 0)),
        ],
        out_specs=pl.BlockSpec((seq, hw), head),
        out_shape=jax.ShapeDtypeStruct(qa.shape, jnp.bfloat16),
        scratch_shapes=[pltpu.VMEM((2, seq, Q_TILE), jnp.float32)] * SCORE_SLOTS
        + [pltpu.VMEM((2, 1, Q_TILE), jnp.float32)] * SCORE_SLOTS
        + [pltpu.VMEM((ATTN_HEADS, PV_ROWS, seq), jnp.bfloat16),
           pltpu.VMEM((ATTN_HEADS, 2, seq, LANES), jnp.bfloat16)],
        compiler_params=pltpu.CompilerParams(
            dimension_semantics=("parallel", "parallel"), vmem_limit_bytes=VMEM_LIMIT),
        name="dattn",
    )(qa, ka, vat, lamv, gsub)


def _mattn_kernel(q_ref, k_ref, vt_ref, o_ref, s0_ref, s1_ref, s2_ref, m0_ref, m1_ref, m2_ref,
                  vx_ref):
    s_refs, m_refs = (s0_ref, s1_ref, s2_ref), (m0_ref, m1_ref, m2_ref)
    width = MLA_V
    for h in range(ATTN_HEADS):
        _fill_values_ext(vx_ref.at[h], vt_ref.at[h * MLA_V:(h + 1) * MLA_V, :])

    def score_chunk(h, t, slot, c):
        qk = slice(h * MLA_QK_PAD, (h + 1) * MLA_QK_PAD)
        sc = lax.dot_general(k_ref[_key_cols(c), qk], q_ref[_q_rows(t), qk], _NT,
                             preferred_element_type=jnp.float32)
        s_refs[slot][_key_cols(c), :] = sc
        return _fold_keys(sc, jnp.maximum)

    def max_merge(m, mc):
        return mc if m is None else jnp.maximum(m, mc)

    def max_store(slot, m):
        m_refs[slot][...] = jnp.max(m, axis=0, keepdims=True)

    def max_load(slot):
        return m_refs[slot][...]

    def value_chunk(h, slot, c, m, acc):
        return _prob_values(s_refs[slot], vx_ref.at[h], c, m, acc)

    def write_out(h, t, ox):
        ot = ox[:width] * (1.0 / ox[width:width + 1])
        o_ref[_q_rows(t), h * MLA_V:(h + 1) * MLA_V] = ot.T.astype(o_ref.dtype)

    _pipelined_tiles(ATTN_HEADS, q_ref.shape[0] // Q_TILE, k_ref.shape[0] // KEY_CHUNK,
                     score_chunk, max_merge, max_store, max_load, value_chunk, write_out)


def _mattn(qm, km, vmt, batch, seq):
    head = lambda b, h: (b, h)
    return pl.pallas_call(
        _mattn_kernel,
        grid=(batch, MLA_HEADS // ATTN_HEADS),
        in_specs=[
            pl.BlockSpec((seq, ATTN_HEADS * MLA_QK_PAD), head),
            pl.BlockSpec((seq, ATTN_HEADS * MLA_QK_PAD), head),
            pl.BlockSpec((ATTN_HEADS * MLA_V, seq), lambda b, h: (h, b)),
        ],
        out_specs=pl.BlockSpec((seq, ATTN_HEADS * MLA_V), head),
        out_shape=jax.ShapeDtypeStruct((batch * seq, MLA_HEADS * MLA_V), jnp.bfloat16),
        scratch_shapes=[pltpu.VMEM((seq, Q_TILE), jnp.float32)] * SCORE_SLOTS
        + [pltpu.VMEM((1, Q_TILE), jnp.float32)] * SCORE_SLOTS
        + [pltpu.VMEM((ATTN_HEADS, PV_ROWS, seq), jnp.bfloat16)],
        compiler_params=pltpu.CompilerParams(
            dimension_semantics=("parallel", "parallel"), vmem_limit_bytes=VMEM_LIMIT),
        name="mattn",
    )(qm, km, vmt)


def _rows(parts, dtype):
    sub = lax.broadcasted_iota(jnp.int32, (len(parts), parts[0].shape[1]), 0)
    out = jnp.zeros(sub.shape, dtype)
    for k, p in enumerate(parts):
        out = jnp.where(sub == k, p.astype(dtype), out)
    return out


def _merge_kernel(x_ref, oa_ref, ob_ref, gates_ref, woa_ref, wob_ref, wout_ref, gffn_ref,
                  wrt_ref, brt_ref,
                  x1_ref, h2_ref, idx_ref, w_ref, rank_ref, counts_ref, carry_ref, hprev_ref):
    i = pl.program_id(0)

    @pl.when(i == 0)
    def _():
        carry_ref[...] = jnp.zeros_like(carry_ref)
        hprev_ref[...] = jnp.zeros_like(hprev_ref)

    gates = gates_ref[...].astype(jnp.float32)
    half = D_MODEL // 2

    logits = lax.dot_general(wrt_ref[...], hprev_ref[...], _NT, precision=lax.Precision.HIGHEST,
                             preferred_element_type=jnp.float32) + brt_ref[...]
    tm = logits.shape[1]
    sub = lax.broadcasted_iota(jnp.int32, logits.shape, 0)
    vals, idxs, hots = [], [], []
    l = logits

    def pick(l):
        m = jnp.max(l, axis=0, keepdims=True)
        idx = jnp.min(jnp.where(l == m, sub, N_EXPERTS), axis=0, keepdims=True)
        hot = sub == idx
        vals.append(m)
        idxs.append(idx)
        hots.append(hot)
        return jnp.where(hot, -jnp.inf, l)

    ya_lo = gates[:, :half] * _dot(oa_ref[...], woa_ref[:, :half])
    l = pick(l)
    ya_hi = gates[:, half:D_MODEL] * _dot(oa_ref[...], woa_ref[:, half:])
    l = pick(l)
    yb_lo = gates[:, D_MODEL:D_MODEL + half] * _dot(ob_ref[...], wob_ref[:, :half])
    l = pick(l)
    yb_hi = gates[:, D_MODEL + half:] * _dot(ob_ref[...], wob_ref[:, half:])
    l = pick(l)
    merged = jnp.concatenate([ya_lo + yb_lo, ya_hi + yb_hi], axis=1).astype(jnp.bfloat16)

    es = [jnp.exp(v - vals[0]) for v in vals]
    den = es[0] + es[1] + es[2] + es[3]
    w_ref[...] = _rows([e / den for e in es], jnp.float32)
    idx_ref[...] = _rows(idxs, jnp.int32)
    x1_lo = x_ref[:, :half] + _dot(merged, wout_ref[:, :half])

    chosen = (hots[0] | hots[1] | hots[2] | hots[3]).astype(jnp.float32)
    r_i = lax.broadcasted_iota(jnp.int32, (tm, tm), 0)
    c_i = lax.broadcasted_iota(jnp.int32, (tm, tm), 1)
    earlier = (r_i < c_i).astype(jnp.bfloat16)
    prefix = _dot(chosen.astype(jnp.bfloat16), earlier) + carry_ref[...]
    x1_hi = x_ref[:, half:] + _dot(merged, wout_ref[:, half:])
    ranks = [jnp.sum(jnp.where(h, prefix, 0.0), axis=0, keepdims=True) for h in hots]
    rank_ref[...] = _rows(ranks, jnp.int32)
    live = (i > 0).astype(jnp.float32)
    carry = carry_ref[...] + live * jnp.sum(chosen, axis=1, keepdims=True)
    carry_ref[...] = carry
    counts_ref[...] = carry.astype(jnp.int32)

    x1 = jnp.concatenate([x1_lo, x1_hi], axis=1)
    x1_ref[...] = x1
    h2 = _rms(x1, gffn_ref[...])
    h2_ref[...] = h2
    hprev_ref[...] = h2


def _merge(x2, oa, ob, gates, woa, wob, wout, gffn, wrt, brt):
    t = x2.shape[0]
    tm = TOKEN_TILE
    n = t // tm
    row = lambda w: pl.BlockSpec((tm, w), lambda i: (jnp.minimum(i, n - 1), 0))
    col = lambda: pl.BlockSpec((TOP_K, tm), lambda i: (0, jnp.maximum(i - 1, 0)))
    consts = (woa, wob, wout, gffn, wrt, brt)
    return pl.pallas_call(
        _merge_kernel,
        grid=(n + 1,),
        in_specs=[row(D_MODEL), row(D_MODEL), row(D_MODEL), row(2 * D_MODEL)]
        + [_const_spec(c.shape) for c in consts],
        out_specs=[row(D_MODEL), row(D_MODEL), col(), col(), col(),
                   _const_spec((N_EXPERTS, 1))],
        out_shape=[
            jax.ShapeDtypeStruct((t, D_MODEL), jnp.float32),
            jax.ShapeDtypeStruct((t, D_MODEL), jnp.float32),
            jax.ShapeDtypeStruct((TOP_K, t), jnp.int32),
            jax.ShapeDtypeStruct((TOP_K, t), jnp.float32),
            jax.ShapeDtypeStruct((TOP_K, t), jnp.int32),
            jax.ShapeDtypeStruct((N_EXPERTS, 1), jnp.int32),
        ],
        scratch_shapes=[pltpu.VMEM((N_EXPERTS, 1), jnp.float32),
                        pltpu.VMEM((tm, D_MODEL), jnp.float32)],
        compiler_params=pltpu.CompilerParams(
            dimension_semantics=("arbitrary",), vmem_limit_bytes=VMEM_LIMIT),
        name="merge",
    )(x2, oa, ob, gates, *consts)


def _row_copy_wait(src_rows_ref, dst_rows_ref, sem, n):
    for _ in range(n):
        pltpu.make_async_copy(src_rows_ref, dst_rows_ref, sem).wait()


def _dispatch_kernel(dest_ref, pad_lo_ref, pad_hi_ref, nu_ref, h2_ref, xs_ref, zero_ref, sem, zsem):
    i = pl.program_id(0)
    tm = h2_ref.shape[0]
    bm = zero_ref.shape[0]
    n_blocks = xs_ref.shape[0] // bm

    @pl.when(i == 0)
    def _():
        zero_ref[...] = jnp.zeros_like(zero_ref)

        def pad_rows(fn):
            def per_expert(e, c):
                return lax.fori_loop(pad_lo_ref[e], pad_hi_ref[e], fn, c)
            lax.fori_loop(0, N_EXPERTS, per_expert, 0)

        def row_copy(j):
            return pltpu.make_async_copy(zero_ref.at[pl.ds(0, 1)], xs_ref.at[pl.ds(j, 1)], zsem)

        def blk_copy(b):
            return pltpu.make_async_copy(zero_ref, xs_ref.at[pl.ds(pl.multiple_of(b * bm, bm), bm)], zsem)

        def start_row(j, c):
            row_copy(j).start()
            return c

        def wait_row(j, c):
            row_copy(j).wait()
            return c

        def start_blk(b, c):
            blk_copy(b).start()
            return c

        def wait_blk(b, c):
            blk_copy(b).wait()
            return c

        pad_rows(start_row)
        lax.fori_loop(nu_ref[0], n_blocks, start_blk, 0)
        pad_rows(wait_row)
        lax.fori_loop(nu_ref[0], n_blocks, wait_blk, 0)

    def body(r, c):
        base = (i * tm + r) * TOP_K
        for k in range(TOP_K):
            d = dest_ref[base + k]
            pltpu.make_async_copy(h2_ref.at[pl.ds(r, 1)], xs_ref.at[pl.ds(d, 1)], sem).start(
                priority=k % 2)
        return c

    lax.fori_loop(0, tm, body, 0, unroll=ROW_DMA_UNROLL)
    _row_copy_wait(h2_ref, xs_ref.at[pl.ds(0, tm)], sem, TOP_K)


def _dispatch(dest, pad_lo, pad_hi, n_used, h2, n_blocks):
    t = h2.shape[0]
    tm = DISPATCH_TILE
    bm = ROW_BLOCK
    return pl.pallas_call(
        _dispatch_kernel,
        grid_spec=pltpu.PrefetchScalarGridSpec(
            num_scalar_prefetch=4,
            grid=(t // tm,),
            in_specs=[pl.BlockSpec((tm, D_MODEL), lambda i, *_: (i, 0))],
            out_specs=pl.BlockSpec(memory_space=pl.ANY),
            scratch_shapes=[pltpu.VMEM((bm, D_MODEL), jnp.float32),
                            pltpu.SemaphoreType.DMA(()), pltpu.SemaphoreType.DMA(())],
        ),
        out_shape=jax.ShapeDtypeStruct((n_blocks * bm, D_MODEL), jnp.float32),
        compiler_params=pltpu.CompilerParams(
            dimension_semantics=("arbitrary",), vmem_limit_bytes=VMEM_LIMIT),
        name="dispatch",
    )(dest, pad_lo, pad_hi, n_used, h2)


def _experts_kernel(be_ref, nu_ref, nxt_ref, grp_ref, xs_ref, wg_ref, bg_ref, wu_ref, bu_ref, wd_ref,
                    bd_ref, ys_ref, stage_ref, wbf_ref, sem):
    w_hbm = (wg_ref, wu_ref, wd_ref)
    bm = ROW_BLOCK

    def weight_copies(expert, slot):
        return [pltpu.make_async_copy(w.at[expert], stage_ref.at[slot, j], sem.at[slot, j])
                for j, w in enumerate(w_hbm)]

    def row_block(b, rows):
        e = be_ref[b]
        slot = lax.rem(grp_ref[b], 2)

        @pl.when(b == 0)
        def _():
            for cp in weight_copies(e, slot):
                cp.start()

        @pl.when(jnp.logical_or(b == 0, e != be_ref[jnp.maximum(b - 1, 0)]))
        def _():
            for j, cp in enumerate(weight_copies(e, slot)):
                cp.wait()
                wbf_ref[j] = stage_ref[slot, j].astype(jnp.bfloat16)

            @pl.when(nxt_ref[b] >= 0)
            def _():
                for cp in weight_copies(nxt_ref[b], 1 - slot):
                    cp.start()

        @pl.when(b >= nu_ref[0])
        def _():
            ys_ref[rows, :] = jnp.zeros((bm, ys_ref.shape[1]), ys_ref.dtype)

        @pl.when(b < nu_ref[0])
        def _():
            xb = xs_ref[rows, :].astype(jnp.bfloat16)
            gate = jnp.minimum(_dot(xb, wbf_ref[0]) + bg_ref[e], SWIGLU_LIMIT)
            up = jnp.clip(_dot(xb, wbf_ref[1]) + bu_ref[e], -SWIGLU_LIMIT, SWIGLU_LIMIT)
            act = (up + 1.0) * (gate * jax.nn.sigmoid(SWIGLU_ALPHA * gate))
            ys_ref[rows, :] = _dot(act.astype(jnp.bfloat16), wbf_ref[2]) + bd_ref[e]

    for j in range(BLOCKS_PER_STEP):
        row_block(pl.program_id(0) * BLOCKS_PER_STEP + j, slice(j * bm, (j + 1) * bm))


def _experts(block_e, n_used, next_e, group, xs, wg, bg, wu, bu, wd, bd, n_blocks):
    rows_per_step = ROW_BLOCK * BLOCKS_PER_STEP
    rows = lambda s, be, nu, *_: (jnp.maximum(jnp.minimum(s, (nu[0] - 1) // BLOCKS_PER_STEP), 0), 0)
    hbm = lambda: pl.BlockSpec(memory_space=pl.ANY)
    bias = lambda: pl.BlockSpec((N_EXPERTS, 1, D_FF), lambda s, *_: (0, 0, 0))
    return pl.pallas_call(
        _experts_kernel,
        grid_spec=pltpu.PrefetchScalarGridSpec(
            num_scalar_prefetch=4,
            grid=(n_blocks // BLOCKS_PER_STEP,),
            in_specs=[pl.BlockSpec((rows_per_step, D_MODEL), rows), hbm(), bias(), hbm(), bias(),
                      hbm(), bias()],
            out_specs=pl.BlockSpec((rows_per_step, D_MODEL), lambda s, *_: (s, 0)),
            scratch_shapes=[pltpu.VMEM((2, 3, D_MODEL, D_FF), jnp.float32),
                            pltpu.VMEM((3, D_MODEL, D_FF), jnp.bfloat16),
                            pltpu.SemaphoreType.DMA((2, 3))],
        ),
        out_shape=jax.ShapeDtypeStruct((n_blocks * ROW_BLOCK, D_MODEL), jnp.float32),
        compiler_params=pltpu.CompilerParams(
            dimension_semantics=("arbitrary",), vmem_limit_bytes=VMEM_LIMIT),
        name="experts",
    )(block_e, n_used, next_e, group, xs, wg, bg, wu, bu, wd, bd)


def _combine_kernel(dest_ref, ys_ref, x1_ref, w_ref, gfin_ref, o_ref, buf_ref, sem):
    i = pl.program_id(0)
    tm = x1_ref.shape[0]

    def gather(tile, slot):
        def body(r, c):
            base = (tile * tm + r) * TOP_K
            for k in range(TOP_K):
                d = dest_ref[base + k]
                pltpu.make_async_copy(ys_ref.at[pl.ds(d, 1)], buf_ref.at[slot, k, pl.ds(r, 1)],
                                      sem.at[slot]).start(priority=k % 2)
            return c

        lax.fori_loop(0, tm, body, 0, unroll=ROW_DMA_UNROLL)

    @pl.when(i == 0)
    def _():
        gather(i, 0)

    for next_slot in range(2):
        @pl.when(jnp.logical_and(i + 1 < pl.num_programs(0), lax.rem(i + 1, 2) == next_slot))
        def _():
            gather(i + 1, next_slot)

    slot = lax.rem(i, 2)
    _row_copy_wait(ys_ref.at[pl.ds(0, tm)], buf_ref.at[slot, 0], sem.at[slot], TOP_K)
    w = w_ref[...]
    y = x1_ref[...]
    for k in range(TOP_K):
        y = y + buf_ref[slot, k] * w[:, k:k + 1]
    o_ref[...] = _rms(y, gfin_ref[...])


def _combine(dest, ys, x1, top_w, gfin):
    t = x1.shape[0]
    tm = DISPATCH_TILE
    return pl.pallas_call(
        _combine_kernel,
        grid_spec=pltpu.PrefetchScalarGridSpec(
            num_scalar_prefetch=1,
            grid=(t // tm,),
            in_specs=[pl.BlockSpec(memory_space=pl.ANY),
                      pl.BlockSpec((tm, D_MODEL), lambda i, d: (i, 0)),
                      pl.BlockSpec((tm, TOP_K), lambda i, d: (i, 0)),
                      pl.BlockSpec((1, D_MODEL), lambda i, d: (0, 0))],
            out_specs=pl.BlockSpec((tm, D_MODEL), lambda i, d: (i, 0)),
            scratch_shapes=[pltpu.VMEM((2, TOP_K, tm, D_MODEL), jnp.float32),
                            pltpu.SemaphoreType.DMA((2,))],
        ),
        out_shape=jax.ShapeDtypeStruct((t, D_MODEL), jnp.float32),
        compiler_params=pltpu.CompilerParams(
            dimension_semantics=("arbitrary",), vmem_limit_bytes=VMEM_LIMIT),
        name="combine",
    )(dest, ys, x1, top_w, gfin)


def _rope_freq_lanes():
    def inv(rot):
        return ROPE_THETA ** (-jnp.arange(0, rot, 2, dtype=jnp.float32) / rot)
    lane = jnp.arange(LANES)
    f_mla = inv(MLA_ROPE)[lane % (MLA_ROPE // 2)]
    f_diff = inv(DA_ROT)[lane % (DA_ROT // 2)]
    freq = jnp.where(lane < MLA_ROPE, f_mla, jnp.where(lane < MLA_ROPE + DA_ROT, f_diff, 0.0))
    return freq.astype(jnp.float32)[None]


def kernel(x, positions, g_mix, w_in, lam_q1, lam_k1, lam_q2, lam_k2, g_subln, g_q, g_kv, w_uq, w_ukv, w_o_diff, w_o_mla, b_gates, w_out, g_ffn, w_router, b_router, w_gate, b_gate, w_up, b_up, w_down, b_down, g_final):
    batch, seq, d = x.shape
    t = batch * seq
    bf = jnp.bfloat16
    l = 0
    x2 = x.reshape(t, d)
    posf = positions.astype(jnp.float32).reshape(t, 1)

    w = w_in[l]
    win = w[:, :IN_KR].astype(bf)
    wkr = jnp.pad(w[:, IN_KR:IN_GATES], ((0, 0), (0, LANES - MLA_ROPE))).astype(bf)
    wgate = w[:, IN_GATES:].astype(bf)
    wuq = jnp.pad(w_uq[l].reshape(MLA_Q_LORA, MLA_HEADS, MLA_NOPE + MLA_ROPE),
                  ((0, 0), (0, 0), (0, MLA_QK_PAD - MLA_NOPE - MLA_ROPE))
                  ).reshape(MLA_Q_LORA, MLA_HEADS * MLA_QK_PAD).astype(bf)
    wukv = w_ukv[l].reshape(MLA_KV_LORA, MLA_HEADS, MLA_NOPE + MLA_V)
    wkn = wukv[:, :, :MLA_NOPE].reshape(MLA_KV_LORA, MLA_HEADS * MLA_NOPE).astype(bf)
    wvt = wukv[:, :, MLA_NOPE:].reshape(MLA_KV_LORA, MLA_HEADS * MLA_V).T.astype(bf)
    wvat = w[:, IN_OFF["va"]:IN_OFF["va"] + IN_W["va"]].T.astype(bf)
    freq = _rope_freq_lanes()

    qa, ka, vat, gates, qm, km, vmt = _proj(
        x2, posf, g_mix[l][None], win, wkr, wgate, b_gates[l][None],
        g_q[l][None], g_kv[l][None], wuq, wkn, wvat, wvt, freq)

    lamv = jnp.stack([lam_q1[l], lam_k1[l], lam_q2[l], lam_k2[l]]).astype(jnp.float32)
    oa = _dattn(qa, ka, vat, lamv, g_subln[l][None], batch, seq)
    ob = _mattn(qm, km, vmt, batch, seq)

    x1, h2, e_idx, top_w, rank, counts = _merge(
        x2, oa, ob, gates, w_o_diff[l].astype(bf), w_o_mla[l].astype(bf), w_out[l].astype(bf),
        g_ffn[l][None], w_router[l].T, b_router[l][:, None])

    bm = ROW_BLOCK
    n_blocks = (t * TOP_K) // bm + N_EXPERTS
    counts = counts[:, 0]
    padded = (counts + bm - 1) // bm * bm
    padded_end = jnp.cumsum(padded)
    padded_start = padded_end - padded
    n_used = (padded_end[-1] // bm).astype(jnp.int32)
    blk = jnp.minimum(jnp.arange(n_blocks, dtype=jnp.int32), n_used - 1)
    block_e = jnp.minimum(jnp.sum(padded_end[None, :] <= (blk * bm)[:, None], axis=1),
                          N_EXPERTS - 1).astype(jnp.int32)
    hot = e_idx[:, :, None] == jnp.arange(N_EXPERTS, dtype=jnp.int32)
    dest = (jnp.sum(jnp.where(hot, padded_start, 0), axis=-1) + rank).T.reshape(-1).astype(jnp.int32)

    n_used = n_used.reshape(1)
    xs = _dispatch(dest, (padded_start + counts).astype(jnp.int32), padded_end.astype(jnp.int32),
                   n_used, h2, n_blocks)
    later = block_e[None, :] > block_e[:, None]
    next_e = jnp.min(jnp.where(later, block_e[None, :], N_EXPERTS), axis=1)
    next_e = jnp.where(next_e == N_EXPERTS, -1, next_e).astype(jnp.int32)
    group = jnp.cumsum(jnp.concatenate([jnp.zeros((1,), jnp.int32),
                                        (block_e[1:] != block_e[:-1]).astype(jnp.int32)]))
    ys = _experts(block_e, n_used, next_e, group.astype(jnp.int32), xs, w_gate[l],
                  b_gate[l][:, None, :], w_up[l], b_up[l][:, None, :], w_down[l],
                  b_down[l][:, None, :], n_blocks)
    out = _combine(dest, ys, x1, top_w.T, g_final[None])
    return out.reshape(batch, seq, d)
```
